```python
import jax, jax.numpy as jnp
from jax import lax
import numpy as np

D_MODEL = 2048
BATCH = 4
SEQ = 2048
DEPTH = 1

CHUNK = 64
Q_BLOCK = 128
SB_HEADS = 8
HEAD_DIM = 128
SB_WIDTH = SB_HEADS * HEAD_DIM
MLA_HEADS = 8
MLA_Q_RANK = 512
MLA_KV_RANK = 256
MLA_NOPE = 128
MLA_ROPE = 64
MLA_V = 128
MLA_WIDTH = MLA_HEADS * MLA_V
MIX_WIDTH = SB_WIDTH + MLA_WIDTH
IN_PROJ = 3 * SB_WIDTH + MLA_Q_RANK + MLA_KV_RANK + MLA_ROPE
ROPE_THETA = 10000.0
N_EXPERTS = 32
TOP_K = 4
D_FF = 2048
SWIGLU_LIMIT = 7.0
SWIGLU_ALPHA = 1.702
MOE_BLOCK = 128
LN_EPS = 1e-5
RMS_EPS = 1e-6
DEEPNORM_ALPHA = (2 * DEPTH) ** 0.25
DEEPNORM_BETA = (8 * DEPTH) ** -0.25

kernel_name = 'hybrid_sb_mla_moe_deepnorm'


def layer_norm(x, g, b):
    xf = x.astype(jnp.float32)
    mu = jnp.mean(xf, -1, keepdims=True)
    xc = xf - mu
    var = jnp.mean(xc * xc, -1, keepdims=True)
    return (xc * lax.rsqrt(var + LN_EPS) * g.astype(jnp.float32) + b.astype(jnp.float32)).astype(x.dtype)


def rms_norm(x, g):
    xf = x.astype(jnp.float32)
    ms = jnp.mean(xf * xf, -1, keepdims=True)
    return (xf * lax.rsqrt(ms + RMS_EPS) * g.astype(jnp.float32)).astype(x.dtype)


def rope(x, positions):
    half = x.shape[-1] // 2
    inv_freq = ROPE_THETA ** (-(jnp.arange(half, dtype=jnp.float32) * 2.0 / x.shape[-1]))
    ang = positions[..., None].astype(jnp.float32) * inv_freq
    cos = jnp.cos(ang)[:, :, None, :]
    sin = jnp.sin(ang)[:, :, None, :]
    xf = x.astype(jnp.float32)
    x1, x2 = xf[..., :half], xf[..., half:]
    return jnp.concatenate([x1 * cos - x2 * sin, x2 * cos + x1 * sin], -1).astype(x.dtype)


def stick_breaking_attention(q, k, v):
    S = q.shape[1]
    scale = q.shape[-1] ** -0.5
    outs = []
    for qb in range(S // Q_BLOCK):
        q0 = qb * Q_BLOCK
        kend = q0 + Q_BLOCK
        z = jnp.einsum('bqhd,bkhd->bhqk', q[:, q0:kend], k[:, :kend],
                       preferred_element_type=jnp.float32) * scale
        t_idx = (q0 + jnp.arange(Q_BLOCK))[:, None]
        s_idx = jnp.arange(kend)[None, :]
        past = s_idx < t_idx
        neg_log_keep = jnp.where(past, jax.nn.softplus(z), 0.0)
        between = lax.cumsum(neg_log_keep, axis=3, reverse=True) - neg_log_keep
        w = jnp.where(past, jnp.exp(jax.nn.log_sigmoid(z) - between), 0.0)
        outs.append(jnp.einsum('bhqk,bkhd->bqhd', w.astype(v.dtype), v[:, :kend]))
    return jnp.concatenate(outs, axis=1)


def chunk_causal_softmax_attention(q, k, v):
    S = q.shape[1]
    scale = q.shape[-1] ** -0.5
    outs = []
    for qb in range(S // Q_BLOCK):
        q0 = qb * Q_BLOCK
        kend = q0 + Q_BLOCK
        s = jnp.einsum('bqhd,bkhd->bhqk', q[:, q0:kend], k[:, :kend],
                       preferred_element_type=jnp.float32) * scale
        t_chunk = (q0 + jnp.arange(Q_BLOCK))[:, None] // CHUNK
        s_chunk = jnp.arange(kend)[None, :] // CHUNK
        s = jnp.where(s_chunk <= t_chunk, s, -jnp.inf)
        p = jax.nn.softmax(s, axis=-1)
        outs.append(jnp.einsum('bhqk,bkhd->bqhd', p.astype(v.dtype), v[:, :kend]))
    return jnp.concatenate(outs, axis=1)


def hybrid_mixer(h, positions, w_in, q_a_norm, w_q_b, kv_a_norm, w_kv_b, sb_out_norm, mla_out_norm, w_o):
    B, S, _ = h.shape
    u = h @ w_in
    splits = [SB_WIDTH, 2 * SB_WIDTH, 3 * SB_WIDTH,
              3 * SB_WIDTH + MLA_Q_RANK, 3 * SB_WIDTH + MLA_Q_RANK + MLA_KV_RANK]
    sb_q, sb_k, sb_v, q_lat, kv_lat, k_rope = jnp.split(u, splits, axis=-1)
    hs = (B, S, SB_HEADS, HEAD_DIM)
    sb_o = stick_breaking_attention(sb_q.reshape(hs), sb_k.reshape(hs), sb_v.reshape(hs))
    sb_o = rms_norm(sb_o.reshape(B, S, SB_WIDTH), sb_out_norm)
    q = (rms_norm(q_lat, q_a_norm) @ w_q_b).reshape(B, S, MLA_HEADS, MLA_NOPE + MLA_ROPE)
    q_nope, q_pe = q[..., :MLA_NOPE], rope(q[..., MLA_NOPE:], positions)
    kv = (rms_norm(kv_lat, kv_a_norm) @ w_kv_b).reshape(B, S, MLA_HEADS, MLA_NOPE + MLA_V)
    k_nope, mla_v = kv[..., :MLA_NOPE], kv[..., MLA_NOPE:]
    k_pe = jnp.broadcast_to(rope(k_rope[:, :, None, :], positions), (B, S, MLA_HEADS, MLA_ROPE))
    mla_q = jnp.concatenate([q_nope, q_pe], -1)
    mla_k = jnp.concatenate([k_nope, k_pe], -1)
    mla_o = chunk_causal_softmax_attention(mla_q, mla_k, mla_v)
    mla_o = rms_norm(mla_o.reshape(B, S, MLA_WIDTH), mla_out_norm)
    return jnp.concatenate([sb_o, mla_o], -1) @ w_o


def moe_ffn(h, w_router, b_router, w_gate_up, b_gate_up, w_down, b_down):
    B, S, D = h.shape
    T = B * S
    xf = h.reshape(T, D)
    logits = (xf @ w_router + b_router).astype(jnp.float32)
    top_vals, top_idx = lax.top_k(logits, TOP_K)
    gates = jax.nn.softmax(top_vals, axis=-1)
    n_assign = T * TOP_K
    e_flat = top_idx.reshape(-1)
    tok_flat = jnp.arange(n_assign, dtype=jnp.int32) // TOP_K
    gate_flat = gates.reshape(-1)
    order = jnp.argsort(e_flat)
    e_sorted = e_flat[order]
    counts = jnp.bincount(e_flat, length=N_EXPERTS)
    starts = jnp.cumsum(counts) - counts
    padded = ((counts + MOE_BLOCK - 1) // MOE_BLOCK) * MOE_BLOCK
    padded_ends = jnp.cumsum(padded)
    padded_starts = padded_ends - padded
    rank = jnp.arange(n_assign, dtype=jnp.int32) - starts[e_sorted]
    dest = padded_starts[e_sorted] + rank
    n_rows = n_assign + N_EXPERTS * MOE_BLOCK
    n_blocks = n_rows // MOE_BLOCK
    row_tok = jnp.zeros((n_rows,), jnp.int32).at[dest].set(tok_flat[order])
    row_gate = jnp.zeros((n_rows,), jnp.float32).at[dest].set(gate_flat[order])
    block_expert = jnp.minimum(
        jnp.searchsorted(padded_ends, jnp.arange(n_blocks, dtype=jnp.int32) * MOE_BLOCK, side='right'),
        N_EXPERTS - 1)
    x_rows = xf[row_tok].reshape(n_blocks, MOE_BLOCK, D)

    def expert_block(args):
        xb, e = args
        gu = xb @ w_gate_up[e] + b_gate_up[e]
        g, up = gu[:, :D_FF], gu[:, D_FF:]
        g = jnp.minimum(g, SWIGLU_LIMIT)
        up = jnp.clip(up, -SWIGLU_LIMIT, SWIGLU_LIMIT)
        act = (up + 1.0) * (g * jax.nn.sigmoid(SWIGLU_ALPHA * g))
        return act @ w_down[e] + b_down[e]

    y_rows = lax.map(expert_block, (x_rows, block_expert)).reshape(n_rows, D)
    out = jnp.zeros((T, D), jnp.float32).at[row_tok].add(y_rows.astype(jnp.float32) * row_gate[:, None])
    return out.astype(h.dtype).reshape(B, S, D)


def setup_inputs(seed: int = 0) -> dict:
    key = jax.random.key(seed)
    ks = jax.random.split(key, 24)
    f32 = jnp.float32

    def nrm(k, shape, scale):
        return jax.random.normal(k, shape, f32) * scale

    x = jax.random.normal(ks[0], (BATCH, SEQ, D_MODEL), f32)
    offs = jax.random.randint(ks[1], (BATCH, 1), 0, 64, dtype=jnp.int32) * CHUNK
    positions = offs + jnp.arange(SEQ, dtype=jnp.int32)[None, :]
    ln_in_g = 1.0 + nrm(ks[2], (D_MODEL,), 0.02)
    ln_in_b = nrm(ks[3], (D_MODEL,), 0.02)
    in_scale = jnp.ones((IN_PROJ,), f32).at[2 * SB_WIDTH:3 * SB_WIDTH].set(DEEPNORM_BETA)
    w_in = nrm(ks[4], (DEPTH, D_MODEL, IN_PROJ), D_MODEL ** -0.5) * in_scale
    q_a_norm = 1.0 + nrm(ks[5], (DEPTH, MLA_Q_RANK), 0.02)
    w_q_b = nrm(ks[6], (DEPTH, MLA_Q_RANK, MLA_HEADS * (MLA_NOPE + MLA_ROPE)), MLA_Q_RANK ** -0.5)
    kv_a_norm = 1.0 + nrm(ks[7], (DEPTH, MLA_KV_RANK), 0.02)
    kv_scale = jnp.tile(jnp.concatenate([jnp.ones((MLA_NOPE,), f32),
                                         jnp.full((MLA_V,), DEEPNORM_BETA, f32)]), MLA_HEADS)
    w_kv_b = nrm(ks[8], (DEPTH, MLA_KV_RANK, MLA_HEADS * (MLA_NOPE + MLA_V)), MLA_KV_RANK ** -0.5) * kv_scale
    sb_out_norm = 1.0 + nrm(ks[9], (DEPTH, SB_WIDTH), 0.02)
    mla_out_norm = 1.0 + nrm(ks[10], (DEPTH, MLA_WIDTH), 0.02)
    w_o = nrm(ks[11], (DEPTH, MIX_WIDTH, D_MODEL), MIX_WIDTH ** -0.5 * DEEPNORM_BETA)
    ln_mix_g = 1.0 + nrm(ks[12], (DEPTH, D_MODEL), 0.02)
    ln_mix_b = nrm(ks[13], (DEPTH, D_MODEL), 0.02)
    w_router = nrm(ks[14], (DEPTH, D_MODEL, N_EXPERTS), D_MODEL ** -0.5)
    b_router = nrm(ks[15], (DEPTH, N_EXPERTS), 0.01)
    w_gate_up = nrm(ks[16], (DEPTH, N_EXPERTS, D_MODEL, 2 * D_FF), D_MODEL ** -0.5 * DEEPNORM_BETA)
    b_gate_up = nrm(ks[17], (DEPTH, N_EXPERTS, 2 * D_FF), 0.02)
    w_down = nrm(ks[18], (DEPTH, N_EXPERTS, D_FF, D_MODEL), D_FF ** -0.5 * DEEPNORM_BETA)
    b_down = nrm(ks[19], (DEPTH, N_EXPERTS, D_MODEL), 0.02)
    ln_ffn_g = 1.0 + nrm(ks[20], (DEPTH, D_MODEL), 0.02)
    ln_ffn_b = nrm(ks[21], (DEPTH, D_MODEL), 0.02)
    return {'x': x, 'positions': positions, 'ln_in_g': ln_in_g, 'ln_in_b': ln_in_b,
            'w_in': w_in, 'q_a_norm': q_a_norm, 'w_q_b': w_q_b, 'kv_a_norm': kv_a_norm,
            'w_kv_b': w_kv_b, 'sb_out_norm': sb_out_norm, 'mla_out_norm': mla_out_norm,
            'w_o': w_o, 'ln_mix_g': ln_mix_g, 'ln_mix_b': ln_mix_b,
            'w_router': w_router, 'b_router': b_router, 'w_gate_up': w_gate_up,
            'b_gate_up': b_gate_up, 'w_down': w_down, 'b_down': b_down,
            'ln_ffn_g': ln_ffn_g, 'ln_ffn_b': ln_ffn_b}


def reference(x, positions, ln_in_g, ln_in_b, w_in, q_a_norm, w_q_b, kv_a_norm, w_kv_b,
              sb_out_norm, mla_out_norm, w_o, ln_mix_g, ln_mix_b, w_router, b_router,
              w_gate_up, b_gate_up, w_down, b_down, ln_ffn_g, ln_ffn_b):
    h = layer_norm(x, ln_in_g, ln_in_b)
    for l in range(DEPTH):
        mix = hybrid_mixer(h, positions, w_in[l], q_a_norm[l], w_q_b[l], kv_a_norm[l], w_kv_b[l],
                           sb_out_norm[l], mla_out_norm[l], w_o[l])
        h = layer_norm(DEEPNORM_ALPHA * h + mix, ln_mix_g[l], ln_mix_b[l])
        ffn = moe_ffn(h, w_router[l], b_router[l], w_gate_up[l], b_gate_up[l], w_down[l], b_down[l])
        h = layer_norm(DEEPNORM_ALPHA * h + ffn, ln_ffn_g[l], ln_ffn_b[l])
    return h
```

```python
import functools

import jax
import jax.numpy as jnp
from jax import lax
from jax.experimental import pallas as pl
from jax.experimental.pallas import tpu as pltpu

D_MODEL = 2048
DEPTH = 1
CHUNK = 64
SB_HEADS = 8
HEAD_DIM = 128
SB_WIDTH = SB_HEADS * HEAD_DIM
MLA_HEADS = 8
MLA_Q_RANK = 512
MLA_KV_RANK = 256
MLA_NOPE = 128
MLA_ROPE = 64
MLA_V = 128
MLA_WIDTH = MLA_HEADS * MLA_V
MLA_QK_PAD = 256
ROPE_THETA = 10000.0
N_EXPERTS = 32
TOP_K = 4
D_FF = 2048
SWIGLU_LIMIT = 7.0
SWIGLU_ALPHA = 1.702
MOE_BLOCK = 128
LN_EPS = 1e-5
RMS_EPS = 1e-6
DEEPNORM_ALPHA = (2 * DEPTH) ** 0.25

LANES = 128
HALF = D_MODEL // 2
VMEM_LIMIT_BYTES = 56 * 1024 * 1024

BF16 = jnp.bfloat16
F32 = jnp.float32
U32 = jnp.uint32


def _cparams(n_axes, vmem=None):
    return pltpu.CompilerParams(dimension_semantics=("arbitrary",) * n_axes,
                                vmem_limit_bytes=vmem or VMEM_LIMIT_BYTES)


def _layer_norm(x, g, b):
    mu = jnp.mean(x, -1, keepdims=True)
    xc = x - mu
    var = jnp.mean(xc * xc, -1, keepdims=True)
    return xc * lax.rsqrt(var + LN_EPS) * g + b


def _rms_norm(x, g):
    ms = jnp.mean(x * x, -1, keepdims=True)
    return x * lax.rsqrt(ms + RMS_EPS) * g


def _pack_halves(lo_f32, hi_f32):
    lo = lax.bitcast_convert_type(lo_f32.astype(BF16).astype(F32), U32)
    hi = lax.bitcast_convert_type(hi_f32.astype(BF16).astype(F32), U32)
    return lax.shift_right_logical(lo, U32(16)) | (hi & U32(0xFFFF0000))


def _unpack_halves(w):
    lo = lax.bitcast_convert_type(lax.shift_left(w, U32(16)), F32)
    hi = lax.bitcast_convert_type(w & U32(0xFFFF0000), F32)
    return lo, hi


def _ln_inproj_kernel(x_ref, g_ref, b_ref, w_ref, u_ref, h_ref):
    @pl.when(pl.program_id(1) == 0)
    def _():
        h_ref[...] = _layer_norm(x_ref[...], g_ref[...], b_ref[...]).astype(BF16)

    u_ref[...] = jnp.dot(h_ref[...], w_ref[...], preferred_element_type=F32).astype(BF16)


def _ln_inproj(x2, g, b, w_sb, tm=512, tn=1024):
    t, d = x2.shape
    n = w_sb.shape[1]
    return pl.pallas_call(
        _ln_inproj_kernel,
        grid=(t // tm, n // tn),
        in_specs=[pl.BlockSpec((tm, d), lambda i, j: (i, 0)),
                  pl.BlockSpec((1, d), lambda i, j: (0, 0)),
                  pl.BlockSpec((1, d), lambda i, j: (0, 0)),
                  pl.BlockSpec((d, tn), lambda i, j: (0, j))],
        out_specs=[pl.BlockSpec((tm, tn), lambda i, j: (i, j)),
                   pl.BlockSpec((tm, d), lambda i, j: (i, 0))],
        out_shape=[jax.ShapeDtypeStruct((t, n), BF16), jax.ShapeDtypeStruct((t, d), BF16)],
        compiler_params=_cparams(2),
        name="ln_inproj",
    )(x2, g, b, w_sb)


def _rope_slab(x, cos, sin_neg, sin_pos):
    return (x * cos + pltpu.roll(x, LANES - MLA_ROPE // 2, 1) * sin_neg
            + pltpu.roll(x, MLA_ROPE // 2, 1) * sin_pos)


def _mla_prep_kernel(h_ref, pos_ref, invf_ref, wl_ref, qg_ref, kg_ref, wqn_ref, wqp_ref,
                     wkn_ref, wv_ref, q_ref, k_ref, v_ref):
    lat = jnp.dot(h_ref[...], wl_ref[...], preferred_element_type=F32)
    qn = _rms_norm(lat[:, :MLA_Q_RANK], qg_ref[...]).astype(BF16)
    kvn = _rms_norm(lat[:, MLA_Q_RANK:MLA_Q_RANK + MLA_KV_RANK], kg_ref[...]).astype(BF16)
    k_rope = lat[:, MLA_Q_RANK + MLA_KV_RANK:]

    ang = pos_ref[...] * invf_ref[...]
    cos = jnp.cos(ang)
    sin = jnp.sin(ang)
    lane = lax.broadcasted_iota(jnp.int32, ang.shape, 1)
    sin_neg = jnp.where(lane < MLA_ROPE // 2, -sin, 0.0)
    sin_pos = jnp.where((lane >= MLA_ROPE // 2) & (lane < MLA_ROPE), sin, 0.0)

    q_nope = jnp.dot(qn, wqn_ref[...], preferred_element_type=F32)
    q_pe = jnp.dot(qn, wqp_ref[...], preferred_element_type=F32)
    k_nope = jnp.dot(kvn, wkn_ref[...], preferred_element_type=F32)
    v_ref[...] = jnp.dot(kvn, wv_ref[...], preferred_element_type=F32).astype(BF16)
    k_pe = _rope_slab(k_rope, cos, sin_neg, sin_pos).astype(BF16)
    for hd in range(MLA_HEADS):
        c0 = hd * MLA_QK_PAD
        s0 = hd * LANES
        q_ref[:, c0:c0 + LANES] = q_nope[:, s0:s0 + LANES].astype(BF16)
        q_ref[:, c0 + LANES:c0 + 2 * LANES] = _rope_slab(
            q_pe[:, s0:s0 + LANES], cos, sin_neg, sin_pos).astype(BF16)
        k_ref[:, c0:c0 + LANES] = k_nope[:, s0:s0 + LANES].astype(BF16)
        k_ref[:, c0 + LANES:c0 + 2 * LANES] = k_pe


def _mla_prep(h_b, pos_f, invf, w_lat, qg, kg, w_qn, w_qp, w_kn, w_v, tm=512):
    t, d = h_b.shape
    full = lambda a: pl.BlockSpec(a.shape, lambda i: (0,) * a.ndim)
    wq = MLA_HEADS * MLA_QK_PAD
    return pl.pallas_call(
        _mla_prep_kernel,
        grid=(t // tm,),
        in_specs=[pl.BlockSpec((tm, d), lambda i: (i, 0)),
                  pl.BlockSpec((tm, 1), lambda i: (i, 0)),
                  full(invf), full(w_lat), full(qg), full(kg), full(w_qn), full(w_qp),
                  full(w_kn), full(w_v)],
        out_specs=[pl.BlockSpec((tm, wq), lambda i: (i, 0)),
                   pl.BlockSpec((tm, wq), lambda i: (i, 0)),
                   pl.BlockSpec((tm, MLA_WIDTH), lambda i: (i, 0))],
        out_shape=[jax.ShapeDtypeStruct((t, wq), BF16), jax.ShapeDtypeStruct((t, wq), BF16),
                   jax.ShapeDtypeStruct((t, MLA_WIDTH), BF16)],
        compiler_params=_cparams(1),
        name="mla_prep",
    )(h_b, pos_f, invf, w_lat, qg, kg, w_qn, w_qp, w_kn, w_v)


def _sb_attn_kernel(q_ref, k_ref, v_ref, o_ref, *, tq, tk, scale):
    i = pl.program_id(2)
    q = q_ref[...]
    t_idx = i * tq + lax.broadcasted_iota(jnp.int32, (tq, 1), 0)
    r_i = lax.broadcasted_iota(jnp.int32, (tk, tk), 0)
    c_i = lax.broadcasted_iota(jnp.int32, (tk, tk), 1)
    suffix = (r_i >= c_i).astype(BF16)
    nkb = (i + 1) * (tq // tk)

    def body(jj, carry):
        c, acc = carry
        j = nkb - 1 - jj
        k0 = pl.multiple_of(j * tk, tk)
        ks = k_ref[pl.ds(k0, tk), :]
        vs = v_ref[pl.ds(k0, tk), :]
        z = lax.dot_general(q, ks, (((1,), (1,)), ((), ())), preferred_element_type=F32) * scale
        s_idx = k0 + lax.broadcasted_iota(jnp.int32, (1, tk), 1)
        past = s_idx < t_idx
        sp = jnp.maximum(z, 0.0) + jnp.log1p(jnp.exp(-jnp.abs(z)))
        spm = jnp.where(past, sp, 0.0)
        hi = spm.astype(BF16)
        lo = (spm - hi.astype(F32)).astype(BF16)
        incl = (jnp.dot(hi, suffix, preferred_element_type=F32)
                + jnp.dot(lo, suffix, preferred_element_type=F32))
        w = jnp.where(past, jnp.exp(z - incl - c), 0.0)
        acc = acc + jnp.dot(w.astype(BF16), vs, preferred_element_type=F32)
        return c + incl[:, 0:1], acc

    init = (jnp.zeros((tq, 1), F32), jnp.zeros((tq, HEAD_DIM), F32))
    _, acc = lax.fori_loop(0, nkb, body, init)
    o_ref[...] = acc.astype(o_ref.dtype)


def _sb_attention(u_sb3, tq=256, tk=128):
    b, s, _ = u_sb3.shape
    kern = functools.partial(_sb_attn_kernel, tq=tq, tk=tk, scale=HEAD_DIM ** -0.5)
    return pl.pallas_call(
        kern,
        grid=(b, SB_HEADS, s // tq),
        in_specs=[pl.BlockSpec((None, tq, HEAD_DIM), lambda bb, h, i: (bb, i, h)),
                  pl.BlockSpec((None, s, HEAD_DIM), lambda bb, h, i: (bb, 0, SB_HEADS + h)),
                  pl.BlockSpec((None, s, HEAD_DIM), lambda bb, h, i: (bb, 0, 2 * SB_HEADS + h))],
        out_specs=pl.BlockSpec((None, tq, HEAD_DIM), lambda bb, h, i: (bb, i, h)),
        out_shape=jax.ShapeDtypeStruct((b, s, SB_WIDTH), BF16),
        compiler_params=_cparams(3),
        name="sb_attention",
    )(u_sb3, u_sb3, u_sb3)


def _mla_attn_kernel(q_ref, k_ref, v_ref, o_ref, *, tq, scale):
    i = pl.program_id(2)
    q = q_ref[...]
    t_chunk = lax.shift_right_logical(
        i * tq + lax.broadcasted_iota(jnp.int32, (tq, 1), 0), CHUNK.bit_length() - 1)

    def body(j, carry):
        m, l, acc = carry
        k0 = pl.multiple_of(j * tq, tq)
        ks = k_ref[pl.ds(k0, tq), :]
        vs = v_ref[pl.ds(k0, tq), :]
        s = lax.dot_general(q, ks, (((1,), (1,)), ((), ())), preferred_element_type=F32) * scale
        s_chunk = lax.shift_right_logical(
            k0 + lax.broadcasted_iota(jnp.int32, (1, tq), 1), CHUNK.bit_length() - 1)
        s = jnp.where(s_chunk <= t_chunk, s, -1e30)
        m_new = jnp.maximum(m, jnp.max(s, -1, keepdims=True))
        alpha = jnp.exp(m - m_new)
        p = jnp.exp(s - m_new)
        l = alpha * l + jnp.sum(p, -1, keepdims=True)
        acc = alpha * acc + jnp.dot(p.astype(BF16), vs, preferred_element_type=F32)
        return m_new, l, acc

    init = (jnp.full((tq, 1), -1e30, F32), jnp.zeros((tq, 1), F32), jnp.zeros((tq, MLA_V), F32))
    _, l, acc = lax.fori_loop(0, i + 1, body, init)
    o_ref[...] = (acc / l).astype(o_ref.dtype)


def _mla_attention(q3, k3, v3, tq=256):
    b, s, _ = q3.shape
    kern = functools.partial(_mla_attn_kernel, tq=tq, scale=(MLA_NOPE + MLA_ROPE) ** -0.5)
    return pl.pallas_call(
        kern,
        grid=(b, MLA_HEADS, s // tq),
        in_specs=[pl.BlockSpec((None, tq, MLA_QK_PAD), lambda bb, h, i: (bb, i, h)),
                  pl.BlockSpec((None, s, MLA_QK_PAD), lambda bb, h, i: (bb, 0, h)),
                  pl.BlockSpec((None, s, MLA_V), lambda bb, h, i: (bb, 0, h))],
        out_specs=pl.BlockSpec((None, tq, MLA_V), lambda bb, h, i: (bb, i, h)),
        out_shape=jax.ShapeDtypeStruct((b, s, MLA_WIDTH), BF16),
        compiler_params=_cparams(3),
        name="mla_attention",
    )(q3, k3, v3)


def _mix_router_kernel(sb_ref, mla_ref, x_ref, lig_ref, lib_ref, sg_ref, mg_ref, wo_ref,
                       lmg_ref, lmb_ref, wr_ref, br_ref,
                       h2_ref, h2p_ref, sel_ref, gate_ref, cnt_ref, run_ref, *, tm):
    step = pl.program_id(0)

    @pl.when(step == 0)
    def _():
        run_ref[...] = jnp.zeros_like(run_ref)

    a = _rms_norm(sb_ref[...].astype(F32), sg_ref[...]).astype(BF16)
    bm = _rms_norm(mla_ref[...].astype(F32), mg_ref[...]).astype(BF16)
    mix = (jnp.dot(a, wo_ref[:SB_WIDTH, :], preferred_element_type=F32)
           + jnp.dot(bm, wo_ref[SB_WIDTH:, :], preferred_element_type=F32))
    h = _layer_norm(x_ref[...], lig_ref[...], lib_ref[...])
    h2 = _layer_norm(DEEPNORM_ALPHA * h + mix, lmg_ref[...], lmb_ref[...])
    h2_ref[...] = h2
    h2p_ref[...] = _pack_halves(h2[:, :HALF], h2[:, HALF:])

    logits = jnp.dot(h2, wr_ref[...], preferred_element_type=F32,
                     precision=lax.Precision.HIGHEST) + br_ref[...]
    lane = lax.broadcasted_iota(jnp.int32, (tm, LANES), 1)
    lane_f = lane.astype(F32)
    logits = jnp.where(lane < N_EXPERTS, logits, -jnp.inf)
    vals, hots = [], []
    sel = jnp.zeros((tm, LANES), F32)
    for k in range(TOP_K):
        m = jnp.max(logits, -1, keepdims=True)
        idx = jnp.min(jnp.where(logits == m, lane_f, float(LANES)), -1, keepdims=True)
        hot = lane_f == idx
        logits = jnp.where(hot, -jnp.inf, logits)
        vals.append(m)
        hots.append(hot)
        sel = jnp.where(lane == k, idx, sel)
    exps = [jnp.exp(v - vals[0]) for v in vals]
    denom = exps[0] + exps[1] + exps[2] + exps[3]
    gates = jnp.zeros((tm, LANES), F32)
    for k in range(TOP_K):
        gates = jnp.where(lane == k, exps[k] / denom, gates)
    gate_ref[...] = gates

    onehot = (hots[0] | hots[1] | hots[2] | hots[3]).astype(F32)
    r_i = lax.broadcasted_iota(jnp.int32, (tm, tm), 0)
    c_i = lax.broadcasted_iota(jnp.int32, (tm, tm), 1)
    before = (c_i < r_i).astype(BF16)
    rank = jnp.dot(before, onehot.astype(BF16), preferred_element_type=F32) + run_ref[...]
    for k in range(TOP_K):
        pos = jnp.sum(jnp.where(hots[k], rank, 0.0), -1, keepdims=True)
        sel = jnp.where(lane == TOP_K + k, pos, sel)
    sel_ref[...] = sel
    run_ref[...] = run_ref[...] + jnp.sum(onehot, 0, keepdims=True)
    cnt_ref[...] = run_ref[...]


def _mix_router(sb_o, mla_o, x2, lig, lib, sg, mg, w_o, lmg, lmb, w_r, b_r, tm=256):
    t, d = x2.shape
    full = lambda a: pl.BlockSpec(a.shape, lambda i: (0,) * a.ndim)
    row = lambda w: pl.BlockSpec((tm, w), lambda i: (i, 0))
    return pl.pallas_call(
        functools.partial(_mix_router_kernel, tm=tm),
        grid=(t // tm,),
        in_specs=[row(SB_WIDTH), row(MLA_WIDTH), row(d), full(lig), full(lib), full(sg), full(mg),
                  full(w_o), full(lmg), full(lmb), full(w_r), full(b_r)],
        out_specs=[row(d), row(HALF), row(LANES), row(LANES),
                   pl.BlockSpec((1, LANES), lambda i: (0, 0))],
        out_shape=[jax.ShapeDtypeStruct((t, d), F32), jax.ShapeDtypeStruct((t, HALF), U32),
                   jax.ShapeDtypeStruct((t, LANES), F32), jax.ShapeDtypeStruct((t, LANES), F32),
                   jax.ShapeDtypeStruct((1, LANES), F32)],
        scratch_shapes=[pltpu.VMEM((1, LANES), F32)],
        compiler_params=_cparams(1),
        name="mix_router",
    )(sb_o, mla_o, x2, lig, lib, sg, mg, w_o, lmg, lmb, w_r, b_r)


def _gather_kernel(idx_ref, src_hbm, out_ref, sem, *, rows):
    def issue(r, _):
        src = idx_ref[0, 0, r]
        pltpu.make_async_copy(src_hbm.at[pl.ds(src, 1)], out_ref.at[pl.ds(r, 1)], sem).start()
        return 0

    lax.fori_loop(0, rows, issue, 0, unroll=8)
    pltpu.make_async_copy(src_hbm.at[pl.ds(0, rows)], out_ref, sem).wait()


def _gather_rows(src, idx, rows=512):
    n = idx.shape[0]
    w = src.shape[1]
    idx3 = idx.reshape(n // rows, 1, rows)
    return pl.pallas_call(
        functools.partial(_gather_kernel, rows=rows),
        grid=(n // rows,),
        in_specs=[pl.BlockSpec((1, 1, rows), lambda i: (i, 0, 0), memory_space=pltpu.SMEM),
                  pl.BlockSpec(memory_space=pl.ANY)],
        out_specs=pl.BlockSpec((rows, w), lambda i: (i, 0)),
        out_shape=jax.ShapeDtypeStruct((n, w), src.dtype),
        scratch_shapes=[pltpu.SemaphoreType.DMA(())],
        compiler_params=_cparams(1),
        name="gather_rows",
    )(idx3, src)


MOE_ROWS = 2048
MOE_SUB = 256
MOE_TF = 256
MOE_NF = D_FF // MOE_TF
MOE_ITEMS = N_EXPERTS + (N_EXPERTS * (MOE_BLOCK - 1) + 8192 * TOP_K) // MOE_ROWS + 1


def _moe_kernel(ie_ref, is_ref, nb_ref, tail_ref, xs_hbm, wg_ref, wu_ref, bg_ref, bu_ref, wd_ref,
                bd_ref, ys_hbm, pk_ref, x_ref, acc_ref, wgb_ref, wub_ref, wdb_ref, sem):
    i = pl.program_id(0)
    f = pl.program_id(1)
    nb = nb_ref[i]
    start = is_ref[i]
    n_out_blocks = ys_hbm.shape[0] // MOE_BLOCK

    def block_copy(b, to_vmem):
        hbm = (xs_hbm if to_vmem else ys_hbm).at[pl.ds(pl.multiple_of(start + b * MOE_BLOCK, MOE_BLOCK),
                                                        MOE_BLOCK)]
        vm = pk_ref.at[pl.ds(pl.multiple_of(b * MOE_BLOCK, MOE_BLOCK), MOE_BLOCK)]
        return pltpu.make_async_copy(hbm, vm, sem) if to_vmem else pltpu.make_async_copy(vm, hbm, sem)

    @pl.when((i == 0) & (f == 0))
    def _():
        x_ref[...] = jnp.zeros_like(x_ref)
        acc_ref[...] = jnp.zeros_like(acc_ref)

    @pl.when((f == 0) & (nb > 0))
    def _():
        def start_in(b, _):
            block_copy(b, True).start()
            return 0

        def wait_in(b, _):
            block_copy(b, True).wait()
            return 0

        def unpack(b, _):
            r0 = pl.multiple_of(b * MOE_BLOCK, MOE_BLOCK)
            lo, hi = _unpack_halves(pk_ref[pl.ds(r0, MOE_BLOCK), :])
            x_ref[pl.ds(r0, MOE_BLOCK), :HALF] = lo.astype(BF16)
            x_ref[pl.ds(r0, MOE_BLOCK), HALF:] = hi.astype(BF16)
            return 0

        lax.fori_loop(0, nb, start_in, 0)
        lax.fori_loop(0, nb, wait_in, 0)
        lax.fori_loop(0, nb, unpack, 0)

    @pl.when(nb > 0)
    def _():
        wgb_ref[...] = wg_ref[...].astype(BF16)
        wub_ref[...] = wu_ref[...].astype(BF16)
        wdb_ref[...] = wd_ref[...].astype(BF16)

        def sub(s, _):
            r0 = pl.multiple_of(s * MOE_SUB, MOE_SUB)
            x = x_ref[pl.ds(r0, MOE_SUB), :]
            g = jnp.dot(x, wgb_ref[...], preferred_element_type=F32) + bg_ref[...]
            u = jnp.dot(x, wub_ref[...], preferred_element_type=F32) + bu_ref[...]
            g = jnp.minimum(g, SWIGLU_LIMIT)
            u = jnp.clip(u, -SWIGLU_LIMIT, SWIGLU_LIMIT)
            act = (u + 1.0) * (g * jax.nn.sigmoid(SWIGLU_ALPHA * g))
            y = jnp.dot(act.astype(BF16), wdb_ref[...], preferred_element_type=F32)
            prev = jnp.where(f == 0, bd_ref[...], acc_ref[pl.ds(r0, MOE_SUB), :])
            acc_ref[pl.ds(r0, MOE_SUB), :] = prev + y
            return 0

        lax.fori_loop(0, (nb * MOE_BLOCK + MOE_SUB - 1) // MOE_SUB, sub, 0)

    @pl.when((f == MOE_NF - 1) & (nb > 0))
    def _():
        def pack(b, _):
            r0 = pl.multiple_of(b * MOE_BLOCK, MOE_BLOCK)
            y = acc_ref[pl.ds(r0, MOE_BLOCK), :]
            pk_ref[pl.ds(r0, MOE_BLOCK), :] = _pack_halves(y[:, :HALF], y[:, HALF:])
            return 0

        def start_out(b, _):
            block_copy(b, False).start()
            return 0

        def wait_out(b, _):
            block_copy(b, False).wait()
            return 0

        lax.fori_loop(0, nb, pack, 0)
        lax.fori_loop(0, nb, start_out, 0)
        lax.fori_loop(0, nb, wait_out, 0)

    @pl.when((i == MOE_ITEMS - 1) & (f == MOE_NF - 1))
    def _():
        pk_ref[:MOE_BLOCK, :] = jnp.zeros((MOE_BLOCK, HALF), U32)

        def tail_copy(b):
            return pltpu.make_async_copy(
                pk_ref.at[pl.ds(0, MOE_BLOCK)],
                ys_hbm.at[pl.ds(pl.multiple_of(b * MOE_BLOCK, MOE_BLOCK), MOE_BLOCK)], sem)

        def start_tail(b, _):
            tail_copy(b).start()
            return 0

        def wait_tail(b, _):
            tail_copy(b).wait()
            return 0

        lax.fori_loop(tail_ref[0], n_out_blocks, start_tail, 0)
        lax.fori_loop(tail_ref[0], n_out_blocks, wait_tail, 0)


def _moe_ffn(item_e, item_start, item_nb, tail_blk, xs, w_gu, b_gu, w_d, b_d, n_rows):
    def wmap(off):
        def im(i, f, ie, is_, nb, tail):
            return ie[i], 0, off + jnp.where(nb[i] > 0, f, MOE_NF - 1)
        return im

    def dmap(i, f, ie, is_, nb, tail):
        return ie[i], jnp.where(nb[i] > 0, f, MOE_NF - 1), 0

    grid_spec = pltpu.PrefetchScalarGridSpec(
        num_scalar_prefetch=4,
        grid=(MOE_ITEMS, MOE_NF),
        in_specs=[pl.BlockSpec(memory_space=pl.ANY),
                  pl.BlockSpec((None, D_MODEL, MOE_TF), wmap(0)),
                  pl.BlockSpec((None, D_MODEL, MOE_TF), wmap(MOE_NF)),
                  pl.BlockSpec((None, 1, MOE_TF), wmap(0)),
                  pl.BlockSpec((None, 1, MOE_TF), wmap(MOE_NF)),
                  pl.BlockSpec((None, MOE_TF, D_MODEL), dmap),
                  pl.BlockSpec((None, 1, D_MODEL), lambda i, f, ie, is_, nb, tail: (ie[i], 0, 0))],
        out_specs=pl.BlockSpec(memory_space=pl.ANY),
        scratch_shapes=[pltpu.VMEM((MOE_ROWS, HALF), U32),
                        pltpu.VMEM((MOE_ROWS, D_MODEL), BF16),
                        pltpu.VMEM((MOE_ROWS, D_MODEL), F32),
                        pltpu.VMEM((D_MODEL, MOE_TF), BF16),
                        pltpu.VMEM((D_MODEL, MOE_TF), BF16),
                        pltpu.VMEM((MOE_TF, D_MODEL), BF16),
                        pltpu.SemaphoreType.DMA(())])
    return pl.pallas_call(
        _moe_kernel,
        grid_spec=grid_spec,
        out_shape=jax.ShapeDtypeStruct((n_rows, HALF), U32),
        compiler_params=_cparams(2),
        name="moe_ffn",
    )(item_e, item_start, item_nb, tail_blk, xs, w_gu, w_gu, b_gu, b_gu, w_d, b_d)


def _combine_kernel(yg_ref, gate_ref, h2_ref, g_ref, b_ref, o_ref):
    h2 = h2_ref[...]
    lo = DEEPNORM_ALPHA * h2[:, :HALF]
    hi = DEEPNORM_ALPHA * h2[:, HALF:]
    for k in range(TOP_K):
        ylo, yhi = _unpack_halves(yg_ref[:, k * HALF:(k + 1) * HALF])
        gk = gate_ref[:, k:k + 1]
        lo = lo + gk * ylo
        hi = hi + gk * yhi
    mu = (jnp.sum(lo, -1, keepdims=True) + jnp.sum(hi, -1, keepdims=True)) * (1.0 / D_MODEL)
    lo = lo - mu
    hi = hi - mu
    var = (jnp.sum(lo * lo, -1, keepdims=True) + jnp.sum(hi * hi, -1, keepdims=True)) * (1.0 / D_MODEL)
    inv = lax.rsqrt(var + LN_EPS)
    o_ref[:, :HALF] = lo * inv * g_ref[:, :HALF] + b_ref[:, :HALF]
    o_ref[:, HALF:] = hi * inv * g_ref[:, HALF:] + b_ref[:, HALF:]


def _combine(yg, gates, h2, g, b, tm=256):
    t, d = h2.shape
    full = lambda a: pl.BlockSpec(a.shape, lambda i: (0,) * a.ndim)
    return pl.pallas_call(
        _combine_kernel,
        grid=(t // tm,),
        in_specs=[pl.BlockSpec((tm, TOP_K * HALF), lambda i: (i, 0)),
                  pl.BlockSpec((tm, LANES), lambda i: (i, 0)),
                  pl.BlockSpec((tm, d), lambda i: (i, 0)), full(g), full(b)],
        out_specs=pl.BlockSpec((tm, d), lambda i: (i, 0)),
        out_shape=jax.ShapeDtypeStruct((t, d), F32),
        compiler_params=_cparams(1),
        name="combine_ln",
    )(yg, gates, h2, g, b)


def _routing_tables(sel, counts_f):
    t = sel.shape[0]
    idx = sel[:, :TOP_K].astype(jnp.int32)
    pos = sel[:, TOP_K:2 * TOP_K].astype(jnp.int32)
    counts = counts_f[0, :N_EXPERTS].astype(jnp.int32)
    padded = ((counts + MOE_BLOCK - 1) // MOE_BLOCK) * MOE_BLOCK
    pends = jnp.cumsum(padded)
    pstarts = pends - padded
    dest = (pstarts[idx] + pos).reshape(-1)
    n_rows = t * TOP_K + N_EXPERTS * MOE_BLOCK
    tok = jnp.arange(t * TOP_K, dtype=jnp.int32) // TOP_K
    row_tok = jnp.zeros((n_rows,), jnp.int32).at[dest].set(tok)

    nchunk = (padded + MOE_ROWS - 1) // MOE_ROWS
    cend = jnp.cumsum(nchunk)
    cstart = cend - nchunk
    total = cend[-1]
    it = jnp.arange(MOE_ITEMS, dtype=jnp.int32)
    valid = it < total
    it_c = jnp.minimum(it, total - 1)
    ie = jnp.searchsorted(cend, it_c, side="right").astype(jnp.int32)
    c = it_c - cstart[ie]
    istart = (pstarts[ie] + c * MOE_ROWS).astype(jnp.int32)
    inb = jnp.where(valid, jnp.minimum(MOE_ROWS, padded[ie] - c * MOE_ROWS) // MOE_BLOCK, 0)
    tail_blk = (pends[-1] // MOE_BLOCK).astype(jnp.int32).reshape(1)
    return dest, row_tok, ie, istart, inb.astype(jnp.int32), tail_blk, n_rows


def kernel(x, positions, ln_in_g, ln_in_b, w_in, q_a_norm, w_q_b, kv_a_norm, w_kv_b, sb_out_norm,
           mla_out_norm, w_o, ln_mix_g, ln_mix_b, w_router, b_router, w_gate_up, b_gate_up, w_down,
           b_down, ln_ffn_g, ln_ffn_b):
    bsz, seq, d = x.shape
    t = bsz * seq
    x2 = x.reshape(t, d)
    row = lambda v: v.reshape(1, -1)

    w_in0 = w_in[0]
    w_sb = w_in0[:, :3 * SB_WIDTH].astype(BF16)
    w_lat = jnp.pad(w_in0[:, 3 * SB_WIDTH:], ((0, 0), (0, LANES - MLA_ROPE))).astype(BF16)
    wq = w_q_b[0].reshape(MLA_Q_RANK, MLA_HEADS, MLA_NOPE + MLA_ROPE)
    w_qn = wq[:, :, :MLA_NOPE].reshape(MLA_Q_RANK, -1).astype(BF16)
    w_qp = jnp.pad(wq[:, :, MLA_NOPE:], ((0, 0), (0, 0), (0, LANES - MLA_ROPE))
                   ).reshape(MLA_Q_RANK, -1).astype(BF16)
    wkv = w_kv_b[0].reshape(MLA_KV_RANK, MLA_HEADS, MLA_NOPE + MLA_V)
    w_kn = wkv[:, :, :MLA_NOPE].reshape(MLA_KV_RANK, -1).astype(BF16)
    w_v = wkv[:, :, MLA_NOPE:].reshape(MLA_KV_RANK, -1).astype(BF16)
    w_r = jnp.pad(w_router[0], ((0, 0), (0, LANES - N_EXPERTS)))
    b_r = jnp.pad(b_router[0], (0, LANES - N_EXPERTS)).reshape(1, LANES)
    half = MLA_ROPE // 2
    inv_freq = ROPE_THETA ** (-(jnp.arange(half, dtype=F32) * 2.0 / MLA_ROPE))
    invf = jnp.concatenate([inv_freq, inv_freq, jnp.zeros((LANES - MLA_ROPE,), F32)]).reshape(1, LANES)
    pos_f = positions.reshape(t, 1).astype(F32)

    for l in range(DEPTH):
        u_sb, h_b = _ln_inproj(x2, row(ln_in_g), row(ln_in_b), w_sb)
        q_m, k_m, v_m = _mla_prep(h_b, pos_f, invf, w_lat, row(q_a_norm[l]), row(kv_a_norm[l]),
                                  w_qn, w_qp, w_kn, w_v)
        sb_o = _sb_attention(u_sb.reshape(bsz, seq, -1)).reshape(t, -1)
        mla_o = _mla_attention(q_m.reshape(bsz, seq, -1), k_m.reshape(bsz, seq, -1),
                               v_m.reshape(bsz, seq, -1)).reshape(t, -1)
        h2, h2p, sel, gates, counts_f = _mix_router(
            sb_o, mla_o, x2, row(ln_in_g), row(ln_in_b), row(sb_out_norm[l]), row(mla_out_norm[l]),
            w_o[l].astype(BF16), row(ln_mix_g[l]), row(ln_mix_b[l]), w_r, b_r)
        dest, row_tok, ie, istart, inb, tail_blk, n_rows = _routing_tables(sel, counts_f)
        xs = _gather_rows(h2p, row_tok)
        ys = _moe_ffn(ie, istart, inb, tail_blk, xs, w_gate_up[l], b_gate_up[l].reshape(N_EXPERTS, 1, -1),
                      w_down[l], b_down[l].reshape(N_EXPERTS, 1, -1), n_rows)
        yg = _gather_rows(ys, dest).reshape(t, TOP_K * HALF)
        out = _combine(yg, gates, h2, row(ln_ffn_g[l]), row(ln_ffn_b[l]))
    return out.reshape(bsz, seq, d)
```

```python
import functools

import jax
import jax.numpy as jnp
from jax import lax
from jax.experimental import pallas as pl
from jax.experimental.pallas import tpu as pltpu

D_MODEL = 2048
DEPTH = 1
CHUNK = 64
SB_HEADS = 8
HEAD_DIM = 128
SB_WIDTH = SB_HEADS * HEAD_DIM
MLA_HEADS = 8
MLA_Q_RANK = 512
MLA_KV_RANK = 256
MLA_NOPE = 128
MLA_ROPE = 64
MLA_V = 128
MLA_WIDTH = MLA_HEADS * MLA_V
MLA_QK_PAD = 256
ROPE_THETA = 10000.0
N_EXPERTS = 32
TOP_K = 4
D_FF = 2048
SWIGLU_LIMIT = 7.0
SWIGLU_ALPHA = 1.702
MOE_BLOCK = 128
LN_EPS = 1e-5
RMS_EPS = 1e-6
DEEPNORM_ALPHA = (2 * DEPTH) ** 0.25

LANES = 128
HALF = D_MODEL // 2
VMEM_LIMIT_BYTES = 56 * 1024 * 1024

BF16 = jnp.bfloat16
F32 = jnp.float32
U32 = jnp.uint32


def _cparams(n_axes, vmem=None):
    return pltpu.CompilerParams(dimension_semantics=("arbitrary",) * n_axes,
                                vmem_limit_bytes=vmem or VMEM_LIMIT_BYTES)


def _layer_norm(x, g, b):
    mu = jnp.mean(x, -1, keepdims=True)
    xc = x - mu
    var = jnp.mean(xc * xc, -1, keepdims=True)
    return xc * lax.rsqrt(var + LN_EPS) * g + b


def _rms_norm(x, g):
    ms = jnp.mean(x * x, -1, keepdims=True)
    return x * lax.rsqrt(ms + RMS_EPS) * g


def _pack_halves(lo_f32, hi_f32):
    lo = lax.bitcast_convert_type(lo_f32.astype(BF16).astype(F32), U32)
    hi = lax.bitcast_convert_type(hi_f32.astype(BF16).astype(F32), U32)
    return lax.shift_right_logical(lo, U32(16)) | (hi & U32(0xFFFF0000))


def _unpack_halves(w):
    lo = lax.bitcast_convert_type(lax.shift_left(w, U32(16)), F32)
    hi = lax.bitcast_convert_type(w & U32(0xFFFF0000), F32)
    return lo, hi


def _ln_inproj_kernel(x_ref, g_ref, b_ref, w_ref, u_ref, h_ref):
    @pl.when(pl.program_id(1) == 0)
    def _():
        h_ref[...] = _layer_norm(x_ref[...], g_ref[...], b_ref[...]).astype(BF16)

    u_ref[...] = jnp.dot(h_ref[...], w_ref[...], preferred_element_type=F32).astype(BF16)


def _ln_inproj(x2, g, b, w_sb, tm=512, tn=1024):
    t, d = x2.shape
    n = w_sb.shape[1]
    return pl.pallas_call(
        _ln_inproj_kernel,
        grid=(t // tm, n // tn),
        in_specs=[pl.BlockSpec((tm, d), lambda i, j: (i, 0)),
                  pl.BlockSpec((1, d), lambda i, j: (0, 0)),
                  pl.BlockSpec((1, d), lambda i, j: (0, 0)),
                  pl.BlockSpec((d, tn), lambda i, j: (0, j))],
        out_specs=[pl.BlockSpec((tm, tn), lambda i, j: (i, j)),
                   pl.BlockSpec((tm, d), lambda i, j: (i, 0))],
        out_shape=[jax.ShapeDtypeStruct((t, n), BF16), jax.ShapeDtypeStruct((t, d), BF16)],
        compiler_params=_cparams(2),
        name="ln_inproj",
    )(x2, g, b, w_sb)


def _rope_slab(x, cos, sin_neg, sin_pos):
    return (x * cos + pltpu.roll(x, LANES - MLA_ROPE // 2, 1) * sin_neg
            + pltpu.roll(x, MLA_ROPE // 2, 1) * sin_pos)


def _mla_prep_kernel(h_ref, pos_ref, invf_ref, wl_ref, qg_ref, kg_ref, wqn_ref, wqp_ref,
                     wkn_ref, wv_ref, q_ref, k_ref, v_ref):
    lat = jnp.dot(h_ref[...], wl_ref[...], preferred_element_type=F32)
    qn = _rms_norm(lat[:, :MLA_Q_RANK], qg_ref[...]).astype(BF16)
    kvn = _rms_norm(lat[:, MLA_Q_RANK:MLA_Q_RANK + MLA_KV_RANK], kg_ref[...]).astype(BF16)
    k_rope = lat[:, MLA_Q_RANK + MLA_KV_RANK:]

    ang = pos_ref[...] * invf_ref[...]
    cos = jnp.cos(ang)
    sin = jnp.sin(ang)
    lane = lax.broadcasted_iota(jnp.int32, ang.shape, 1)
    sin_neg = jnp.where(lane < MLA_ROPE // 2, -sin, 0.0)
    sin_pos = jnp.where((lane >= MLA_ROPE // 2) & (lane < MLA_ROPE), sin, 0.0)

    q_nope = jnp.dot(qn, wqn_ref[...], preferred_element_type=F32)
    q_pe = jnp.dot(qn, wqp_ref[...], preferred_element_type=F32)
    k_nope = jnp.dot(kvn, wkn_ref[...], preferred_element_type=F32)
    v_ref[...] = jnp.dot(kvn, wv_ref[...], preferred_element_type=F32).astype(BF16)
    k_pe = _rope_slab(k_rope, cos, sin_neg, sin_pos).astype(BF16)
    for hd in range(MLA_HEADS):
        c0 = hd * MLA_QK_PAD
        s0 = hd * LANES
        q_ref[:, c0:c0 + LANES] = q_nope[:, s0:s0 + LANES].astype(BF16)
        q_ref[:, c0 + LANES:c0 + 2 * LANES] = _rope_slab(
            q_pe[:, s0:s0 + LANES], cos, sin_neg, sin_pos).astype(BF16)
        k_ref[:, c0:c0 + LANES] = k_nope[:, s0:s0 + LANES].astype(BF16)
        k_ref[:, c0 + LANES:c0 + 2 * LANES] = k_pe


def _mla_prep(h_b, pos_f, invf, w_lat, qg, kg, w_qn, w_qp, w_kn, w_v, tm=512):
    t, d = h_b.shape
    full = lambda a: pl.BlockSpec(a.shape, lambda i: (0,) * a.ndim)
    wq = MLA_HEADS * MLA_QK_PAD
    return pl.pallas_call(
        _mla_prep_kernel,
        grid=(t // tm,),
        in_specs=[pl.BlockSpec((tm, d), lambda i: (i, 0)),
                  pl.BlockSpec((tm, 1), lambda i: (i, 0)),
                  full(invf), full(w_lat), full(qg), full(kg), full(w_qn), full(w_qp),
                  full(w_kn), full(w_v)],
        out_specs=[pl.BlockSpec((tm, wq), lambda i: (i, 0)),
                   pl.BlockSpec((tm, wq), lambda i: (i, 0)),
                   pl.BlockSpec((tm, MLA_WIDTH), lambda i: (i, 0))],
        out_shape=[jax.ShapeDtypeStruct((t, wq), BF16), jax.ShapeDtypeStruct((t, wq), BF16),
                   jax.ShapeDtypeStruct((t, MLA_WIDTH), BF16)],
        compiler_params=_cparams(1),
        name="mla_prep",
    )(h_b, pos_f, invf, w_lat, qg, kg, w_qn, w_qp, w_kn, w_v)


ATTN_HEADS_PER_STEP = 4


def _sb_attn_kernel(q_ref, k_ref, v_ref, o_ref, *, tq, tk, heads, scale):
    i = pl.program_id(2)
    t_idx = i * tq + lax.broadcasted_iota(jnp.int32, (tq, 1), 0)
    r_i = lax.broadcasted_iota(jnp.int32, (tk, tk), 0)
    c_i = lax.broadcasted_iota(jnp.int32, (tk, tk), 1)
    suffix = (r_i >= c_i).astype(BF16)
    nkb = (i + 1) * (tq // tk)

    def block(j, hd, c, acc):
        cols = slice(hd * HEAD_DIM, (hd + 1) * HEAD_DIM)
        k0 = pl.multiple_of(j * tk, tk)
        ks = k_ref[pl.ds(k0, tk), cols]
        vs = v_ref[pl.ds(k0, tk), cols]
        z = lax.dot_general(q_ref[:, cols], ks, (((1,), (1,)), ((), ())),
                            preferred_element_type=F32) * scale
        s_idx = k0 + lax.broadcasted_iota(jnp.int32, (1, tk), 1)
        past = s_idx < t_idx
        sp = jnp.maximum(z, 0.0) + jnp.log1p(jnp.exp(-jnp.abs(z)))
        spm = jnp.where(past, sp, 0.0)
        hi = spm.astype(BF16)
        lo = (spm - hi.astype(F32)).astype(BF16)
        incl = (jnp.dot(hi, suffix, preferred_element_type=F32)
                + jnp.dot(lo, suffix, preferred_element_type=F32))
        w = jnp.where(past, jnp.exp(z - incl - c), 0.0)
        acc = acc + jnp.dot(w.astype(BF16), vs, preferred_element_type=F32)
        return c + incl[:, 0:1], acc

    def body(p, carry):
        j = nkb - 1 - 2 * p
        out = []
        for hd in range(heads):
            c, acc = carry[hd]
            c, acc = block(j, hd, c, acc)
            c, acc = block(j - 1, hd, c, acc)
            out.append((c, acc))
        return tuple(out)

    init = tuple((jnp.zeros((tq, 1), F32), jnp.zeros((tq, HEAD_DIM), F32)) for _ in range(heads))
    res = lax.fori_loop(0, nkb // 2, body, init)
    for hd in range(heads):
        o_ref[:, hd * HEAD_DIM:(hd + 1) * HEAD_DIM] = res[hd][1].astype(o_ref.dtype)


def _sb_attention(u_sb3, tq=256, tk=128, heads=ATTN_HEADS_PER_STEP):
    b, s, _ = u_sb3.shape
    assert (tq // tk) % 2 == 0 and SB_HEADS % heads == 0
    groups = SB_HEADS // heads
    w = heads * HEAD_DIM
    kern = functools.partial(_sb_attn_kernel, tq=tq, tk=tk, heads=heads, scale=HEAD_DIM ** -0.5)
    return pl.pallas_call(
        kern,
        grid=(b, groups, s // tq),
        in_specs=[pl.BlockSpec((None, tq, w), lambda bb, g, i: (bb, i, g)),
                  pl.BlockSpec((None, s, w), lambda bb, g, i: (bb, 0, groups + g)),
                  pl.BlockSpec((None, s, w), lambda bb, g, i: (bb, 0, 2 * groups + g))],
        out_specs=pl.BlockSpec((None, tq, w), lambda bb, g, i: (bb, i, g)),
        out_shape=jax.ShapeDtypeStruct((b, s, SB_WIDTH), BF16),
        compiler_params=_cparams(3),
        name="sb_attention",
    )(u_sb3, u_sb3, u_sb3)


def _mla_attn_kernel(q_ref, k_ref, v_ref, o_ref, *, tq, heads, scale):
    i = pl.program_id(2)
    t_chunk = lax.shift_right_logical(
        i * tq + lax.broadcasted_iota(jnp.int32, (tq, 1), 0), CHUNK.bit_length() - 1)

    def body(j, carry):
        k0 = pl.multiple_of(j * tq, tq)
        s_chunk = lax.shift_right_logical(
            k0 + lax.broadcasted_iota(jnp.int32, (1, tq), 1), CHUNK.bit_length() - 1)
        visible = s_chunk <= t_chunk
        out = []
        for hd in range(heads):
            m, l, acc = carry[hd]
            qk = slice(hd * MLA_QK_PAD, (hd + 1) * MLA_QK_PAD)
            ks = k_ref[pl.ds(k0, tq), qk]
            vs = v_ref[pl.ds(k0, tq), hd * MLA_V:(hd + 1) * MLA_V]
            s = lax.dot_general(q_ref[:, qk], ks, (((1,), (1,)), ((), ())),
                                preferred_element_type=F32) * scale
            s = jnp.where(visible, s, -1e30)
            m_new = jnp.maximum(m, jnp.max(s, -1, keepdims=True))
            alpha = jnp.exp(m - m_new)
            p = jnp.exp(s - m_new)
            l = alpha * l + jnp.sum(p, -1, keepdims=True)
            acc = alpha * acc + jnp.dot(p.astype(BF16), vs, preferred_element_type=F32)
            out.append((m_new, l, acc))
        return tuple(out)

    init = tuple((jnp.full((tq, 1), -1e30, F32), jnp.zeros((tq, 1), F32),
                  jnp.zeros((tq, MLA_V), F32)) for _ in range(heads))
    res = lax.fori_loop(0, i + 1, body, init)
    for hd in range(heads):
        _, l, acc = res[hd]
        o_ref[:, hd * MLA_V:(hd + 1) * MLA_V] = (acc / l).astype(o_ref.dtype)


def _mla_attention(q3, k3, v3, tq=256, heads=ATTN_HEADS_PER_STEP):
    b, s, _ = q3.shape
    assert MLA_HEADS % heads == 0
    kern = functools.partial(_mla_attn_kernel, tq=tq, heads=heads,
                             scale=(MLA_NOPE + MLA_ROPE) ** -0.5)
    return pl.pallas_call(
        kern,
        grid=(b, MLA_HEADS // heads, s // tq),
        in_specs=[pl.BlockSpec((None, tq, heads * MLA_QK_PAD), lambda bb, g, i: (bb, i, g)),
                  pl.BlockSpec((None, s, heads * MLA_QK_PAD), lambda bb, g, i: (bb, 0, g)),
                  pl.BlockSpec((None, s, heads * MLA_V), lambda bb, g, i: (bb, 0, g))],
        out_specs=pl.BlockSpec((None, tq, heads * MLA_V), lambda bb, g, i: (bb, i, g)),
        out_shape=jax.ShapeDtypeStruct((b, s, MLA_WIDTH), BF16),
        compiler_params=_cparams(3),
        name="mla_attention",
    )(q3, k3, v3)


def _mix_router_kernel(sb_ref, mla_ref, x_ref, lig_ref, lib_ref, sg_ref, mg_ref, wo_ref,
                       lmg_ref, lmb_ref, wr_ref, br_ref,
                       h2_ref, h2p_ref, sel_ref, gate_ref, cnt_ref, run_ref, *, tm):
    step = pl.program_id(0)

    @pl.when(step == 0)
    def _():
        run_ref[...] = jnp.zeros_like(run_ref)

    a = _rms_norm(sb_ref[...].astype(F32), sg_ref[...]).astype(BF16)
    bm = _rms_norm(mla_ref[...].astype(F32), mg_ref[...]).astype(BF16)
    mix = (jnp.dot(a, wo_ref[:SB_WIDTH, :], preferred_element_type=F32)
           + jnp.dot(bm, wo_ref[SB_WIDTH:, :], preferred_element_type=F32))
    h = _layer_norm(x_ref[...], lig_ref[...], lib_ref[...])
    h2 = _layer_norm(DEEPNORM_ALPHA * h + mix, lmg_ref[...], lmb_ref[...])
    h2_ref[...] = h2
    h2p_ref[...] = _pack_halves(h2[:, :HALF], h2[:, HALF:])

    logits = jnp.dot(h2, wr_ref[...], preferred_element_type=F32,
                     precision=lax.Precision.HIGHEST) + br_ref[...]
    lane = lax.broadcasted_iota(jnp.int32, (tm, LANES), 1)
    lane_f = lane.astype(F32)
    logits = jnp.where(lane < N_EXPERTS, logits, -jnp.inf)
    vals, hots = [], []
    sel = jnp.zeros((tm, LANES), F32)
    for k in range(TOP_K):
        m = jnp.max(logits, -1, keepdims=True)
        idx = jnp.min(jnp.where(logits == m, lane_f, float(LANES)), -1, keepdims=True)
        hot = lane_f == idx
        logits = jnp.where(hot, -jnp.inf, logits)
        vals.append(m)
        hots.append(hot)
        sel = jnp.where(lane == k, idx, sel)
    exps = [jnp.exp(v - vals[0]) for v in vals]
    denom = exps[0] + exps[1] + exps[2] + exps[3]
    gates = jnp.zeros((tm, LANES), F32)
    for k in range(TOP_K):
        gates = jnp.where(lane == k, exps[k] / denom, gates)
    gate_ref[...] = gates

    onehot = (hots[0] | hots[1] | hots[2] | hots[3]).astype(F32)
    r_i = lax.broadcasted_iota(jnp.int32, (tm, tm), 0)
    c_i = lax.broadcasted_iota(jnp.int32, (tm, tm), 1)
    before = (c_i < r_i).astype(BF16)
    rank = jnp.dot(before, onehot.astype(BF16), preferred_element_type=F32) + run_ref[...]
    for k in range(TOP_K):
        pos = jnp.sum(jnp.where(hots[k], rank, 0.0), -1, keepdims=True)
        sel = jnp.where(lane == TOP_K + k, pos, sel)
    sel_ref[...] = sel
    run_ref[...] = run_ref[...] + jnp.sum(onehot, 0, keepdims=True)
    cnt_ref[...] = run_ref[...]


def _mix_router(sb_o, mla_o, x2, lig, lib, sg, mg, w_o, lmg, lmb, w_r, b_r, tm=256):
    t, d = x2.shape
    full = lambda a: pl.BlockSpec(a.shape, lambda i: (0,) * a.ndim)
    row = lambda w: pl.BlockSpec((tm, w), lambda i: (i, 0))
    return pl.pallas_call(
        functools.partial(_mix_router_kernel, tm=tm),
        grid=(t // tm,),
        in_specs=[row(SB_WIDTH), row(MLA_WIDTH), row(d), full(lig), full(lib), full(sg), full(mg),
                  full(w_o), full(lmg), full(lmb), full(w_r), full(b_r)],
        out_specs=[row(d), row(HALF), row(LANES), row(LANES),
                   pl.BlockSpec((1, LANES), lambda i: (0, 0))],
        out_shape=[jax.ShapeDtypeStruct((t, d), F32), jax.ShapeDtypeStruct((t, HALF), U32),
                   jax.ShapeDtypeStruct((t, LANES), F32), jax.ShapeDtypeStruct((t, LANES), F32),
                   jax.ShapeDtypeStruct((1, LANES), F32)],
        scratch_shapes=[pltpu.VMEM((1, LANES), F32)],
        compiler_params=_cparams(1),
        name="mix_router",
    )(sb_o, mla_o, x2, lig, lib, sg, mg, w_o, lmg, lmb, w_r, b_r)


def _gather_kernel(idx_ref, src_hbm, out_ref, sem, *, rows):
    def issue(r, _):
        src = idx_ref[0, 0, r]
        pltpu.make_async_copy(src_hbm.at[pl.ds(src, 1)], out_ref.at[pl.ds(r, 1)], sem).start()
        return 0

    lax.fori_loop(0, rows, issue, 0, unroll=8)
    pltpu.make_async_copy(src_hbm.at[pl.ds(0, rows)], out_ref, sem).wait()


def _gather_rows(src, idx, rows=512):
    n = idx.shape[0]
    w = src.shape[1]
    idx3 = idx.reshape(n // rows, 1, rows)
    return pl.pallas_call(
        functools.partial(_gather_kernel, rows=rows),
        grid=(n // rows,),
        in_specs=[pl.BlockSpec((1, 1, rows), lambda i: (i, 0, 0), memory_space=pltpu.SMEM),
                  pl.BlockSpec(memory_space=pl.ANY)],
        out_specs=pl.BlockSpec((rows, w), lambda i: (i, 0)),
        out_shape=jax.ShapeDtypeStruct((n, w), src.dtype),
        scratch_shapes=[pltpu.SemaphoreType.DMA(())],
        compiler_params=_cparams(1),
        name="gather_rows",
    )(idx3, src)


MOE_ROWS = 2048
MOE_SUB = 256
MOE_TF = 256
MOE_NF = D_FF // MOE_TF
MOE_ITEMS = N_EXPERTS + (N_EXPERTS * (MOE_BLOCK - 1) + 8192 * TOP_K) // MOE_ROWS + 1


def _moe_kernel(ie_ref, is_ref, nb_ref, tail_ref, xs_hbm, wg_ref, wu_ref, bg_ref, bu_ref, wd_ref,
                bd_ref, ys_hbm, pk_ref, x_ref, acc_ref, wgb_ref, wub_ref, wdb_ref, sem):
    i = pl.program_id(0)
    f = pl.program_id(1)
    nb = nb_ref[i]
    start = is_ref[i]
    n_out_blocks = ys_hbm.shape[0] // MOE_BLOCK

    def block_copy(b, to_vmem):
        hbm = (xs_hbm if to_vmem else ys_hbm).at[pl.ds(pl.multiple_of(start + b * MOE_BLOCK, MOE_BLOCK),
                                                        MOE_BLOCK)]
        vm = pk_ref.at[pl.ds(pl.multiple_of(b * MOE_BLOCK, MOE_BLOCK), MOE_BLOCK)]
        return pltpu.make_async_copy(hbm, vm, sem) if to_vmem else pltpu.make_async_copy(vm, hbm, sem)

    @pl.when((i == 0) & (f == 0))
    def _():
        x_ref[...] = jnp.zeros_like(x_ref)
        acc_ref[...] = jnp.zeros_like(acc_ref)

    @pl.when((f == 0) & (nb > 0))
    def _():
        def start_in(b, _):
            block_copy(b, True).start()
            return 0

        def wait_in(b, _):
            block_copy(b, True).wait()
            return 0

        def unpack(b, _):
            r0 = pl.multiple_of(b * MOE_BLOCK, MOE_BLOCK)
            lo, hi = _unpack_halves(pk_ref[pl.ds(r0, MOE_BLOCK), :])
            x_ref[pl.ds(r0, MOE_BLOCK), :HALF] = lo.astype(BF16)
            x_ref[pl.ds(r0, MOE_BLOCK), HALF:] = hi.astype(BF16)
            return 0

        lax.fori_loop(0, nb, start_in, 0)
        lax.fori_loop(0, nb, wait_in, 0)
        lax.fori_loop(0, nb, unpack, 0)

    @pl.when(nb > 0)
    def _():
        n_sub = (nb * MOE_BLOCK + MOE_SUB - 1) // MOE_SUB

        def gate_up(s, wg, wu):
            x = x_ref[pl.ds(pl.multiple_of(s * MOE_SUB, MOE_SUB), MOE_SUB), :]
            g = jnp.dot(x, wg, preferred_element_type=F32) + bg_ref[...]
            u = jnp.dot(x, wu, preferred_element_type=F32) + bu_ref[...]
            g = jnp.minimum(g, SWIGLU_LIMIT)
            u = jnp.clip(u, -SWIGLU_LIMIT, SWIGLU_LIMIT)
            return ((u + 1.0) * (g * jax.nn.sigmoid(SWIGLU_ALPHA * g))).astype(BF16)

        def down(s, act, wd):
            r0 = pl.multiple_of(s * MOE_SUB, MOE_SUB)
            y = jnp.dot(act, wd, preferred_element_type=F32)
            prev = jnp.where(f == 0, bd_ref[...], acc_ref[pl.ds(r0, MOE_SUB), :])
            acc_ref[pl.ds(r0, MOE_SUB), :] = prev + y

        wg = wg_ref[...].astype(BF16)
        wu = wu_ref[...].astype(BF16)
        wgb_ref[...] = wg
        wub_ref[...] = wu
        act0 = gate_up(0, wg, wu)
        wdb_ref[...] = wd_ref[...].astype(BF16)

        def sub(s, act):
            down(s - 1, act, wdb_ref[...])
            return gate_up(s, wgb_ref[...], wub_ref[...])

        act_last = lax.fori_loop(1, n_sub, sub, act0)
        down(n_sub - 1, act_last, wdb_ref[...])

    @pl.when((f == MOE_NF - 1) & (nb > 0))
    def _():
        def pack(b, _):
            r0 = pl.multiple_of(b * MOE_BLOCK, MOE_BLOCK)
            y = acc_ref[pl.ds(r0, MOE_BLOCK), :]
            pk_ref[pl.ds(r0, MOE_BLOCK), :] = _pack_halves(y[:, :HALF], y[:, HALF:])
            return 0

        def start_out(b, _):
            block_copy(b, False).start()
            return 0

        def wait_out(b, _):
            block_copy(b, False).wait()
            return 0

        lax.fori_loop(0, nb, pack, 0)
        lax.fori_loop(0, nb, start_out, 0)
        lax.fori_loop(0, nb, wait_out, 0)

    @pl.when((i == MOE_ITEMS - 1) & (f == MOE_NF - 1))
    def _():
        pk_ref[:MOE_BLOCK, :] = jnp.zeros((MOE_BLOCK, HALF), U32)

        def tail_copy(b):
            return pltpu.make_async_copy(
                pk_ref.at[pl.ds(0, MOE_BLOCK)],
                ys_hbm.at[pl.ds(pl.multiple_of(b * MOE_BLOCK, MOE_BLOCK), MOE_BLOCK)], sem)

        def start_tail(b, _):
            tail_copy(b).start()
            return 0

        def wait_tail(b, _):
            tail_copy(b).wait()
            return 0

        lax.fori_loop(tail_ref[0], n_out_blocks, start_tail, 0)
        lax.fori_loop(tail_ref[0], n_out_blocks, wait_tail, 0)


def _moe_ffn(item_e, item_start, item_nb, tail_blk, xs, w_gu, b_gu, w_d, b_d, n_rows):
    def wmap(off):
        def im(i, f, ie, is_, nb, tail):
            return ie[i], 0, off + jnp.where(nb[i] > 0, f, MOE_NF - 1)
        return im

    def dmap(i, f, ie, is_, nb, tail):
        return ie[i], jnp.where(nb[i] > 0, f, MOE_NF - 1), 0

    grid_spec = pltpu.PrefetchScalarGridSpec(
        num_scalar_prefetch=4,
        grid=(MOE_ITEMS, MOE_NF),
        in_specs=[pl.BlockSpec(memory_space=pl.ANY),
                  pl.BlockSpec((None, D_MODEL, MOE_TF), wmap(0)),
                  pl.BlockSpec((None, D_MODEL, MOE_TF), wmap(MOE_NF)),
                  pl.BlockSpec((None, 1, MOE_TF), wmap(0)),
                  pl.BlockSpec((None, 1, MOE_TF), wmap(MOE_NF)),
                  pl.BlockSpec((None, MOE_TF, D_MODEL), dmap),
                  pl.BlockSpec((None, 1, D_MODEL), lambda i, f, ie, is_, nb, tail: (ie[i], 0, 0))],
        out_specs=pl.BlockSpec(memory_space=pl.ANY),
        scratch_shapes=[pltpu.VMEM((MOE_ROWS, HALF), U32),
                        pltpu.VMEM((MOE_ROWS, D_MODEL), BF16),
                        pltpu.VMEM((MOE_ROWS, D_MODEL), F32),
                        pltpu.VMEM((D_MODEL, MOE_TF), BF16),
                        pltpu.VMEM((D_MODEL, MOE_TF), BF16),
                        pltpu.VMEM((MOE_TF, D_MODEL), BF16),
                        pltpu.SemaphoreType.DMA(())])
    return pl.pallas_call(
        _moe_kernel,
        grid_spec=grid_spec,
        out_shape=jax.ShapeDtypeStruct((n_rows, HALF), U32),
        compiler_params=_cparams(2),
        name="moe_ffn",
    )(item_e, item_start, item_nb, tail_blk, xs, w_gu, w_gu, b_gu, b_gu, w_d, b_d)


def _combine_kernel(y0_ref, y1_ref, y2_ref, y3_ref, gate_ref, h2_ref, g_ref, b_ref, o_ref):
    h2 = h2_ref[...]
    lo = DEEPNORM_ALPHA * h2[:, :HALF]
    hi = DEEPNORM_ALPHA * h2[:, HALF:]
    for k, y_ref in enumerate((y0_ref, y1_ref, y2_ref, y3_ref)):
        ylo, yhi = _unpack_halves(y_ref[...])
        gk = gate_ref[:, k:k + 1]
        lo = lo + gk * ylo
        hi = hi + gk * yhi
    mu = (jnp.sum(lo, -1, keepdims=True) + jnp.sum(hi, -1, keepdims=True)) * (1.0 / D_MODEL)
    lo = lo - mu
    hi = hi - mu
    var = (jnp.sum(lo * lo, -1, keepdims=True) + jnp.sum(hi * hi, -1, keepdims=True)) * (1.0 / D_MODEL)
    inv = lax.rsqrt(var + LN_EPS)
    o_ref[:, :HALF] = lo * inv * g_ref[:, :HALF] + b_ref[:, :HALF]
    o_ref[:, HALF:] = hi * inv * g_ref[:, HALF:] + b_ref[:, HALF:]


def _combine(yg, gates, h2, g, b, tm=256):
    t, d = h2.shape
    full = lambda a: pl.BlockSpec(a.shape, lambda i: (0,) * a.ndim)
    return pl.pallas_call(
        _combine_kernel,
        grid=(t // tm,),
        in_specs=[pl.BlockSpec((None, tm, HALF), functools.partial(lambda k, i: (k, i, 0), k))
                  for k in range(TOP_K)] + [
                  pl.BlockSpec((tm, LANES), lambda i: (i, 0)),
                  pl.BlockSpec((tm, d), lambda i: (i, 0)), full(g), full(b)],
        out_specs=pl.BlockSpec((tm, d), lambda i: (i, 0)),
        out_shape=jax.ShapeDtypeStruct((t, d), F32),
        compiler_params=_cparams(1),
        name="combine_ln",
    )(yg, yg, yg, yg, gates, h2, g, b)


def _routing_tables(sel, counts_f):
    t = sel.shape[0]
    idx = sel[:, :TOP_K].astype(jnp.int32)
    pos = sel[:, TOP_K:2 * TOP_K].astype(jnp.int32)
    counts = counts_f[0, :N_EXPERTS].astype(jnp.int32)
    padded = ((counts + MOE_BLOCK - 1) // MOE_BLOCK) * MOE_BLOCK
    pends = jnp.cumsum(padded)
    pstarts = pends - padded
    dest = (pstarts[idx] + pos).reshape(-1)
    n_rows = t * TOP_K + N_EXPERTS * MOE_BLOCK
    tok = jnp.arange(t * TOP_K, dtype=jnp.int32) // TOP_K
    row_tok = jnp.zeros((n_rows,), jnp.int32).at[dest].set(tok)

    nchunk = (padded + MOE_ROWS - 1) // MOE_ROWS
    cend = jnp.cumsum(nchunk)
    cstart = cend - nchunk
    total = cend[-1]
    it = jnp.arange(MOE_ITEMS, dtype=jnp.int32)
    valid = it < total
    it_c = jnp.minimum(it, total - 1)
    ie = jnp.searchsorted(cend, it_c, side="right").astype(jnp.int32)
    c = it_c - cstart[ie]
    istart = (pstarts[ie] + c * MOE_ROWS).astype(jnp.int32)
    inb = jnp.where(valid, jnp.minimum(MOE_ROWS, padded[ie] - c * MOE_ROWS) // MOE_BLOCK, 0)
    tail_blk = (pends[-1] // MOE_BLOCK).astype(jnp.int32).reshape(1)
    return dest, row_tok, ie, istart, inb.astype(jnp.int32), tail_blk, n_rows


def kernel(x, positions, ln_in_g, ln_in_b, w_in, q_a_norm, w_q_b, kv_a_norm, w_kv_b, sb_out_norm,
           mla_out_norm, w_o, ln_mix_g, ln_mix_b, w_router, b_router, w_gate_up, b_gate_up, w_down,
           b_down, ln_ffn_g, ln_ffn_b):
    bsz, seq, d = x.shape
    t = bsz * seq
    x2 = x.reshape(t, d)
    row = lambda v: v.reshape(1, -1)

    w_in0 = w_in[0]
    w_sb = w_in0[:, :3 * SB_WIDTH].astype(BF16)
    w_lat = jnp.pad(w_in0[:, 3 * SB_WIDTH:], ((0, 0), (0, LANES - MLA_ROPE))).astype(BF16)
    wq = w_q_b[0].reshape(MLA_Q_RANK, MLA_HEADS, MLA_NOPE + MLA_ROPE)
    w_qn = wq[:, :, :MLA_NOPE].reshape(MLA_Q_RANK, -1).astype(BF16)
    w_qp = jnp.pad(wq[:, :, MLA_NOPE:], ((0, 0), (0, 0), (0, LANES - MLA_ROPE))
                   ).reshape(MLA_Q_RANK, -1).astype(BF16)
    wkv = w_kv_b[0].reshape(MLA_KV_RANK, MLA_HEADS, MLA_NOPE + MLA_V)
    w_kn = wkv[:, :, :MLA_NOPE].reshape(MLA_KV_RANK, -1).astype(BF16)
    w_v = wkv[:, :, MLA_NOPE:].reshape(MLA_KV_RANK, -1).astype(BF16)
    w_r = jnp.pad(w_router[0], ((0, 0), (0, LANES - N_EXPERTS)))
    b_r = jnp.pad(b_router[0], (0, LANES - N_EXPERTS)).reshape(1, LANES)
    half = MLA_ROPE // 2
    inv_freq = ROPE_THETA ** (-(jnp.arange(half, dtype=F32) * 2.0 / MLA_ROPE))
    invf = jnp.concatenate([inv_freq, inv_freq, jnp.zeros((LANES - MLA_ROPE,), F32)]).reshape(1, LANES)
    pos_f = positions.reshape(t, 1).astype(F32)

    for l in range(DEPTH):
        u_sb, h_b = _ln_inproj(x2, row(ln_in_g), row(ln_in_b), w_sb)
        q_m, k_m, v_m = _mla_prep(h_b, pos_f, invf, w_lat, row(q_a_norm[l]), row(kv_a_norm[l]),
                                  w_qn, w_qp, w_kn, w_v)
        sb_o = _sb_attention(u_sb.reshape(bsz, seq, -1)).reshape(t, -1)
        mla_o = _mla_attention(q_m.reshape(bsz, seq, -1), k_m.reshape(bsz, seq, -1),
                               v_m.reshape(bsz, seq, -1)).reshape(t, -1)
        h2, h2p, sel, gates, counts_f = _mix_router(
            sb_o, mla_o, x2, row(ln_in_g), row(ln_in_b), row(sb_out_norm[l]), row(mla_out_norm[l]),
            w_o[l].astype(BF16), row(ln_mix_g[l]), row(ln_mix_b[l]), w_r, b_r)
        dest, row_tok, ie, istart, inb, tail_blk, n_rows = _routing_tables(sel, counts_f)
        xs = _gather_rows(h2p, row_tok)
        ys = _moe_ffn(ie, istart, inb, tail_blk, xs, w_gate_up[l], b_gate_up[l].reshape(N_EXPERTS, 1, -1),
                      w_down[l], b_down[l].reshape(N_EXPERTS, 1, -1), n_rows)
        yg = _gather_rows(ys, dest.reshape(t, TOP_K).T.reshape(-1)).reshape(TOP_K, t, HALF)
        out = _combine(yg, gates, h2, row(ln_ffn_g[l]), row(ln_ffn_b[l]))
    return out.reshape(bsz, seq, d)
```

```python
import functools

import jax
import jax.numpy as jnp
from jax import lax
from jax.experimental import pallas as pl
from jax.experimental.pallas import tpu as pltpu

D_MODEL = 2048
DEPTH = 1
CHUNK = 64
SB_HEADS = 8
HEAD_DIM = 128
SB_WIDTH = SB_HEADS * HEAD_DIM
MLA_HEADS = 8
MLA_Q_RANK = 512
MLA_KV_RANK = 256
MLA_NOPE = 128
MLA_ROPE = 64
MLA_V = 128
MLA_WIDTH = MLA_HEADS * MLA_V
MLA_QK_PAD = 256
ROPE_THETA = 10000.0
N_EXPERTS = 32
TOP_K = 4
D_FF = 2048
SWIGLU_LIMIT = 7.0
SWIGLU_ALPHA = 1.702
MOE_BLOCK = 128
LN_EPS = 1e-5
RMS_EPS = 1e-6
DEEPNORM_ALPHA = (2 * DEPTH) ** 0.25

LANES = 128
HALF = D_MODEL // 2
VMEM_LIMIT_BYTES = 56 * 1024 * 1024

BF16 = jnp.bfloat16
F32 = jnp.float32
U32 = jnp.uint32


def _cparams(n_axes, vmem=None, flags=None):
    return pltpu.CompilerParams(dimension_semantics=("arbitrary",) * n_axes,
                                vmem_limit_bytes=vmem or VMEM_LIMIT_BYTES, flags=flags)


def _layer_norm(x, g, b):
    mu = jnp.mean(x, -1, keepdims=True)
    xc = x - mu
    var = jnp.mean(xc * xc, -1, keepdims=True)
    return xc * lax.rsqrt(var + LN_EPS) * g + b


def _rms_norm(x, g):
    ms = jnp.mean(x * x, -1, keepdims=True)
    return x * lax.rsqrt(ms + RMS_EPS) * g


def _pack_halves(lo_f32, hi_f32):
    lo = lax.bitcast_convert_type(lo_f32.astype(BF16).astype(F32), U32)
    hi = lax.bitcast_convert_type(hi_f32.astype(BF16).astype(F32), U32)
    return lax.shift_right_logical(lo, U32(16)) | (hi & U32(0xFFFF0000))


def _unpack_halves(w):
    lo = lax.bitcast_convert_type(lax.shift_left(w, U32(16)), F32)
    hi = lax.bitcast_convert_type(w & U32(0xFFFF0000), F32)
    return lo, hi


def _ln_inproj_kernel(x_ref, g_ref, b_ref, w_ref, u_ref, h_ref):
    h = _layer_norm(x_ref[...], g_ref[...], b_ref[...]).astype(BF16)
    h_ref[...] = h
    u_ref[...] = jnp.dot(h, w_ref[...], preferred_element_type=F32).astype(BF16)


def _ln_inproj(x2, g, b, w_sb, tm=512):
    t, d = x2.shape
    n = w_sb.shape[1]
    return pl.pallas_call(
        _ln_inproj_kernel,
        grid=(t // tm,),
        in_specs=[pl.BlockSpec((tm, d), lambda i: (i, 0)),
                  pl.BlockSpec((1, d), lambda i: (0, 0)),
                  pl.BlockSpec((1, d), lambda i: (0, 0)),
                  pl.BlockSpec((d, n), lambda i: (0, 0))],
        out_specs=[pl.BlockSpec((tm, n), lambda i: (i, 0)),
                   pl.BlockSpec((tm, d), lambda i: (i, 0))],
        out_shape=[jax.ShapeDtypeStruct((t, n), BF16), jax.ShapeDtypeStruct((t, d), BF16)],
        compiler_params=_cparams(1),
        name="ln_inproj",
    )(x2, g, b, w_sb)


def _rope_slab(x, cos, sin_neg, sin_pos):
    return (x * cos + pltpu.roll(x, LANES - MLA_ROPE // 2, 1) * sin_neg
            + pltpu.roll(x, MLA_ROPE // 2, 1) * sin_pos)


def _mla_prep_kernel(h_ref, pos_ref, invf_ref, wl_ref, qg_ref, kg_ref, wqn_ref, wqp_ref,
                     wkn_ref, wv_ref, q_ref, k_ref, v_ref):
    lat = jnp.dot(h_ref[...], wl_ref[...], preferred_element_type=F32)
    qn = _rms_norm(lat[:, :MLA_Q_RANK], qg_ref[...]).astype(BF16)
    kvn = _rms_norm(lat[:, MLA_Q_RANK:MLA_Q_RANK + MLA_KV_RANK], kg_ref[...]).astype(BF16)
    k_rope = lat[:, MLA_Q_RANK + MLA_KV_RANK:]

    ang = pos_ref[...] * invf_ref[...]
    cos = jnp.cos(ang)
    sin = jnp.sin(ang)
    lane = lax.broadcasted_iota(jnp.int32, ang.shape, 1)
    sin_neg = jnp.where(lane < MLA_ROPE // 2, -sin, 0.0)
    sin_pos = jnp.where((lane >= MLA_ROPE // 2) & (lane < MLA_ROPE), sin, 0.0)

    q_nope = jnp.dot(qn, wqn_ref[...], preferred_element_type=F32)
    q_pe = jnp.dot(qn, wqp_ref[...], preferred_element_type=F32)
    k_nope = jnp.dot(kvn, wkn_ref[...], preferred_element_type=F32)
    v_ref[...] = jnp.dot(kvn, wv_ref[...], preferred_element_type=F32).astype(BF16)
    k_pe = _rope_slab(k_rope, cos, sin_neg, sin_pos).astype(BF16)
    for hd in range(MLA_HEADS):
        c0 = hd * MLA_QK_PAD
        s0 = hd * LANES
        q_ref[:, c0:c0 + LANES] = q_nope[:, s0:s0 + LANES].astype(BF16)
        q_ref[:, c0 + LANES:c0 + 2 * LANES] = _rope_slab(
            q_pe[:, s0:s0 + LANES], cos, sin_neg, sin_pos).astype(BF16)
        k_ref[:, c0:c0 + LANES] = k_nope[:, s0:s0 + LANES].astype(BF16)
        k_ref[:, c0 + LANES:c0 + 2 * LANES] = k_pe


def _mla_prep(h_b, pos_f, invf, w_lat, qg, kg, w_qn, w_qp, w_kn, w_v, tm=512):
    t, d = h_b.shape
    full = lambda a: pl.BlockSpec(a.shape, lambda i: (0,) * a.ndim)
    wq = MLA_HEADS * MLA_QK_PAD
    return pl.pallas_call(
        _mla_prep_kernel,
        grid=(t // tm,),
        in_specs=[pl.BlockSpec((tm, d), lambda i: (i, 0)),
                  pl.BlockSpec((tm, 1), lambda i: (i, 0)),
                  full(invf), full(w_lat), full(qg), full(kg), full(w_qn), full(w_qp),
                  full(w_kn), full(w_v)],
        out_specs=[pl.BlockSpec((tm, wq), lambda i: (i, 0)),
                   pl.BlockSpec((tm, wq), lambda i: (i, 0)),
                   pl.BlockSpec((tm, MLA_WIDTH), lambda i: (i, 0))],
        out_shape=[jax.ShapeDtypeStruct((t, wq), BF16), jax.ShapeDtypeStruct((t, wq), BF16),
                   jax.ShapeDtypeStruct((t, MLA_WIDTH), BF16)],
        compiler_params=_cparams(1),
        name="mla_prep",
    )(h_b, pos_f, invf, w_lat, qg, kg, w_qn, w_qp, w_kn, w_v)


ATTN_HEADS_PER_STEP = 4


def _sb_attn_kernel(q_ref, k_ref, v_ref, o_ref, *, tq, tk, heads, scale):
    i = pl.program_id(2)
    t_idx = i * tq + lax.broadcasted_iota(jnp.int32, (tq, 1), 0)
    r_i = lax.broadcasted_iota(jnp.int32, (tk, tk), 0)
    c_i = lax.broadcasted_iota(jnp.int32, (tk, tk), 1)
    suffix = (r_i >= c_i).astype(BF16)
    suffix2 = jnp.concatenate([suffix, suffix], axis=0)
    nkb = (i + 1) * (tq // tk)

    def block(j, hd, c, acc, diagonal):
        cols = slice(hd * HEAD_DIM, (hd + 1) * HEAD_DIM)
        k0 = pl.multiple_of(j * tk, tk)
        ks = k_ref[pl.ds(k0, tk), cols]
        vs = v_ref[pl.ds(k0, tk), cols]
        z = lax.dot_general(q_ref[:, cols], ks, (((1,), (1,)), ((), ())),
                            preferred_element_type=F32) * scale
        sp = jnp.maximum(z, 0.0) + jnp.log(1.0 + jnp.exp(-jnp.abs(z)))
        if diagonal:
            past = (k0 + lax.broadcasted_iota(jnp.int32, (1, tk), 1)) < t_idx
            sp = jnp.where(past, sp, 0.0)
        hi = sp.astype(BF16)
        lo = (sp - hi.astype(F32)).astype(BF16)
        incl = jnp.dot(jnp.concatenate([hi, lo], axis=1), suffix2, preferred_element_type=F32)
        w = jnp.exp(z - incl - c)
        if diagonal:
            w = jnp.where(past, w, 0.0)
        acc = acc + jnp.dot(w.astype(BF16), vs, preferred_element_type=F32)
        return c + incl[:, 0:1], acc

    bpt = tq // tk

    def trip(p, carry, diagonal):
        out = []
        for hd in range(heads):
            c, acc = carry[hd]
            for bk in range(bpt):
                c, acc = block(nkb - 1 - bpt * p - bk, hd, c, acc, diagonal)
            out.append((c, acc))
        return tuple(out)

    init = tuple((jnp.zeros((tq, 1), F32), jnp.zeros((tq, HEAD_DIM), F32)) for _ in range(heads))
    res = trip(0, init, True)
    res = lax.fori_loop(1, i + 1, functools.partial(trip, diagonal=False), res)
    for hd in range(heads):
        o_ref[:, hd * HEAD_DIM:(hd + 1) * HEAD_DIM] = res[hd][1].astype(o_ref.dtype)


def _sb_attention(u_sb3, tq=256, tk=128, heads=SB_HEADS):
    b, s, _ = u_sb3.shape
    assert tq % tk == 0 and SB_HEADS % heads == 0
    groups = SB_HEADS // heads
    w = heads * HEAD_DIM
    kern = functools.partial(_sb_attn_kernel, tq=tq, tk=tk, heads=heads, scale=HEAD_DIM ** -0.5)
    return pl.pallas_call(
        kern,
        grid=(b, groups, s // tq),
        in_specs=[pl.BlockSpec((None, tq, w), lambda bb, g, i: (bb, i, g)),
                  pl.BlockSpec((None, s, w), lambda bb, g, i: (bb, 0, groups + g)),
                  pl.BlockSpec((None, s, w), lambda bb, g, i: (bb, 0, 2 * groups + g))],
        out_specs=pl.BlockSpec((None, tq, w), lambda bb, g, i: (bb, i, g)),
        out_shape=jax.ShapeDtypeStruct((b, s, SB_WIDTH), BF16),
        compiler_params=_cparams(3),
        name="sb_attention",
    )(u_sb3, u_sb3, u_sb3)


def _mla_attn_kernel(q_ref, k_ref, v_ref, o_ref, *, tq, heads, scale):
    i = pl.program_id(2)
    t_chunk = lax.shift_right_logical(
        i * tq + lax.broadcasted_iota(jnp.int32, (tq, 1), 0), CHUNK.bit_length() - 1)

    def body(j, carry):
        k0 = pl.multiple_of(j * tq, tq)
        s_chunk = lax.shift_right_logical(
            k0 + lax.broadcasted_iota(jnp.int32, (1, tq), 1), CHUNK.bit_length() - 1)
        visible = s_chunk <= t_chunk
        out = []
        for hd in range(heads):
            m, l, acc = carry[hd]
            qk = slice(hd * MLA_QK_PAD, (hd + 1) * MLA_QK_PAD)
            ks = k_ref[pl.ds(k0, tq), qk]
            vs = v_ref[pl.ds(k0, tq), hd * MLA_V:(hd + 1) * MLA_V]
            s = lax.dot_general(q_ref[:, qk], ks, (((1,), (1,)), ((), ())),
                                preferred_element_type=F32) * scale
            s = jnp.where(visible, s, -1e30)
            m_new = jnp.maximum(m, jnp.max(s, -1, keepdims=True))
            alpha = jnp.exp(m - m_new)
            p = jnp.exp(s - m_new)
            l = alpha * l + jnp.sum(p, -1, keepdims=True)
            acc = alpha * acc + jnp.dot(p.astype(BF16), vs, preferred_element_type=F32)
            out.append((m_new, l, acc))
        return tuple(out)

    init = tuple((jnp.full((tq, 1), -1e30, F32), jnp.zeros((tq, 1), F32),
                  jnp.zeros((tq, MLA_V), F32)) for _ in range(heads))
    res = lax.fori_loop(0, i + 1, body, init)
    for hd in range(heads):
        _, l, acc = res[hd]
        o_ref[:, hd * MLA_V:(hd + 1) * MLA_V] = (acc / l).astype(o_ref.dtype)


def _mla_attention(q3, k3, v3, tq=256, heads=ATTN_HEADS_PER_STEP):
    b, s, _ = q3.shape
    assert MLA_HEADS % heads == 0
    kern = functools.partial(_mla_attn_kernel, tq=tq, heads=heads,
                             scale=(MLA_NOPE + MLA_ROPE) ** -0.5)
    return pl.pallas_call(
        kern,
        grid=(b, MLA_HEADS // heads, s // tq),
        in_specs=[pl.BlockSpec((None, tq, heads * MLA_QK_PAD), lambda bb, g, i: (bb, i, g)),
                  pl.BlockSpec((None, s, heads * MLA_QK_PAD), lambda bb, g, i: (bb, 0, g)),
                  pl.BlockSpec((None, s, heads * MLA_V), lambda bb, g, i: (bb, 0, g))],
        out_specs=pl.BlockSpec((None, tq, heads * MLA_V), lambda bb, g, i: (bb, i, g)),
        out_shape=jax.ShapeDtypeStruct((b, s, MLA_WIDTH), BF16),
        compiler_params=_cparams(3),
        name="mla_attention",
    )(q3, k3, v3)


def _mix_router_kernel(sb_ref, mla_ref, x_ref, lig_ref, lib_ref, sg_ref, mg_ref, wo_ref,
                       lmg_ref, lmb_ref, wr_ref, br_ref,
                       h2_ref, h2p_ref, sel_ref, gate_ref, cnt_ref, run_ref, *, tm):
    step = pl.program_id(0)

    @pl.when(step == 0)
    def _():
        run_ref[...] = jnp.zeros_like(run_ref)

    a = _rms_norm(sb_ref[...].astype(F32), sg_ref[...]).astype(BF16)
    bm = _rms_norm(mla_ref[...].astype(F32), mg_ref[...]).astype(BF16)
    mix = (jnp.dot(a, wo_ref[:SB_WIDTH, :], preferred_element_type=F32)
           + jnp.dot(bm, wo_ref[SB_WIDTH:, :], preferred_element_type=F32))
    h = _layer_norm(x_ref[...], lig_ref[...], lib_ref[...])
    h2 = _layer_norm(DEEPNORM_ALPHA * h + mix, lmg_ref[...], lmb_ref[...])
    h2_ref[...] = h2
    h2p_ref[...] = _pack_halves(h2[:, :HALF], h2[:, HALF:])

    h2_hi = h2.astype(BF16)
    h2_lo = (h2 - h2_hi.astype(F32)).astype(BF16)
    hw = jnp.dot(h2_hi, wr_ref[...], preferred_element_type=F32)
    logits = (hw[:, :LANES] + hw[:, LANES:]
              + jnp.dot(h2_lo, wr_ref[:, :LANES], preferred_element_type=F32) + br_ref[...])
    lane = lax.broadcasted_iota(jnp.int32, (tm, LANES), 1)
    lane_f = lane.astype(F32)
    logits = jnp.where(lane < N_EXPERTS, logits, -jnp.inf)
    vals, hots = [], []
    sel = jnp.zeros((tm, LANES), F32)
    for k in range(TOP_K):
        m = jnp.max(logits, -1, keepdims=True)
        idx = jnp.min(jnp.where(logits == m, lane_f, float(LANES)), -1, keepdims=True)
        hot = lane_f == idx
        logits = jnp.where(hot, -jnp.inf, logits)
        vals.append(m)
        hots.append(hot)
        sel = jnp.where(lane == k, idx, sel)
    exps = [jnp.exp(v - vals[0]) for v in vals]
    denom = exps[0] + exps[1] + exps[2] + exps[3]
    gates = jnp.zeros((tm, LANES), F32)
    for k in range(TOP_K):
        gates = jnp.where(lane == k, exps[k] / denom, gates)
    gate_ref[...] = gates

    onehot = (hots[0] | hots[1] | hots[2] | hots[3]).astype(F32)
    r_i = lax.broadcasted_iota(jnp.int32, (tm, tm), 0)
    c_i = lax.broadcasted_iota(jnp.int32, (tm, tm), 1)
    before = (c_i < r_i).astype(BF16)
    rank = jnp.dot(before, onehot.astype(BF16), preferred_element_type=F32) + run_ref[...]
    for k in range(TOP_K):
        pos = jnp.sum(jnp.where(hots[k], rank, 0.0), -1, keepdims=True)
        sel = jnp.where(lane == TOP_K + k, pos, sel)
    sel_ref[...] = sel
    run_ref[...] = run_ref[...] + jnp.sum(onehot, 0, keepdims=True)
    cnt_ref[...] = run_ref[...]


def _mix_router(sb_o, mla_o, x2, lig, lib, sg, mg, w_o, lmg, lmb, w_r, b_r, tm=256):
    t, d = x2.shape
    full = lambda a: pl.BlockSpec(a.shape, lambda i: (0,) * a.ndim)
    row = lambda w: pl.BlockSpec((tm, w), lambda i: (i, 0))
    return pl.pallas_call(
        functools.partial(_mix_router_kernel, tm=tm),
        grid=(t // tm,),
        in_specs=[row(SB_WIDTH), row(MLA_WIDTH), row(d), full(lig), full(lib), full(sg), full(mg),
                  full(w_o), full(lmg), full(lmb), full(w_r), full(b_r)],
        out_specs=[row(d), row(HALF), row(LANES), row(LANES),
                   pl.BlockSpec((1, LANES), lambda i: (0, 0))],
        out_shape=[jax.ShapeDtypeStruct((t, d), F32), jax.ShapeDtypeStruct((t, HALF), U32),
                   jax.ShapeDtypeStruct((t, LANES), F32), jax.ShapeDtypeStruct((t, LANES), F32),
                   jax.ShapeDtypeStruct((1, LANES), F32)],
        scratch_shapes=[pltpu.VMEM((1, LANES), F32)],
        compiler_params=_cparams(1),
        name="mix_router",
    )(sb_o, mla_o, x2, lig, lib, sg, mg, w_o, lmg, lmb, w_r, b_r)


def _gather_kernel(idx_ref, src_hbm, out_ref, sem, *, rows):
    def issue(p, _):
        for lane in range(2):
            r = 2 * p + lane
            src = idx_ref[0, 0, r]
            pltpu.make_async_copy(src_hbm.at[pl.ds(src, 1)], out_ref.at[pl.ds(r, 1)],
                                  sem).start(priority=lane)
        return 0

    lax.fori_loop(0, rows // 2, issue, 0, unroll=4)
    pltpu.make_async_copy(src_hbm.at[pl.ds(0, rows)], out_ref, sem).wait()


def _gather_rows(src, idx, rows=512):
    n = idx.shape[0]
    w = src.shape[1]
    idx3 = idx.reshape(n // rows, 1, rows)
    return pl.pallas_call(
        functools.partial(_gather_kernel, rows=rows),
        grid=(n // rows,),
        in_specs=[pl.BlockSpec((1, 1, rows), lambda i: (i, 0, 0), memory_space=pltpu.SMEM),
                  pl.BlockSpec(memory_space=pl.ANY)],
        out_specs=pl.BlockSpec((rows, w), lambda i: (i, 0)),
        out_shape=jax.ShapeDtypeStruct((n, w), src.dtype),
        scratch_shapes=[pltpu.SemaphoreType.DMA(())],
        compiler_params=_cparams(1),
        name="gather_rows",
    )(idx3, src)


MOE_ROWS = 2048
MOE_SUB = 256
MOE_TF = 256
MOE_NF = D_FF // MOE_TF
MOE_ITEMS = N_EXPERTS + (N_EXPERTS * (MOE_BLOCK - 1) + 8192 * TOP_K) // MOE_ROWS + 1


def _moe_kernel(ie_ref, is_ref, nb_ref, tail_ref, xs_hbm, wg_ref, wu_ref, bg_ref, bu_ref, wd_ref,
                bd_ref, ys_hbm, pk_ref, x_ref, acc_ref, wgb_ref, wub_ref, wdb_ref, sem):
    i = pl.program_id(0)
    f = pl.program_id(1)
    nb = nb_ref[i]
    start = is_ref[i]
    n_out_blocks = ys_hbm.shape[0] // MOE_BLOCK

    def block_copy(b, to_vmem):
        hbm = (xs_hbm if to_vmem else ys_hbm).at[pl.ds(pl.multiple_of(start + b * MOE_BLOCK, MOE_BLOCK),
                                                        MOE_BLOCK)]
        vm = pk_ref.at[pl.ds(pl.multiple_of(b * MOE_BLOCK, MOE_BLOCK), MOE_BLOCK)]
        return pltpu.make_async_copy(hbm, vm, sem) if to_vmem else pltpu.make_async_copy(vm, hbm, sem)

    @pl.when((i == 0) & (f == 0))
    def _():
        x_ref[...] = jnp.zeros_like(x_ref)
        acc_ref[...] = jnp.zeros_like(acc_ref)

    @pl.when((f == 0) & (nb > 0))
    def _():
        def start_in(b, _):
            block_copy(b, True).start()
            return 0

        def wait_in(b, _):
            block_copy(b, True).wait()
            return 0

        def unpack(b, _):
            r0 = pl.multiple_of(b * MOE_BLOCK, MOE_BLOCK)
            lo, hi = _unpack_halves(pk_ref[pl.ds(r0, MOE_BLOCK), :])
            x_ref[pl.ds(r0, MOE_BLOCK), :HALF] = lo.astype(BF16)
            x_ref[pl.ds(r0, MOE_BLOCK), HALF:] = hi.astype(BF16)
            return 0

        lax.fori_loop(0, nb, start_in, 0)
        lax.fori_loop(0, nb, wait_in, 0)
        lax.fori_loop(0, nb, unpack, 0)

    @pl.when(nb > 0)
    def _():
        n_sub = (nb * MOE_BLOCK + MOE_SUB - 1) // MOE_SUB

        def gate_up(s, wg, wu):
            x = x_ref[pl.ds(pl.multiple_of(s * MOE_SUB, MOE_SUB), MOE_SUB), :]
            g = jnp.dot(x, wg, preferred_element_type=F32) + bg_ref[...]
            u = jnp.dot(x, wu, preferred_element_type=F32) + bu_ref[...]
            g = jnp.minimum(g, SWIGLU_LIMIT)
            u = jnp.clip(u, -SWIGLU_LIMIT, SWIGLU_LIMIT)
            return ((u + 1.0) * (g * jax.nn.sigmoid(SWIGLU_ALPHA * g))).astype(BF16)

        def down(s, act, wd):
            r0 = pl.multiple_of(s * MOE_SUB, MOE_SUB)
            y = jnp.dot(act, wd, preferred_element_type=F32)
            prev = jnp.where(f == 0, bd_ref[...], acc_ref[pl.ds(r0, MOE_SUB), :])
            acc_ref[pl.ds(r0, MOE_SUB), :] = prev + y

        wg = wg_ref[...].astype(BF16)
        wu = wu_ref[...].astype(BF16)
        wgb_ref[...] = wg
        wub_ref[...] = wu
        act0 = gate_up(0, wg, wu)
        wdb_ref[...] = wd_ref[...].astype(BF16)

        def step(s, act):
            down(s - 1, act, wdb_ref[...])
            return gate_up(s, wgb_ref[...], wub_ref[...])

        def two_steps(p, act):
            return step(2 * p + 2, step(2 * p + 1, act))

        act = lax.fori_loop(0, (n_sub - 1) // 2, two_steps, act0)
        act = lax.cond((n_sub - 1) % 2 == 1, lambda a: step(n_sub - 1, a), lambda a: a, act)
        down(n_sub - 1, act, wdb_ref[...])

    @pl.when((f == MOE_NF - 1) & (nb > 0))
    def _():
        def pack(b, _):
            r0 = pl.multiple_of(b * MOE_BLOCK, MOE_BLOCK)
            y = acc_ref[pl.ds(r0, MOE_BLOCK), :]
            pk_ref[pl.ds(r0, MOE_BLOCK), :] = _pack_halves(y[:, :HALF], y[:, HALF:])
            return 0

        def start_out(b, _):
            block_copy(b, False).start()
            return 0

        def wait_out(b, _):
            block_copy(b, False).wait()
            return 0

        lax.fori_loop(0, nb, pack, 0)
        lax.fori_loop(0, nb, start_out, 0)
        lax.fori_loop(0, nb, wait_out, 0)

    @pl.when((i == MOE_ITEMS - 1) & (f == MOE_NF - 1))
    def _():
        pk_ref[:MOE_BLOCK, :] = jnp.zeros((MOE_BLOCK, HALF), U32)

        def tail_copy(b):
            return pltpu.make_async_copy(
                pk_ref.at[pl.ds(0, MOE_BLOCK)],
                ys_hbm.at[pl.ds(pl.multiple_of(b * MOE_BLOCK, MOE_BLOCK), MOE_BLOCK)], sem)

        def start_tail(b, _):
            tail_copy(b).start()
            return 0

        def wait_tail(b, _):
            tail_copy(b).wait()
            return 0

        lax.fori_loop(tail_ref[0], n_out_blocks, start_tail, 0)
        lax.fori_loop(tail_ref[0], n_out_blocks, wait_tail, 0)


def _moe_ffn(item_e, item_start, item_nb, tail_blk, xs, w_gu, b_gu, w_d, b_d, n_rows):
    def wmap(off):
        def im(i, f, ie, is_, nb, tail):
            return ie[i], 0, off + jnp.where(nb[i] > 0, f, MOE_NF - 1)
        return im

    def dmap(i, f, ie, is_, nb, tail):
        return ie[i], jnp.where(nb[i] > 0, f, MOE_NF - 1), 0

    grid_spec = pltpu.PrefetchScalarGridSpec(
        num_scalar_prefetch=4,
        grid=(MOE_ITEMS, MOE_NF),
        in_specs=[pl.BlockSpec(memory_space=pl.ANY),
                  pl.BlockSpec((None, D_MODEL, MOE_TF), wmap(0)),
                  pl.BlockSpec((None, D_MODEL, MOE_TF), wmap(MOE_NF)),
                  pl.BlockSpec((None, 1, MOE_TF), wmap(0)),
                  pl.BlockSpec((None, 1, MOE_TF), wmap(MOE_NF)),
                  pl.BlockSpec((None, MOE_TF, D_MODEL), dmap),
                  pl.BlockSpec((None, 1, D_MODEL), lambda i, f, ie, is_, nb, tail: (ie[i], 0, 0))],
        out_specs=pl.BlockSpec(memory_space=pl.ANY),
        scratch_shapes=[pltpu.VMEM((MOE_ROWS, HALF), U32),
                        pltpu.VMEM((MOE_ROWS, D_MODEL), BF16),
                        pltpu.VMEM((MOE_ROWS, D_MODEL), F32),
                        pltpu.VMEM((D_MODEL, MOE_TF), BF16),
                        pltpu.VMEM((D_MODEL, MOE_TF), BF16),
                        pltpu.VMEM((MOE_TF, D_MODEL), BF16),
                        pltpu.SemaphoreType.DMA(())])
    return pl.pallas_call(
        _moe_kernel,
        grid_spec=grid_spec,
        out_shape=jax.ShapeDtypeStruct((n_rows, HALF), U32),
        compiler_params=_cparams(2),
        name="moe_ffn",
    )(item_e, item_start, item_nb, tail_blk, xs, w_gu, w_gu, b_gu, b_gu, w_d, b_d)


def _combine_kernel(y0_ref, y1_ref, y2_ref, y3_ref, gate_ref, h2_ref, g_ref, b_ref, o_ref):
    h2 = h2_ref[...]
    lo = DEEPNORM_ALPHA * h2[:, :HALF]
    hi = DEEPNORM_ALPHA * h2[:, HALF:]
    for k, y_ref in enumerate((y0_ref, y1_ref, y2_ref, y3_ref)):
        ylo, yhi = _unpack_halves(y_ref[...])
        gk = gate_ref[:, k:k + 1]
        lo = lo + gk * ylo
        hi = hi + gk * yhi
    mu = (jnp.sum(lo, -1, keepdims=True) + jnp.sum(hi, -1, keepdims=True)) * (1.0 / D_MODEL)
    lo = lo - mu
    hi = hi - mu
    var = (jnp.sum(lo * lo, -1, keepdims=True) + jnp.sum(hi * hi, -1, keepdims=True)) * (1.0 / D_MODEL)
    inv = lax.rsqrt(var + LN_EPS)
    o_ref[:, :HALF] = lo * inv * g_ref[:, :HALF] + b_ref[:, :HALF]
    o_ref[:, HALF:] = hi * inv * g_ref[:, HALF:] + b_ref[:, HALF:]


def _combine(yg, gates, h2, g, b, tm=256):
    t, d = h2.shape
    full = lambda a: pl.BlockSpec(a.shape, lambda i: (0,) * a.ndim)
    return pl.pallas_call(
        _combine_kernel,
        grid=(t // tm,),
        in_specs=[pl.BlockSpec((None, tm, HALF), functools.partial(lambda k, i: (k, i, 0), k))
                  for k in range(TOP_K)] + [
                  pl.BlockSpec((tm, LANES), lambda i: (i, 0)),
                  pl.BlockSpec((tm, d), lambda i: (i, 0)), full(g), full(b)],
        out_specs=pl.BlockSpec((tm, d), lambda i: (i, 0)),
        out_shape=jax.ShapeDtypeStruct((t, d), F32),
        compiler_params=_cparams(1),
        name="combine_ln",
    )(yg, yg, yg, yg, gates, h2, g, b)


def _routing_tables(sel, counts_f):
    t = sel.shape[0]
    idx = sel[:, :TOP_K].astype(jnp.int32)
    pos = sel[:, TOP_K:2 * TOP_K].astype(jnp.int32)
    counts = counts_f[0, :N_EXPERTS].astype(jnp.int32)
    padded = ((counts + MOE_BLOCK - 1) // MOE_BLOCK) * MOE_BLOCK
    pends = jnp.cumsum(padded)
    pstarts = pends - padded
    dest = (pstarts[idx] + pos).reshape(-1)
    n_rows = t * TOP_K + N_EXPERTS * MOE_BLOCK
    tok = jnp.arange(t * TOP_K, dtype=jnp.int32) // TOP_K
    row_tok = jnp.zeros((n_rows,), jnp.int32).at[dest].set(tok)

    nchunk = (padded + MOE_ROWS - 1) // MOE_ROWS
    cend = jnp.cumsum(nchunk)
    cstart = cend - nchunk
    total = cend[-1]
    it = jnp.arange(MOE_ITEMS, dtype=jnp.int32)
    valid = it < total
    it_c = jnp.minimum(it, total - 1)
    ie = jnp.searchsorted(cend, it_c, side="right").astype(jnp.int32)
    c = it_c - cstart[ie]
    istart = (pstarts[ie] + c * MOE_ROWS).astype(jnp.int32)
    inb = jnp.where(valid, jnp.minimum(MOE_ROWS, padded[ie] - c * MOE_ROWS) // MOE_BLOCK, 0)
    tail_blk = (pends[-1] // MOE_BLOCK).astype(jnp.int32).reshape(1)
    return dest, row_tok, ie, istart, inb.astype(jnp.int32), tail_blk, n_rows


def kernel(x, positions, ln_in_g, ln_in_b, w_in, q_a_norm, w_q_b, kv_a_norm, w_kv_b, sb_out_norm,
           mla_out_norm, w_o, ln_mix_g, ln_mix_b, w_router, b_router, w_gate_up, b_gate_up, w_down,
           b_down, ln_ffn_g, ln_ffn_b):
    bsz, seq, d = x.shape
    t = bsz * seq
    x2 = x.reshape(t, d)
    row = lambda v: v.reshape(1, -1)

    w_in0 = w_in[0]
    w_sb = w_in0[:, :3 * SB_WIDTH].astype(BF16)
    w_lat = jnp.pad(w_in0[:, 3 * SB_WIDTH:], ((0, 0), (0, LANES - MLA_ROPE))).astype(BF16)
    wq = w_q_b[0].reshape(MLA_Q_RANK, MLA_HEADS, MLA_NOPE + MLA_ROPE)
    w_qn = wq[:, :, :MLA_NOPE].reshape(MLA_Q_RANK, -1).astype(BF16)
    w_qp = jnp.pad(wq[:, :, MLA_NOPE:], ((0, 0), (0, 0), (0, LANES - MLA_ROPE))
                   ).reshape(MLA_Q_RANK, -1).astype(BF16)
    wkv = w_kv_b[0].reshape(MLA_KV_RANK, MLA_HEADS, MLA_NOPE + MLA_V)
    w_kn = wkv[:, :, :MLA_NOPE].reshape(MLA_KV_RANK, -1).astype(BF16)
    w_v = wkv[:, :, MLA_NOPE:].reshape(MLA_KV_RANK, -1).astype(BF16)
    w_r32 = jnp.pad(w_router[0], ((0, 0), (0, LANES - N_EXPERTS)))
    w_r_hi = w_r32.astype(BF16)
    w_r = jnp.concatenate([w_r_hi, (w_r32 - w_r_hi.astype(F32)).astype(BF16)], axis=1)
    b_r = jnp.pad(b_router[0], (0, LANES - N_EXPERTS)).reshape(1, LANES)
    half = MLA_ROPE // 2
    inv_freq = ROPE_THETA ** (-(jnp.arange(half, dtype=F32) * 2.0 / MLA_ROPE))
    invf = jnp.concatenate([inv_freq, inv_freq, jnp.zeros((LANES - MLA_ROPE,), F32)]).reshape(1, LANES)
    pos_f = positions.reshape(t, 1).astype(F32)

    for l in range(DEPTH):
        u_sb, h_b = _ln_inproj(x2, row(ln_in_g), row(ln_in_b), w_sb)
        q_m, k_m, v_m = _mla_prep(h_b, pos_f, invf, w_lat, row(q_a_norm[l]), row(kv_a_norm[l]),
                                  w_qn, w_qp, w_kn, w_v)
        sb_o = _sb_attention(u_sb.reshape(bsz, seq, -1)).reshape(t, -1)
        mla_o = _mla_attention(q_m.reshape(bsz, seq, -1), k_m.reshape(bsz, seq, -1),
                               v_m.reshape(bsz, seq, -1)).reshape(t, -1)
        h2, h2p, sel, gates, counts_f = _mix_router(
            sb_o, mla_o, x2, row(ln_in_g), row(ln_in_b), row(sb_out_norm[l]), row(mla_out_norm[l]),
            w_o[l].astype(BF16), row(ln_mix_g[l]), row(ln_mix_b[l]), w_r, b_r)
        dest, row_tok, ie, istart, inb, tail_blk, n_rows = _routing_tables(sel, counts_f)
        xs = _gather_rows(h2p, row_tok)
        ys = _moe_ffn(ie, istart, inb, tail_blk, xs, w_gate_up[l], b_gate_up[l].reshape(N_EXPERTS, 1, -1),
                      w_down[l], b_down[l].reshape(N_EXPERTS, 1, -1), n_rows)
        yg = _gather_rows(ys, dest.reshape(t, TOP_K).T.reshape(-1)).reshape(TOP_K, t, HALF)
        out = _combine(yg, gates, h2, row(ln_ffn_g[l]), row(ln_ffn_b[l]))
    return out.reshape(bsz, seq, d)
```

```python
import functools

import jax
import jax.numpy as jnp
from jax import lax
from jax.experimental import pallas as pl
from jax.experimental.pallas import tpu as pltpu

D_MODEL = 2048
DEPTH = 1
CHUNK = 64
SB_HEADS = 8
HEAD_DIM = 128
SB_WIDTH = SB_HEADS * HEAD_DIM
MLA_HEADS = 8
MLA_Q_RANK = 512
MLA_KV_RANK = 256
MLA_NOPE = 128
MLA_ROPE = 64
MLA_V = 128
MLA_WIDTH = MLA_HEADS * MLA_V
MLA_QK_PAD = 256
ROPE_THETA = 10000.0
N_EXPERTS = 32
TOP_K = 4
D_FF = 2048
SWIGLU_LIMIT = 7.0
SWIGLU_ALPHA = 1.702
MOE_BLOCK = 128
LN_EPS = 1e-5
RMS_EPS = 1e-6
DEEPNORM_ALPHA = (2 * DEPTH) ** 0.25

LANES = 128
HALF = D_MODEL // 2
VMEM_LIMIT_BYTES = 56 * 1024 * 1024

BF16 = jnp.bfloat16
F32 = jnp.float32
U32 = jnp.uint32


def _cparams(n_axes, vmem=None, flags=None):
    return pltpu.CompilerParams(dimension_semantics=("arbitrary",) * n_axes,
                                vmem_limit_bytes=vmem or VMEM_LIMIT_BYTES, flags=flags)


def _layer_norm(x, g, b):
    mu = jnp.mean(x, -1, keepdims=True)
    xc = x - mu
    var = jnp.mean(xc * xc, -1, keepdims=True)
    return xc * lax.rsqrt(var + LN_EPS) * g + b


def _rms_norm(x, g):
    ms = jnp.mean(x * x, -1, keepdims=True)
    return x * lax.rsqrt(ms + RMS_EPS) * g


def _pack_halves(lo_f32, hi_f32):
    lo = lax.bitcast_convert_type(lo_f32.astype(BF16).astype(F32), U32)
    hi = lax.bitcast_convert_type(hi_f32.astype(BF16).astype(F32), U32)
    return lax.shift_right_logical(lo, U32(16)) | (hi & U32(0xFFFF0000))


def _unpack_halves(w):
    lo = lax.bitcast_convert_type(lax.shift_left(w, U32(16)), F32)
    hi = lax.bitcast_convert_type(w & U32(0xFFFF0000), F32)
    return lo, hi


def _ln_inproj_kernel(x_ref, g_ref, b_ref, w_ref, u_ref, h_ref):
    h = _layer_norm(x_ref[...], g_ref[...], b_ref[...]).astype(BF16)
    h_ref[...] = h
    u_ref[...] = jnp.dot(h, w_ref[...], preferred_element_type=F32).astype(BF16)


def _ln_inproj(x2, g, b, w_sb, tm=512):
    t, d = x2.shape
    n = w_sb.shape[1]
    return pl.pallas_call(
        _ln_inproj_kernel,
        grid=(t // tm,),
        in_specs=[pl.BlockSpec((tm, d), lambda i: (i, 0)),
                  pl.BlockSpec((1, d), lambda i: (0, 0)),
                  pl.BlockSpec((1, d), lambda i: (0, 0)),
                  pl.BlockSpec((d, n), lambda i: (0, 0))],
        out_specs=[pl.BlockSpec((tm, n), lambda i: (i, 0)),
                   pl.BlockSpec((tm, d), lambda i: (i, 0))],
        out_shape=[jax.ShapeDtypeStruct((t, n), BF16), jax.ShapeDtypeStruct((t, d), BF16)],
        compiler_params=_cparams(1),
        name="ln_inproj",
    )(x2, g, b, w_sb)


def _rope_slab(x, cos, sin_neg, sin_pos):
    return (x * cos + pltpu.roll(x, LANES - MLA_ROPE // 2, 1) * sin_neg
            + pltpu.roll(x, MLA_ROPE // 2, 1) * sin_pos)


def _mla_prep_kernel(h_ref, pos_ref, invf_ref, wl_ref, qg_ref, kg_ref, wqn_ref, wqp_ref,
                     wkn_ref, wv_ref, q_ref, k_ref, v_ref):
    lat = jnp.dot(h_ref[...], wl_ref[...], preferred_element_type=F32)
    qn = _rms_norm(lat[:, :MLA_Q_RANK], qg_ref[...]).astype(BF16)
    kvn = _rms_norm(lat[:, MLA_Q_RANK:MLA_Q_RANK + MLA_KV_RANK], kg_ref[...]).astype(BF16)
    k_rope = lat[:, MLA_Q_RANK + MLA_KV_RANK:]

    ang = pos_ref[...] * invf_ref[...]
    cos = jnp.cos(ang)
    sin = jnp.sin(ang)
    lane = lax.broadcasted_iota(jnp.int32, ang.shape, 1)
    sin_neg = jnp.where(lane < MLA_ROPE // 2, -sin, 0.0)
    sin_pos = jnp.where((lane >= MLA_ROPE // 2) & (lane < MLA_ROPE), sin, 0.0)

    q_nope = jnp.dot(qn, wqn_ref[...], preferred_element_type=F32)
    q_pe = jnp.dot(qn, wqp_ref[...], preferred_element_type=F32)
    k_nope = jnp.dot(kvn, wkn_ref[...], preferred_element_type=F32)
    v_ref[...] = jnp.dot(kvn, wv_ref[...], preferred_element_type=F32).astype(BF16)
    k_pe = _rope_slab(k_rope, cos, sin_neg, sin_pos).astype(BF16)
    for hd in range(MLA_HEADS):
        c0 = hd * MLA_QK_PAD
        s0 = hd * LANES
        q_ref[:, c0:c0 + LANES] = q_nope[:, s0:s0 + LANES].astype(BF16)
        q_ref[:, c0 + LANES:c0 + 2 * LANES] = _rope_slab(
            q_pe[:, s0:s0 + LANES], cos, sin_neg, sin_pos).astype(BF16)
        k_ref[:, c0:c0 + LANES] = k_nope[:, s0:s0 + LANES].astype(BF16)
        k_ref[:, c0 + LANES:c0 + 2 * LANES] = k_pe


def _mla_prep(h_b, pos_f, invf, w_lat, qg, kg, w_qn, w_qp, w_kn, w_v, tm=512):
    t, d = h_b.shape
    full = lambda a: pl.BlockSpec(a.shape, lambda i: (0,) * a.ndim)
    wq = MLA_HEADS * MLA_QK_PAD
    return pl.pallas_call(
        _mla_prep_kernel,
        grid=(t // tm,),
        in_specs=[pl.BlockSpec((tm, d), lambda i: (i, 0)),
                  pl.BlockSpec((tm, 1), lambda i: (i, 0)),
                  full(invf), full(w_lat), full(qg), full(kg), full(w_qn), full(w_qp),
                  full(w_kn), full(w_v)],
        out_specs=[pl.BlockSpec((tm, wq), lambda i: (i, 0)),
                   pl.BlockSpec((tm, wq), lambda i: (i, 0)),
                   pl.BlockSpec((tm, MLA_WIDTH), lambda i: (i, 0))],
        out_shape=[jax.ShapeDtypeStruct((t, wq), BF16), jax.ShapeDtypeStruct((t, wq), BF16),
                   jax.ShapeDtypeStruct((t, MLA_WIDTH), BF16)],
        compiler_params=_cparams(1),
        name="mla_prep",
    )(h_b, pos_f, invf, w_lat, qg, kg, w_qn, w_qp, w_kn, w_v)


def _sb_attn_kernel(q_ref, k_ref, v_ref, o_ref, *, tq, tk, heads, scale):
    i = pl.program_id(2)
    t_idx = i * tq + lax.broadcasted_iota(jnp.int32, (tq, 1), 0)
    r_i = lax.broadcasted_iota(jnp.int32, (tk, tk), 0)
    c_i = lax.broadcasted_iota(jnp.int32, (tk, tk), 1)
    suffix = (r_i >= c_i).astype(BF16)
    suffix2 = jnp.concatenate([suffix, suffix], axis=0)
    nkb = (i + 1) * (tq // tk)

    def block(j, hd, c, acc, diagonal):
        cols = slice(hd * HEAD_DIM, (hd + 1) * HEAD_DIM)
        k0 = pl.multiple_of(j * tk, tk)
        ks = k_ref[pl.ds(k0, tk), cols]
        vs = v_ref[pl.ds(k0, tk), cols]
        z = lax.dot_general(q_ref[:, cols], ks, (((1,), (1,)), ((), ())),
                            preferred_element_type=F32) * scale
        sp = jnp.maximum(z, 0.0) + jnp.log(1.0 + jnp.exp(-jnp.abs(z)))
        if diagonal:
            past = (k0 + lax.broadcasted_iota(jnp.int32, (1, tk), 1)) < t_idx
            sp = jnp.where(past, sp, 0.0)
        hi = sp.astype(BF16)
        lo = (sp - hi.astype(F32)).astype(BF16)
        incl = jnp.dot(jnp.concatenate([hi, lo], axis=1), suffix2, preferred_element_type=F32)
        w = jnp.exp(z - incl - c)
        if diagonal:
            w = jnp.where(past, w, 0.0)
        acc = acc + jnp.dot(w.astype(BF16), vs, preferred_element_type=F32)
        return c + incl[:, 0:1], acc

    bpt = tq // tk

    def trip(p, carry, diagonal):
        out = []
        for hd in range(heads):
            c, acc = carry[hd]
            for bk in range(bpt):
                c, acc = block(nkb - 1 - bpt * p - bk, hd, c, acc, diagonal)
            out.append((c, acc))
        return tuple(out)

    init = tuple((jnp.zeros((tq, 1), F32), jnp.zeros((tq, HEAD_DIM), F32)) for _ in range(heads))
    res = trip(0, init, True)
    res = lax.fori_loop(1, i + 1, functools.partial(trip, diagonal=False), res)
    for hd in range(heads):
        o_ref[:, hd * HEAD_DIM:(hd + 1) * HEAD_DIM] = res[hd][1].astype(o_ref.dtype)


def _sb_attention(u_sb3, tq=256, tk=128, heads=SB_HEADS):
    b, s, _ = u_sb3.shape
    assert tq % tk == 0 and SB_HEADS % heads == 0
    groups = SB_HEADS // heads
    w = heads * HEAD_DIM
    kern = functools.partial(_sb_attn_kernel, tq=tq, tk=tk, heads=heads, scale=HEAD_DIM ** -0.5)
    return pl.pallas_call(
        kern,
        grid=(b, groups, s // tq),
        in_specs=[pl.BlockSpec((None, tq, w), lambda bb, g, i: (bb, i, g)),
                  pl.BlockSpec((None, s, w), lambda bb, g, i: (bb, 0, groups + g)),
                  pl.BlockSpec((None, s, w), lambda bb, g, i: (bb, 0, 2 * groups + g))],
        out_specs=pl.BlockSpec((None, tq, w), lambda bb, g, i: (bb, i, g)),
        out_shape=jax.ShapeDtypeStruct((b, s, SB_WIDTH), BF16),
        compiler_params=_cparams(3),
        name="sb_attention",
    )(u_sb3, u_sb3, u_sb3)


def _mla_attn_kernel(q_ref, k_ref, v_ref, o_ref, *, tq, heads, scale):
    i = pl.program_id(2)
    t_chunk = lax.shift_right_logical(
        i * tq + lax.broadcasted_iota(jnp.int32, (tq, 1), 0), CHUNK.bit_length() - 1)

    def body(j, carry):
        k0 = pl.multiple_of(j * tq, tq)
        s_chunk = lax.shift_right_logical(
            k0 + lax.broadcasted_iota(jnp.int32, (1, tq), 1), CHUNK.bit_length() - 1)
        visible = s_chunk <= t_chunk
        out = []
        for hd in range(heads):
            m, l, acc = carry[hd]
            qk = slice(hd * MLA_QK_PAD, (hd + 1) * MLA_QK_PAD)
            ks = k_ref[pl.ds(k0, tq), qk]
            vs = v_ref[pl.ds(k0, tq), hd * MLA_V:(hd + 1) * MLA_V]
            s = lax.dot_general(q_ref[:, qk], ks, (((1,), (1,)), ((), ())),
                                preferred_element_type=F32) * scale
            s = jnp.where(visible, s, -1e30)
            m_new = jnp.maximum(m, jnp.max(s, -1, keepdims=True))
            alpha = jnp.exp(m - m_new)
            p = jnp.exp(s - m_new)
            l = alpha * l + jnp.sum(p, -1, keepdims=True)
            acc = alpha * acc + jnp.dot(p.astype(BF16), vs, preferred_element_type=F32)
            out.append((m_new, l, acc))
        return tuple(out)

    init = tuple((jnp.full((tq, 1), -1e30, F32), jnp.zeros((tq, 1), F32),
                  jnp.zeros((tq, MLA_V), F32)) for _ in range(heads))
    res = lax.fori_loop(0, i + 1, body, init)
    for hd in range(heads):
        _, l, acc = res[hd]
        o_ref[:, hd * MLA_V:(hd + 1) * MLA_V] = (acc / l).astype(o_ref.dtype)


def _mla_attention(q3, k3, v3, tq=256, heads=MLA_HEADS):
    b, s, _ = q3.shape
    assert MLA_HEADS % heads == 0
    kern = functools.partial(_mla_attn_kernel, tq=tq, heads=heads,
                             scale=(MLA_NOPE + MLA_ROPE) ** -0.5)
    return pl.pallas_call(
        kern,
        grid=(b, MLA_HEADS // heads, s // tq),
        in_specs=[pl.BlockSpec((None, tq, heads * MLA_QK_PAD), lambda bb, g, i: (bb, i, g)),
                  pl.BlockSpec((None, s, heads * MLA_QK_PAD), lambda bb, g, i: (bb, 0, g)),
                  pl.BlockSpec((None, s, heads * MLA_V), lambda bb, g, i: (bb, 0, g))],
        out_specs=pl.BlockSpec((None, tq, heads * MLA_V), lambda bb, g, i: (bb, i, g)),
        out_shape=jax.ShapeDtypeStruct((b, s, MLA_WIDTH), BF16),
        compiler_params=_cparams(3),
        name="mla_attention",
    )(q3, k3, v3)


def _mix_router_kernel(sb_ref, mla_ref, x_ref, lig_ref, lib_ref, sg_ref, mg_ref, wo_ref,
                       lmg_ref, lmb_ref, wr_ref, br_ref,
                       h2_ref, h2p_ref, sel_ref, gate_ref, cnt_ref, run_ref, *, tm):
    step = pl.program_id(0)

    @pl.when(step == 0)
    def _():
        run_ref[...] = jnp.zeros_like(run_ref)

    a = _rms_norm(sb_ref[...].astype(F32), sg_ref[...]).astype(BF16)
    bm = _rms_norm(mla_ref[...].astype(F32), mg_ref[...]).astype(BF16)
    mix = (jnp.dot(a, wo_ref[:SB_WIDTH, :], preferred_element_type=F32)
           + jnp.dot(bm, wo_ref[SB_WIDTH:, :], preferred_element_type=F32))
    h = _layer_norm(x_ref[...], lig_ref[...], lib_ref[...])
    h2 = _layer_norm(DEEPNORM_ALPHA * h + mix, lmg_ref[...], lmb_ref[...])
    h2_ref[...] = h2
    h2p_ref[...] = _pack_halves(h2[:, :HALF], h2[:, HALF:])

    h2_hi = h2.astype(BF16)
    h2_lo = (h2 - h2_hi.astype(F32)).astype(BF16)
    hw = jnp.dot(h2_hi, wr_ref[...], preferred_element_type=F32)
    logits = (hw[:, :LANES] + hw[:, LANES:]
              + jnp.dot(h2_lo, wr_ref[:, :LANES], preferred_element_type=F32) + br_ref[...])
    lane = lax.broadcasted_iota(jnp.int32, (tm, LANES), 1)
    lane_f = lane.astype(F32)
    logits = jnp.where(lane < N_EXPERTS, logits, -jnp.inf)
    vals, hots = [], []
    sel = jnp.zeros((tm, LANES), F32)
    for k in range(TOP_K):
        m = jnp.max(logits, -1, keepdims=True)
        idx = jnp.min(jnp.where(logits == m, lane_f, float(LANES)), -1, keepdims=True)
        hot = lane_f == idx
        logits = jnp.where(hot, -jnp.inf, logits)
        vals.append(m)
        hots.append(hot)
        sel = jnp.where(lane == k, idx, sel)
    exps = [jnp.exp(v - vals[0]) for v in vals]
    denom = exps[0] + exps[1] + exps[2] + exps[3]
    gates = jnp.zeros((tm, LANES), F32)
    for k in range(TOP_K):
        gates = jnp.where(lane == k, exps[k] / denom, gates)
    gate_ref[...] = gates

    onehot = (hots[0] | hots[1] | hots[2] | hots[3]).astype(F32)
    r_i = lax.broadcasted_iota(jnp.int32, (tm, tm), 0)
    c_i = lax.broadcasted_iota(jnp.int32, (tm, tm), 1)
    before = (c_i < r_i).astype(BF16)
    rank = jnp.dot(before, onehot.astype(BF16), preferred_element_type=F32) + run_ref[...]
    for k in range(TOP_K):
        pos = jnp.sum(jnp.where(hots[k], rank, 0.0), -1, keepdims=True)
        sel = jnp.where(lane == TOP_K + k, pos, sel)
    sel_ref[...] = sel
    run_ref[...] = run_ref[...] + jnp.sum(onehot, 0, keepdims=True)
    cnt_ref[...] = run_ref[...]


def _mix_router(sb_o, mla_o, x2, lig, lib, sg, mg, w_o, lmg, lmb, w_r, b_r, tm=256):
    t, d = x2.shape
    full = lambda a: pl.BlockSpec(a.shape, lambda i: (0,) * a.ndim)
    row = lambda w: pl.BlockSpec((tm, w), lambda i: (i, 0))
    return pl.pallas_call(
        functools.partial(_mix_router_kernel, tm=tm),
        grid=(t // tm,),
        in_specs=[row(SB_WIDTH), row(MLA_WIDTH), row(d), full(lig), full(lib), full(sg), full(mg),
                  full(w_o), full(lmg), full(lmb), full(w_r), full(b_r)],
        out_specs=[row(d), row(HALF), row(LANES), row(LANES),
                   pl.BlockSpec((1, LANES), lambda i: (0, 0))],
        out_shape=[jax.ShapeDtypeStruct((t, d), F32), jax.ShapeDtypeStruct((t, HALF), U32),
                   jax.ShapeDtypeStruct((t, LANES), F32), jax.ShapeDtypeStruct((t, LANES), F32),
                   jax.ShapeDtypeStruct((1, LANES), F32)],
        scratch_shapes=[pltpu.VMEM((1, LANES), F32)],
        compiler_params=_cparams(1),
        name="mix_router",
    )(sb_o, mla_o, x2, lig, lib, sg, mg, w_o, lmg, lmb, w_r, b_r)


ROWS_PER_STEP = 512


def _lagged_block_wait(out_hbm, sem, rows):
    step = pl.program_id(0)

    def wait_block():
        pltpu.make_async_copy(out_hbm.at[pl.ds(0, rows)], out_hbm.at[pl.ds(0, rows)], sem).wait()

    @pl.when(step > 0)
    def _():
        wait_block()

    @pl.when(step == pl.num_programs(0) - 1)
    def _():
        wait_block()


def _gather_rows_kernel(idx_ref, src_hbm, out_hbm, sem, *, rows):
    base = pl.program_id(0) * rows

    def issue(a, _):
        pltpu.make_async_copy(src_hbm.at[pl.ds(idx_ref[0, 0, a], 1)],
                              out_hbm.at[pl.ds(base + a, 1)], sem).start()
        return 0

    lax.fori_loop(0, rows, issue, 0, unroll=8)
    _lagged_block_wait(out_hbm, sem, rows)


def _scatter_rows_kernel(idx_ref, src_hbm, zero_hbm, out_hbm, sem, *, rows, copies, src_steps):
    step = pl.program_id(0)
    base = step * rows

    def issue_all(src_row_of):
        def issue(a, _):
            pltpu.make_async_copy(src_row_of(a), out_hbm.at[pl.ds(idx_ref[0, 0, a], 1)], sem).start()
            return 0
        lax.fori_loop(0, rows, issue, 0, unroll=8)

    shift = jnp.int32(copies.bit_length() - 1)

    @pl.when(step < src_steps)
    def _():
        issue_all(lambda a: src_hbm.at[pl.ds(lax.shift_right_logical(base + a, shift), 1)])

    @pl.when(step >= src_steps)
    def _():
        issue_all(lambda a: zero_hbm.at[pl.ds(0, 1)])

    _lagged_block_wait(out_hbm, sem, rows)


def _row_move_call(kern, idx, operands, out_rows, width, dtype, name):
    rows = ROWS_PER_STEP
    n = idx.shape[0]
    assert n % rows == 0 and out_rows >= rows
    return pl.pallas_call(
        functools.partial(kern, rows=rows),
        grid=(n // rows,),
        in_specs=[pl.BlockSpec((1, 1, rows), lambda i: (i, 0, 0), memory_space=pltpu.SMEM)]
                 + [pl.BlockSpec(memory_space=pl.ANY)] * len(operands),
        out_specs=pl.BlockSpec(memory_space=pl.ANY),
        out_shape=jax.ShapeDtypeStruct((out_rows, width), dtype),
        scratch_shapes=[pltpu.SemaphoreType.DMA(())],
        compiler_params=_cparams(1),
        name=name,
    )(idx.reshape(n // rows, 1, rows), *operands)


def _gather_rows(src, idx):
    return _row_move_call(_gather_rows_kernel, idx, [src], idx.shape[0], src.shape[1], src.dtype,
                          "gather_rows")


def _scatter_rows(src, copies, idx, out_rows):
    n_src = src.shape[0] * copies
    assert n_src % ROWS_PER_STEP == 0 and idx.shape[0] == out_rows
    kern = functools.partial(_scatter_rows_kernel, copies=copies, src_steps=n_src // ROWS_PER_STEP)
    zero = jnp.zeros((8, src.shape[1]), src.dtype)
    return _row_move_call(kern, idx, [src, zero], out_rows, src.shape[1], src.dtype, "scatter_rows")


MOE_ROWS = 1536
MOE_SUB = 256
MOE_TF = 256
MOE_NF = D_FF // MOE_TF
MOE_ITEMS = N_EXPERTS + (N_EXPERTS * (MOE_BLOCK - 1) + 8192 * TOP_K) // MOE_ROWS + 1


def _moe_kernel(ie_ref, is_ref, nb_ref, tail_ref, xs_hbm, wg_ref, wu_ref, bg_ref, bu_ref, wd_ref,
                bd_ref, ys_hbm, pkin_ref, pkout_ref, x_ref, acc_ref, wgb_ref, wub_ref, wdb_ref,
                pend_ref, sem_in, sem_out):
    i = pl.program_id(0)
    f = pl.program_id(1)
    nb = nb_ref[i]
    start = is_ref[i]
    n_out_blocks = ys_hbm.shape[0] // MOE_BLOCK

    def vm_block(ref, b):
        return ref.at[pl.ds(pl.multiple_of(b * MOE_BLOCK, MOE_BLOCK), MOE_BLOCK)]

    def hbm_block(ref, first_row, b):
        return ref.at[pl.ds(pl.multiple_of(first_row + b * MOE_BLOCK, MOE_BLOCK), MOE_BLOCK)]

    def in_copy(first_row, b):
        return pltpu.make_async_copy(hbm_block(xs_hbm, first_row, b), vm_block(pkin_ref, b), sem_in)

    def out_copy(first_row, b):
        return pltpu.make_async_copy(vm_block(pkout_ref, b), hbm_block(ys_hbm, first_row, b), sem_out)

    def for_blocks(n, fn):
        def body(b, _):
            fn(b)
            return 0
        lax.fori_loop(0, n, body, 0)

    @pl.when((i == 0) & (f == 0))
    def _():
        x_ref[...] = jnp.zeros_like(x_ref)
        acc_ref[...] = jnp.zeros_like(acc_ref)
        pend_ref[0] = 0
        pend_ref[1] = 0
        for_blocks(nb, lambda b: in_copy(start, b).start())

    @pl.when((f == 0) & (nb > 0))
    def _():
        def unpack(b):
            r0 = pl.multiple_of(b * MOE_BLOCK, MOE_BLOCK)
            lo, hi = _unpack_halves(pkin_ref[pl.ds(r0, MOE_BLOCK), :])
            x_ref[pl.ds(r0, MOE_BLOCK), :HALF] = lo.astype(BF16)
            x_ref[pl.ds(r0, MOE_BLOCK), HALF:] = hi.astype(BF16)

        for_blocks(nb, lambda b: in_copy(start, b).wait())
        for_blocks(nb, unpack)

    @pl.when((f == 1) & (i + 1 < MOE_ITEMS))
    def _():
        nxt = jnp.minimum(i + 1, MOE_ITEMS - 1)
        for_blocks(nb_ref[nxt], lambda b: in_copy(is_ref[nxt], b).start())

    @pl.when(nb > 0)
    def _():
        n_sub = (nb * MOE_BLOCK + MOE_SUB - 1) // MOE_SUB

        def gate_up(s, wg, wu):
            x = x_ref[pl.ds(pl.multiple_of(s * MOE_SUB, MOE_SUB), MOE_SUB), :]
            g = jnp.dot(x, wg, preferred_element_type=F32) + bg_ref[...]
            u = jnp.dot(x, wu, preferred_element_type=F32) + bu_ref[...]
            g = jnp.minimum(g, SWIGLU_LIMIT)
            u = jnp.clip(u, -SWIGLU_LIMIT, SWIGLU_LIMIT)
            return ((u + 1.0) * (g * jax.nn.sigmoid(SWIGLU_ALPHA * g))).astype(BF16)

        def down(s, act, wd):
            r0 = pl.multiple_of(s * MOE_SUB, MOE_SUB)
            y = jnp.dot(act, wd, preferred_element_type=F32)
            prev = jnp.where(f == 0, bd_ref[...], acc_ref[pl.ds(r0, MOE_SUB), :])
            acc_ref[pl.ds(r0, MOE_SUB), :] = prev + y

        wg = wg_ref[...].astype(BF16)
        wu = wu_ref[...].astype(BF16)
        wgb_ref[...] = wg
        wub_ref[...] = wu
        act0 = gate_up(0, wg, wu)
        wdb_ref[...] = wd_ref[...].astype(BF16)

        def step(s, act):
            down(s - 1, act, wdb_ref[...])
            return gate_up(s, wgb_ref[...], wub_ref[...])

        def two_steps(p, act):
            return step(2 * p + 2, step(2 * p + 1, act))

        act = lax.fori_loop(0, (n_sub - 1) // 2, two_steps, act0)
        act = lax.cond((n_sub - 1) % 2 == 1, lambda a: step(n_sub - 1, a), lambda a: a, act)
        down(n_sub - 1, act, wdb_ref[...])

    def drain_pending_store():
        for_blocks(pend_ref[1], lambda b: out_copy(pend_ref[0], b).wait())
        pend_ref[1] = 0

    @pl.when((f == MOE_NF - 1) & (nb > 0))
    def _():
        def pack(b):
            r0 = pl.multiple_of(b * MOE_BLOCK, MOE_BLOCK)
            y = acc_ref[pl.ds(r0, MOE_BLOCK), :]
            pkout_ref[pl.ds(r0, MOE_BLOCK), :] = _pack_halves(y[:, :HALF], y[:, HALF:])

        drain_pending_store()
        for_blocks(nb, pack)
        for_blocks(nb, lambda b: out_copy(start, b).start())
        pend_ref[0] = start
        pend_ref[1] = nb

    @pl.when((i == MOE_ITEMS - 1) & (f == MOE_NF - 1))
    def _():
        drain_pending_store()
        pkout_ref[:MOE_BLOCK, :] = jnp.zeros((MOE_BLOCK, HALF), U32)

        def tail_copy(b):
            return pltpu.make_async_copy(
                pkout_ref.at[pl.ds(0, MOE_BLOCK)],
                ys_hbm.at[pl.ds(pl.multiple_of(b * MOE_BLOCK, MOE_BLOCK), MOE_BLOCK)], sem_out)

        def tail_blocks(fn):
            def body(b, _):
                fn(b)
                return 0
            lax.fori_loop(tail_ref[0], n_out_blocks, body, 0)

        tail_blocks(lambda b: tail_copy(b).start())
        tail_blocks(lambda b: tail_copy(b).wait())


def _moe_ffn(item_e, item_start, item_nb, tail_blk, xs, w_gu, b_gu, w_d, b_d, n_rows):
    def wmap(off):
        def im(i, f, ie, is_, nb, tail):
            return ie[i], 0, off + jnp.where(nb[i] > 0, f, MOE_NF - 1)
        return im

    def dmap(i, f, ie, is_, nb, tail):
        return ie[i], jnp.where(nb[i] > 0, f, MOE_NF - 1), 0

    grid_spec = pltpu.PrefetchScalarGridSpec(
        num_scalar_prefetch=4,
        grid=(MOE_ITEMS, MOE_NF),
        in_specs=[pl.BlockSpec(memory_space=pl.ANY),
                  pl.BlockSpec((None, D_MODEL, MOE_TF), wmap(0)),
                  pl.BlockSpec((None, D_MODEL, MOE_TF), wmap(MOE_NF)),
                  pl.BlockSpec((None, 1, MOE_TF), wmap(0)),
                  pl.BlockSpec((None, 1, MOE_TF), wmap(MOE_NF)),
                  pl.BlockSpec((None, MOE_TF, D_MODEL), dmap),
                  pl.BlockSpec((None, 1, D_MODEL), lambda i, f, ie, is_, nb, tail: (ie[i], 0, 0))],
        out_specs=pl.BlockSpec(memory_space=pl.ANY),
        scratch_shapes=[pltpu.VMEM((MOE_ROWS, HALF), U32),
                        pltpu.VMEM((MOE_ROWS, HALF), U32),
                        pltpu.VMEM((MOE_ROWS, D_MODEL), BF16),
                        pltpu.VMEM((MOE_ROWS, D_MODEL), F32),
                        pltpu.VMEM((D_MODEL, MOE_TF), BF16),
                        pltpu.VMEM((D_MODEL, MOE_TF), BF16),
                        pltpu.VMEM((MOE_TF, D_MODEL), BF16),
                        pltpu.SMEM((2,), jnp.int32),
                        pltpu.SemaphoreType.DMA(()),
                        pltpu.SemaphoreType.DMA(())])
    return pl.pallas_call(
        _moe_kernel,
        grid_spec=grid_spec,
        out_shape=jax.ShapeDtypeStruct((n_rows, HALF), U32),
        compiler_params=_cparams(2),
        name="moe_ffn",
    )(item_e, item_start, item_nb, tail_blk, xs, w_gu, w_gu, b_gu, b_gu, w_d, b_d)


def _combine_kernel(y0_ref, y1_ref, y2_ref, y3_ref, gate_ref, h2_ref, g_ref, b_ref, o_ref):
    h2 = h2_ref[...]
    lo = DEEPNORM_ALPHA * h2[:, :HALF]
    hi = DEEPNORM_ALPHA * h2[:, HALF:]
    for k, y_ref in enumerate((y0_ref, y1_ref, y2_ref, y3_ref)):
        ylo, yhi = _unpack_halves(y_ref[...])
        gk = gate_ref[:, k:k + 1]
        lo = lo + gk * ylo
        hi = hi + gk * yhi
    mu = (jnp.sum(lo, -1, keepdims=True) + jnp.sum(hi, -1, keepdims=True)) * (1.0 / D_MODEL)
    lo = lo - mu
    hi = hi - mu
    var = (jnp.sum(lo * lo, -1, keepdims=True) + jnp.sum(hi * hi, -1, keepdims=True)) * (1.0 / D_MODEL)
    inv = lax.rsqrt(var + LN_EPS)
    o_ref[:, :HALF] = lo * inv * g_ref[:, :HALF] + b_ref[:, :HALF]
    o_ref[:, HALF:] = hi * inv * g_ref[:, HALF:] + b_ref[:, HALF:]


def _combine(yg, gates, h2, g, b, tm=256):
    t, d = h2.shape
    full = lambda a: pl.BlockSpec(a.shape, lambda i: (0,) * a.ndim)
    return pl.pallas_call(
        _combine_kernel,
        grid=(t // tm,),
        in_specs=[pl.BlockSpec((None, tm, HALF), functools.partial(lambda k, i: (k, i, 0), k))
                  for k in range(TOP_K)] + [
                  pl.BlockSpec((tm, LANES), lambda i: (i, 0)),
                  pl.BlockSpec((tm, d), lambda i: (i, 0)), full(g), full(b)],
        out_specs=pl.BlockSpec((tm, d), lambda i: (i, 0)),
        out_shape=jax.ShapeDtypeStruct((t, d), F32),
        compiler_params=_cparams(1),
        name="combine_ln",
    )(yg, yg, yg, yg, gates, h2, g, b)


def _routing_tables(sel, counts_f):
    t = sel.shape[0]
    idx = sel[:, :TOP_K].astype(jnp.int32)
    pos = sel[:, TOP_K:2 * TOP_K].astype(jnp.int32)
    counts = counts_f[0, :N_EXPERTS].astype(jnp.int32)
    padded = ((counts + MOE_BLOCK - 1) // MOE_BLOCK) * MOE_BLOCK
    pends = jnp.cumsum(padded)
    pstarts = pends - padded
    dest = (pstarts[idx] + pos).reshape(-1)
    n_rows = t * TOP_K + N_EXPERTS * MOE_BLOCK
    j = jnp.arange(MOE_BLOCK, dtype=jnp.int32)[None, :]
    is_pad = j < (padded - counts)[:, None]
    n_after = jnp.cumsum((~is_pad).reshape(-1).astype(jnp.int32)).reshape(is_pad.shape) - 1
    zero_rows = jnp.where(is_pad, (pstarts + counts)[:, None] + j, pends[-1] + n_after)
    row_of = jnp.concatenate([dest, zero_rows.reshape(-1)])

    nchunk = (padded + MOE_ROWS - 1) // MOE_ROWS
    cend = jnp.cumsum(nchunk)
    cstart = cend - nchunk
    total = cend[-1]
    it = jnp.arange(MOE_ITEMS, dtype=jnp.int32)
    valid = it < total
    it_c = jnp.minimum(it, total - 1)
    ie = jnp.searchsorted(cend, it_c, side="right").astype(jnp.int32)
    c = it_c - cstart[ie]
    istart = (pstarts[ie] + c * MOE_ROWS).astype(jnp.int32)
    inb = jnp.where(valid, jnp.minimum(MOE_ROWS, padded[ie] - c * MOE_ROWS) // MOE_BLOCK, 0)
    tail_blk = (pends[-1] // MOE_BLOCK).astype(jnp.int32).reshape(1)
    return dest, row_of, ie, istart, inb.astype(jnp.int32), tail_blk, n_rows


def kernel(x, positions, ln_in_g, ln_in_b, w_in, q_a_norm, w_q_b, kv_a_norm, w_kv_b, sb_out_norm,
           mla_out_norm, w_o, ln_mix_g, ln_mix_b, w_router, b_router, w_gate_up, b_gate_up, w_down,
           b_down, ln_ffn_g, ln_ffn_b):
    bsz, seq, d = x.shape
    t = bsz * seq
    x2 = x.reshape(t, d)
    row = lambda v: v.reshape(1, -1)

    w_in0 = w_in[0]
    w_sb = w_in0[:, :3 * SB_WIDTH].astype(BF16)
    w_lat = jnp.pad(w_in0[:, 3 * SB_WIDTH:], ((0, 0), (0, LANES - MLA_ROPE))).astype(BF16)
    wq = w_q_b[0].reshape(MLA_Q_RANK, MLA_HEADS, MLA_NOPE + MLA_ROPE)
    w_qn = wq[:, :, :MLA_NOPE].reshape(MLA_Q_RANK, -1).astype(BF16)
    w_qp = jnp.pad(wq[:, :, MLA_NOPE:], ((0, 0), (0, 0), (0, LANES - MLA_ROPE))
                   ).reshape(MLA_Q_RANK, -1).astype(BF16)
    wkv = w_kv_b[0].reshape(MLA_KV_RANK, MLA_HEADS, MLA_NOPE + MLA_V)
    w_kn = wkv[:, :, :MLA_NOPE].reshape(MLA_KV_RANK, -1).astype(BF16)
    w_v = wkv[:, :, MLA_NOPE:].reshape(MLA_KV_RANK, -1).astype(BF16)
    w_r32 = jnp.pad(w_router[0], ((0, 0), (0, LANES - N_EXPERTS)))
    w_r_hi = w_r32.astype(BF16)
    w_r = jnp.concatenate([w_r_hi, (w_r32 - w_r_hi.astype(F32)).astype(BF16)], axis=1)
    b_r = jnp.pad(b_router[0], (0, LANES - N_EXPERTS)).reshape(1, LANES)
    half = MLA_ROPE // 2
    inv_freq = ROPE_THETA ** (-(jnp.arange(half, dtype=F32) * 2.0 / MLA_ROPE))
    invf = jnp.concatenate([inv_freq, inv_freq, jnp.zeros((LANES - MLA_ROPE,), F32)]).reshape(1, LANES)
    pos_f = positions.reshape(t, 1).astype(F32)

    for l in range(DEPTH):
        u_sb, h_b = _ln_inproj(x2, row(ln_in_g), row(ln_in_b), w_sb)
        q_m, k_m, v_m = _mla_prep(h_b, pos_f, invf, w_lat, row(q_a_norm[l]), row(kv_a_norm[l]),
                                  w_qn, w_qp, w_kn, w_v)
        sb_o = _sb_attention(u_sb.reshape(bsz, seq, -1)).reshape(t, -1)
        mla_o = _mla_attention(q_m.reshape(bsz, seq, -1), k_m.reshape(bsz, seq, -1),
                               v_m.reshape(bsz, seq, -1)).reshape(t, -1)
        h2, h2p, sel, gates, counts_f = _mix_router(
            sb_o, mla_o, x2, row(ln_in_g), row(ln_in_b), row(sb_out_norm[l]), row(mla_out_norm[l]),
            w_o[l].astype(BF16), row(ln_mix_g[l]), row(ln_mix_b[l]), w_r, b_r)
        dest, row_of, ie, istart, inb, tail_blk, n_rows = _routing_tables(sel, counts_f)
        xs = _scatter_rows(h2p, TOP_K, row_of, n_rows)
        ys = _moe_ffn(ie, istart, inb, tail_blk, xs, w_gate_up[l], b_gate_up[l].reshape(N_EXPERTS, 1, -1),
                      w_down[l], b_down[l].reshape(N_EXPERTS, 1, -1), n_rows)
        yg = _gather_rows(ys, dest.reshape(t, TOP_K).T.reshape(-1)).reshape(TOP_K, t, HALF)
        out = _combine(yg, gates, h2, row(ln_ffn_g[l]), row(ln_ffn_b[l]))
    return out.reshape(bsz, seq, d)
```

```python
import functools

import jax
import jax.numpy as jnp
from jax import lax
from jax.experimental import pallas as pl
from jax.experimental.pallas import tpu as pltpu

D_MODEL = 2048
DEPTH = 1
CHUNK = 64
SB_HEADS = 8
HEAD_DIM = 128
SB_WIDTH = SB_HEADS * HEAD_DIM
MLA_HEADS = 8
MLA_Q_RANK = 512
MLA_KV_RANK = 256
MLA_NOPE = 128
MLA_ROPE = 64
MLA_V = 128
MLA_WIDTH = MLA_HEADS * MLA_V
MLA_QK_PAD = 256
ROPE_THETA = 10000.0
N_EXPERTS = 32
TOP_K = 4
D_FF = 2048
SWIGLU_LIMIT = 7.0
SWIGLU_ALPHA = 1.702
MOE_BLOCK = 128
LN_EPS = 1e-5
RMS_EPS = 1e-6
DEEPNORM_ALPHA = (2 * DEPTH) ** 0.25

LANES = 128
HALF = D_MODEL // 2
VMEM_LIMIT_BYTES = 56 * 1024 * 1024

BF16 = jnp.bfloat16
F32 = jnp.float32
U32 = jnp.uint32


def _cparams(n_axes, vmem=None, flags=None):
    return pltpu.CompilerParams(dimension_semantics=("arbitrary",) * n_axes,
                                vmem_limit_bytes=vmem or VMEM_LIMIT_BYTES, flags=flags)


def _layer_norm(x, g, b):
    mu = jnp.mean(x, -1, keepdims=True)
    xc = x - mu
    var = jnp.mean(xc * xc, -1, keepdims=True)
    return xc * lax.rsqrt(var + LN_EPS) * g + b


def _rms_norm(x, g):
    ms = jnp.mean(x * x, -1, keepdims=True)
    return x * lax.rsqrt(ms + RMS_EPS) * g


def _pack_halves(lo_f32, hi_f32):
    lo = lax.bitcast_convert_type(lo_f32.astype(BF16).astype(F32), U32)
    hi = lax.bitcast_convert_type(hi_f32.astype(BF16).astype(F32), U32)
    return lax.shift_right_logical(lo, U32(16)) | (hi & U32(0xFFFF0000))


def _unpack_halves(w):
    lo = lax.bitcast_convert_type(lax.shift_left(w, U32(16)), F32)
    hi = lax.bitcast_convert_type(w & U32(0xFFFF0000), F32)
    return lo, hi


def _ln_inproj_kernel(x_ref, g_ref, b_ref, w_ref, u_ref, h_ref):
    h = _layer_norm(x_ref[...], g_ref[...], b_ref[...]).astype(BF16)
    h_ref[...] = h
    u_ref[...] = jnp.dot(h, w_ref[...], preferred_element_type=F32).astype(BF16)


def _ln_inproj(x2, g, b, w_sb, tm=512):
    t, d = x2.shape
    n = w_sb.shape[1]
    return pl.pallas_call(
        _ln_inproj_kernel,
        grid=(t // tm,),
        in_specs=[pl.BlockSpec((tm, d), lambda i: (i, 0)),
                  pl.BlockSpec((1, d), lambda i: (0, 0)),
                  pl.BlockSpec((1, d), lambda i: (0, 0)),
                  pl.BlockSpec((d, n), lambda i: (0, 0))],
        out_specs=[pl.BlockSpec((tm, n), lambda i: (i, 0)),
                   pl.BlockSpec((tm, d), lambda i: (i, 0))],
        out_shape=[jax.ShapeDtypeStruct((t, n), BF16), jax.ShapeDtypeStruct((t, d), BF16)],
        compiler_params=_cparams(1),
        name="ln_inproj",
    )(x2, g, b, w_sb)


def _rope_slab(x, cos, sin_neg, sin_pos):
    return (x * cos + pltpu.roll(x, LANES - MLA_ROPE // 2, 1) * sin_neg
            + pltpu.roll(x, MLA_ROPE // 2, 1) * sin_pos)


def _mla_prep_kernel(h_ref, pos_ref, invf_ref, wl_ref, qg_ref, kg_ref, wqn_ref, wqp_ref,
                     wkn_ref, wv_ref, q_ref, k_ref, v_ref):
    lat = jnp.dot(h_ref[...], wl_ref[...], preferred_element_type=F32)
    qn = _rms_norm(lat[:, :MLA_Q_RANK], qg_ref[...]).astype(BF16)
    kvn = _rms_norm(lat[:, MLA_Q_RANK:MLA_Q_RANK + MLA_KV_RANK], kg_ref[...]).astype(BF16)
    k_rope = lat[:, MLA_Q_RANK + MLA_KV_RANK:]

    ang = pos_ref[...] * invf_ref[...]
    cos = jnp.cos(ang)
    sin = jnp.sin(ang)
    lane = lax.broadcasted_iota(jnp.int32, ang.shape, 1)
    sin_neg = jnp.where(lane < MLA_ROPE // 2, -sin, 0.0)
    sin_pos = jnp.where((lane >= MLA_ROPE // 2) & (lane < MLA_ROPE), sin, 0.0)

    q_nope = jnp.dot(qn, wqn_ref[...], preferred_element_type=F32)
    q_pe = jnp.dot(qn, wqp_ref[...], preferred_element_type=F32)
    k_nope = jnp.dot(kvn, wkn_ref[...], preferred_element_type=F32)
    v_ref[...] = jnp.dot(kvn, wv_ref[...], preferred_element_type=F32).astype(BF16)
    k_pe = _rope_slab(k_rope, cos, sin_neg, sin_pos).astype(BF16)
    for hd in range(MLA_HEADS):
        c0 = hd * MLA_QK_PAD
        s0 = hd * LANES
        q_ref[:, c0:c0 + LANES] = q_nope[:, s0:s0 + LANES].astype(BF16)
        q_ref[:, c0 + LANES:c0 + 2 * LANES] = _rope_slab(
            q_pe[:, s0:s0 + LANES], cos, sin_neg, sin_pos).astype(BF16)
        k_ref[:, c0:c0 + LANES] = k_nope[:, s0:s0 + LANES].astype(BF16)
        k_ref[:, c0 + LANES:c0 + 2 * LANES] = k_pe


def _mla_prep(h_b, pos_f, invf, w_lat, qg, kg, w_qn, w_qp, w_kn, w_v, tm=512):
    t, d = h_b.shape
    full = lambda a: pl.BlockSpec(a.shape, lambda i: (0,) * a.ndim)
    wq = MLA_HEADS * MLA_QK_PAD
    return pl.pallas_call(
        _mla_prep_kernel,
        grid=(t // tm,),
        in_specs=[pl.BlockSpec((tm, d), lambda i: (i, 0)),
                  pl.BlockSpec((tm, 1), lambda i: (i, 0)),
                  full(invf), full(w_lat), full(qg), full(kg), full(w_qn), full(w_qp),
                  full(w_kn), full(w_v)],
        out_specs=[pl.BlockSpec((tm, wq), lambda i: (i, 0)),
                   pl.BlockSpec((tm, wq), lambda i: (i, 0)),
                   pl.BlockSpec((tm, MLA_WIDTH), lambda i: (i, 0))],
        out_shape=[jax.ShapeDtypeStruct((t, wq), BF16), jax.ShapeDtypeStruct((t, wq), BF16),
                   jax.ShapeDtypeStruct((t, MLA_WIDTH), BF16)],
        compiler_params=_cparams(1),
        name="mla_prep",
    )(h_b, pos_f, invf, w_lat, qg, kg, w_qn, w_qp, w_kn, w_v)


def _sb_attn_kernel(q_ref, k_ref, v_ref, o_ref, *, tq, tk, heads, scale):
    i = pl.program_id(2)
    t_idx = i * tq + lax.broadcasted_iota(jnp.int32, (tq, 1), 0)
    r_i = lax.broadcasted_iota(jnp.int32, (tk, tk), 0)
    c_i = lax.broadcasted_iota(jnp.int32, (tk, tk), 1)
    suffix = (r_i >= c_i).astype(BF16)
    suffix2 = jnp.concatenate([suffix, suffix], axis=0)
    nkb = (i + 1) * (tq // tk)

    def block(j, hd, c, acc, diagonal):
        cols = slice(hd * HEAD_DIM, (hd + 1) * HEAD_DIM)
        k0 = pl.multiple_of(j * tk, tk)
        ks = k_ref[pl.ds(k0, tk), cols]
        vs = v_ref[pl.ds(k0, tk), cols]
        z = lax.dot_general(q_ref[:, cols], ks, (((1,), (1,)), ((), ())),
                            preferred_element_type=F32) * scale
        sp = jnp.maximum(z, 0.0) + jnp.log(1.0 + jnp.exp(-jnp.abs(z)))
        if diagonal:
            past = (k0 + lax.broadcasted_iota(jnp.int32, (1, tk), 1)) < t_idx
            sp = jnp.where(past, sp, 0.0)
        hi = sp.astype(BF16)
        lo = (sp - hi.astype(F32)).astype(BF16)
        incl = jnp.dot(jnp.concatenate([hi, lo], axis=1), suffix2, preferred_element_type=F32)
        w = jnp.exp(z - incl - c)
        if diagonal:
            w = jnp.where(past, w, 0.0)
        acc = acc + jnp.dot(w.astype(BF16), vs, preferred_element_type=F32)
        return c + incl[:, 0:1], acc

    bpt = tq // tk

    def trip(p, carry, diagonal):
        out = []
        for hd in range(heads):
            c, acc = carry[hd]
            for bk in range(bpt):
                c, acc = block(nkb - 1 - bpt * p - bk, hd, c, acc, diagonal)
            out.append((c, acc))
        return tuple(out)

    init = tuple((jnp.zeros((tq, 1), F32), jnp.zeros((tq, HEAD_DIM), F32)) for _ in range(heads))
    res = trip(0, init, True)
    res = lax.fori_loop(1, i + 1, functools.partial(trip, diagonal=False), res)
    for hd in range(heads):
        o_ref[:, hd * HEAD_DIM:(hd + 1) * HEAD_DIM] = res[hd][1].astype(o_ref.dtype)


def _sb_attention(u_sb3, tq=256, tk=128, heads=SB_HEADS):
    b, s, _ = u_sb3.shape
    assert tq % tk == 0 and SB_HEADS % heads == 0
    groups = SB_HEADS // heads
    w = heads * HEAD_DIM
    kern = functools.partial(_sb_attn_kernel, tq=tq, tk=tk, heads=heads, scale=HEAD_DIM ** -0.5)
    return pl.pallas_call(
        kern,
        grid=(b, groups, s // tq),
        in_specs=[pl.BlockSpec((None, tq, w), lambda bb, g, i: (bb, i, g)),
                  pl.BlockSpec((None, s, w), lambda bb, g, i: (bb, 0, groups + g)),
                  pl.BlockSpec((None, s, w), lambda bb, g, i: (bb, 0, 2 * groups + g))],
        out_specs=pl.BlockSpec((None, tq, w), lambda bb, g, i: (bb, i, g)),
        out_shape=jax.ShapeDtypeStruct((b, s, SB_WIDTH), BF16),
        compiler_params=_cparams(3),
        name="sb_attention",
    )(u_sb3, u_sb3, u_sb3)


def _mla_attn_kernel(q_ref, k_ref, v_ref, o_ref, *, tq, heads, scale):
    i = pl.program_id(2)
    t_chunk = lax.shift_right_logical(
        i * tq + lax.broadcasted_iota(jnp.int32, (tq, 1), 0), CHUNK.bit_length() - 1)

    def body(j, carry):
        k0 = pl.multiple_of(j * tq, tq)
        s_chunk = lax.shift_right_logical(
            k0 + lax.broadcasted_iota(jnp.int32, (1, tq), 1), CHUNK.bit_length() - 1)
        visible = s_chunk <= t_chunk
        out = []
        for hd in range(heads):
            m, l, acc = carry[hd]
            qk = slice(hd * MLA_QK_PAD, (hd + 1) * MLA_QK_PAD)
            ks = k_ref[pl.ds(k0, tq), qk]
            vs = v_ref[pl.ds(k0, tq), hd * MLA_V:(hd + 1) * MLA_V]
            s = lax.dot_general(q_ref[:, qk], ks, (((1,), (1,)), ((), ())),
                                preferred_element_type=F32) * scale
            s = jnp.where(visible, s, -1e30)
            m_new = jnp.maximum(m, jnp.max(s, -1, keepdims=True))
            alpha = jnp.exp(m - m_new)
            p = jnp.exp(s - m_new)
            l = alpha * l + jnp.sum(p, -1, keepdims=True)
            acc = alpha * acc + jnp.dot(p.astype(BF16), vs, preferred_element_type=F32)
            out.append((m_new, l, acc))
        return tuple(out)

    init = tuple((jnp.full((tq, 1), -1e30, F32), jnp.zeros((tq, 1), F32),
                  jnp.zeros((tq, MLA_V), F32)) for _ in range(heads))
    res = lax.fori_loop(0, i + 1, body, init)
    for hd in range(heads):
        _, l, acc = res[hd]
        o_ref[:, hd * MLA_V:(hd + 1) * MLA_V] = (acc / l).astype(o_ref.dtype)


def _mla_attention(q3, k3, v3, tq=256, heads=MLA_HEADS):
    b, s, _ = q3.shape
    assert MLA_HEADS % heads == 0
    kern = functools.partial(_mla_attn_kernel, tq=tq, heads=heads,
                             scale=(MLA_NOPE + MLA_ROPE) ** -0.5)
    return pl.pallas_call(
        kern,
        grid=(b, MLA_HEADS // heads, s // tq),
        in_specs=[pl.BlockSpec((None, tq, heads * MLA_QK_PAD), lambda bb, g, i: (bb, i, g)),
                  pl.BlockSpec((None, s, heads * MLA_QK_PAD), lambda bb, g, i: (bb, 0, g)),
                  pl.BlockSpec((None, s, heads * MLA_V), lambda bb, g, i: (bb, 0, g))],
        out_specs=pl.BlockSpec((None, tq, heads * MLA_V), lambda bb, g, i: (bb, i, g)),
        out_shape=jax.ShapeDtypeStruct((b, s, MLA_WIDTH), BF16),
        compiler_params=_cparams(3),
        name="mla_attention",
    )(q3, k3, v3)


def _mix_router_kernel(sb_ref, mla_ref, x_ref, lig_ref, lib_ref, sg_ref, mg_ref, wo_ref,
                       lmg_ref, lmb_ref, wr_ref, br_ref,
                       h2_ref, h2p_ref, sel_ref, gate_ref, cnt_ref, run_ref, *, tm):
    step = pl.program_id(0)

    @pl.when(step == 0)
    def _():
        run_ref[...] = jnp.zeros_like(run_ref)

    a = _rms_norm(sb_ref[...].astype(F32), sg_ref[...]).astype(BF16)
    bm = _rms_norm(mla_ref[...].astype(F32), mg_ref[...]).astype(BF16)
    mix = (jnp.dot(a, wo_ref[:SB_WIDTH, :], preferred_element_type=F32)
           + jnp.dot(bm, wo_ref[SB_WIDTH:, :], preferred_element_type=F32))
    h = _layer_norm(x_ref[...], lig_ref[...], lib_ref[...])
    h2 = _layer_norm(DEEPNORM_ALPHA * h + mix, lmg_ref[...], lmb_ref[...])
    h2_ref[...] = h2
    h2p_ref[...] = _pack_halves(h2[:, :HALF], h2[:, HALF:])

    h2_hi = h2.astype(BF16)
    h2_lo = (h2 - h2_hi.astype(F32)).astype(BF16)
    hw = jnp.dot(h2_hi, wr_ref[...], preferred_element_type=F32)
    logits = (hw[:, :LANES] + hw[:, LANES:]
              + jnp.dot(h2_lo, wr_ref[:, :LANES], preferred_element_type=F32) + br_ref[...])
    lane = lax.broadcasted_iota(jnp.int32, (tm, LANES), 1)
    lane_f = lane.astype(F32)
    logits = jnp.where(lane < N_EXPERTS, logits, -jnp.inf)
    vals, hots = [], []
    sel = jnp.zeros((tm, LANES), F32)
    for k in range(TOP_K):
        m = jnp.max(logits, -1, keepdims=True)
        idx = jnp.min(jnp.where(logits == m, lane_f, float(LANES)), -1, keepdims=True)
        hot = lane_f == idx
        logits = jnp.where(hot, -jnp.inf, logits)
        vals.append(m)
        hots.append(hot)
        sel = jnp.where(lane == k, idx, sel)
    exps = [jnp.exp(v - vals[0]) for v in vals]
    denom = exps[0] + exps[1] + exps[2] + exps[3]
    gates = jnp.zeros((tm, LANES), F32)
    for k in range(TOP_K):
        gates = jnp.where(lane == k, exps[k] / denom, gates)
    gate_ref[...] = gates

    onehot = (hots[0] | hots[1] | hots[2] | hots[3]).astype(F32)
    r_i = lax.broadcasted_iota(jnp.int32, (tm, tm), 0)
    c_i = lax.broadcasted_iota(jnp.int32, (tm, tm), 1)
    before = (c_i < r_i).astype(BF16)
    rank = jnp.dot(before, onehot.astype(BF16), preferred_element_type=F32) + run_ref[...]
    for k in range(TOP_K):
        pos = jnp.sum(jnp.where(hots[k], rank, 0.0), -1, keepdims=True)
        sel = jnp.where(lane == TOP_K + k, pos, sel)
    sel_ref[...] = sel
    run_ref[...] = run_ref[...] + jnp.sum(onehot, 0, keepdims=True)
    cnt_ref[...] = run_ref[...]


def _mix_router(sb_o, mla_o, x2, lig, lib, sg, mg, w_o, lmg, lmb, w_r, b_r, tm=256):
    t, d = x2.shape
    full = lambda a: pl.BlockSpec(a.shape, lambda i: (0,) * a.ndim)
    row = lambda w: pl.BlockSpec((tm, w), lambda i: (i, 0))
    return pl.pallas_call(
        functools.partial(_mix_router_kernel, tm=tm),
        grid=(t // tm,),
        in_specs=[row(SB_WIDTH), row(MLA_WIDTH), row(d), full(lig), full(lib), full(sg), full(mg),
                  full(w_o), full(lmg), full(lmb), full(w_r), full(b_r)],
        out_specs=[row(d), row(HALF), row(LANES), row(LANES),
                   pl.BlockSpec((1, LANES), lambda i: (0, 0))],
        out_shape=[jax.ShapeDtypeStruct((t, d), F32), jax.ShapeDtypeStruct((t, HALF), U32),
                   jax.ShapeDtypeStruct((t, LANES), F32), jax.ShapeDtypeStruct((t, LANES), F32),
                   jax.ShapeDtypeStruct((1, LANES), F32)],
        scratch_shapes=[pltpu.VMEM((1, LANES), F32)],
        compiler_params=_cparams(1),
        name="mix_router",
    )(sb_o, mla_o, x2, lig, lib, sg, mg, w_o, lmg, lmb, w_r, b_r)


ROWS_PER_STEP = 2048


def _wait_rows(like_hbm, sem, rows):
    pltpu.make_async_copy(like_hbm.at[pl.ds(0, rows)], like_hbm.at[pl.ds(0, rows)], sem).wait()


def _gather_rows_kernel(idx_ref, src_hbm, out_ref, sem, *, rows):
    def issue(g, _):
        r0 = pl.multiple_of(g * 8, 8)
        for k in range(8):
            pltpu.make_async_copy(src_hbm.at[pl.ds(idx_ref[0, 0, r0 + k], 1)],
                                  out_ref.at[pl.ds(r0 + k, 1)], sem).start()
        return 0

    lax.fori_loop(0, rows // 8, issue, 0)
    _wait_rows(src_hbm, sem, rows)


def _scatter_rows_kernel(idx_ref, src_ref, out_hbm, zero_ref, sem, *, rows, copies, src_steps):
    step = pl.program_id(0)

    @pl.when(step == 0)
    def _():
        zero_ref[...] = jnp.zeros_like(zero_ref)

    @pl.when(step < src_steps)
    def _():
        def issue(g, _):
            r0 = pl.multiple_of(g * 8, 8)
            for k in range(8):
                pltpu.make_async_copy(src_ref.at[pl.ds(g * (8 // copies) + k // copies, 1)],
                                      out_hbm.at[pl.ds(idx_ref[0, 0, r0 + k], 1)], sem).start()
            return 0
        lax.fori_loop(0, rows // 8, issue, 0)

    @pl.when(step >= src_steps)
    def _():
        def issue(a, _):
            pltpu.make_async_copy(zero_ref.at[pl.ds(0, 1)],
                                  out_hbm.at[pl.ds(idx_ref[0, 0, a], 1)], sem).start()
            return 0
        lax.fori_loop(0, rows, issue, 0, unroll=8)

    _wait_rows(out_hbm, sem, rows)


def _gather_rows(src, idx):
    rows = ROWS_PER_STEP
    n, w = idx.shape[0], src.shape[1]
    assert n % rows == 0 and src.shape[0] >= rows
    return pl.pallas_call(
        functools.partial(_gather_rows_kernel, rows=rows),
        grid=(n // rows,),
        in_specs=[pl.BlockSpec((1, 1, rows), lambda i: (i, 0, 0), memory_space=pltpu.SMEM),
                  pl.BlockSpec(memory_space=pl.ANY)],
        out_specs=pl.BlockSpec((rows, w), lambda i: (i, 0)),
        out_shape=jax.ShapeDtypeStruct((n, w), src.dtype),
        scratch_shapes=[pltpu.SemaphoreType.DMA(())],
        compiler_params=_cparams(1),
        name="gather_rows",
    )(idx.reshape(n // rows, 1, rows), src)


def _scatter_rows(src, copies, idx, out_rows):
    rows = ROWS_PER_STEP
    n, w = idx.shape[0], src.shape[1]
    n_src = src.shape[0] * copies
    assert n_src % rows == 0 and n % rows == 0 and n == out_rows and 8 % copies == 0
    src_steps = n_src // rows
    kern = functools.partial(_scatter_rows_kernel, rows=rows, copies=copies, src_steps=src_steps)
    return pl.pallas_call(
        kern,
        grid=(n // rows,),
        in_specs=[pl.BlockSpec((1, 1, rows), lambda i: (i, 0, 0), memory_space=pltpu.SMEM),
                  pl.BlockSpec((rows // copies, w), lambda i: (jnp.minimum(i, src_steps - 1), 0))],
        out_specs=pl.BlockSpec(memory_space=pl.ANY),
        out_shape=jax.ShapeDtypeStruct((out_rows, w), src.dtype),
        scratch_shapes=[pltpu.VMEM((8, w), src.dtype), pltpu.SemaphoreType.DMA(())],
        compiler_params=_cparams(1),
        name="scatter_rows",
    )(idx.reshape(n // rows, 1, rows), src)


MOE_ROWS = 1536
MOE_SUB = 256
MOE_TF = 256
MOE_NF = D_FF // MOE_TF
MOE_ITEMS = N_EXPERTS + (N_EXPERTS * (MOE_BLOCK - 1) + 8192 * TOP_K) // MOE_ROWS + 1


def _moe_kernel(ie_ref, is_ref, nb_ref, tail_ref, xs_hbm, wg_ref, wu_ref, bg_ref, bu_ref, wd_ref,
                bd_ref, ys_hbm, pkin_ref, pkout_ref, x_ref, acc_ref, wgb_ref, wub_ref, wdb_ref,
                pend_ref, sem_in, sem_out):
    i = pl.program_id(0)
    f = pl.program_id(1)
    nb = nb_ref[i]
    start = is_ref[i]
    n_out_blocks = ys_hbm.shape[0] // MOE_BLOCK

    def vm_block(ref, b):
        return ref.at[pl.ds(pl.multiple_of(b * MOE_BLOCK, MOE_BLOCK), MOE_BLOCK)]

    def hbm_block(ref, first_row, b):
        return ref.at[pl.ds(pl.multiple_of(first_row + b * MOE_BLOCK, MOE_BLOCK), MOE_BLOCK)]

    def in_copy(first_row, b):
        return pltpu.make_async_copy(hbm_block(xs_hbm, first_row, b), vm_block(pkin_ref, b), sem_in)

    def out_copy(first_row, b):
        return pltpu.make_async_copy(vm_block(pkout_ref, b), hbm_block(ys_hbm, first_row, b), sem_out)

    def for_blocks(n, fn):
        def body(b, _):
            fn(b)
            return 0
        lax.fori_loop(0, n, body, 0)

    @pl.when((i == 0) & (f == 0))
    def _():
        x_ref[...] = jnp.zeros_like(x_ref)
        acc_ref[...] = jnp.zeros_like(acc_ref)
        pend_ref[0] = 0
        pend_ref[1] = 0
        for_blocks(nb, lambda b: in_copy(start, b).start())

    @pl.when((f == 0) & (nb > 0))
    def _():
        def unpack(b):
            r0 = pl.multiple_of(b * MOE_BLOCK, MOE_BLOCK)
            lo, hi = _unpack_halves(pkin_ref[pl.ds(r0, MOE_BLOCK), :])
            x_ref[pl.ds(r0, MOE_BLOCK), :HALF] = lo.astype(BF16)
            x_ref[pl.ds(r0, MOE_BLOCK), HALF:] = hi.astype(BF16)

        for_blocks(nb, lambda b: in_copy(start, b).wait())
        for_blocks(nb, unpack)

    @pl.when((f == 1) & (i + 1 < MOE_ITEMS))
    def _():
        nxt = jnp.minimum(i + 1, MOE_ITEMS - 1)
        for_blocks(nb_ref[nxt], lambda b: in_copy(is_ref[nxt], b).start())

    @pl.when(nb > 0)
    def _():
        n_sub = (nb * MOE_BLOCK + MOE_SUB - 1) // MOE_SUB

        def gate_up(s, wg, wu):
            x = x_ref[pl.ds(pl.multiple_of(s * MOE_SUB, MOE_SUB), MOE_SUB), :]
            g = jnp.dot(x, wg, preferred_element_type=F32) + bg_ref[...]
            u = jnp.dot(x, wu, preferred_element_type=F32) + bu_ref[...]
            g = jnp.minimum(g, SWIGLU_LIMIT)
            u = jnp.clip(u, -SWIGLU_LIMIT, SWIGLU_LIMIT)
            return ((u + 1.0) * (g * jax.nn.sigmoid(SWIGLU_ALPHA * g))).astype(BF16)

        def down(s, act, wd):
            r0 = pl.multiple_of(s * MOE_SUB, MOE_SUB)
            y = jnp.dot(act, wd, preferred_element_type=F32)
            prev = jnp.where(f == 0, bd_ref[...], acc_ref[pl.ds(r0, MOE_SUB), :])
            acc_ref[pl.ds(r0, MOE_SUB), :] = prev + y

        wg = wg_ref[...].astype(BF16)
        wu = wu_ref[...].astype(BF16)
        wgb_ref[...] = wg
        wub_ref[...] = wu
        act0 = gate_up(0, wg, wu)
        wdb_ref[...] = wd_ref[...].astype(BF16)

        def step(s, act):
            down(s - 1, act, wdb_ref[...])
            return gate_up(s, wgb_ref[...], wub_ref[...])

        def two_steps(p, act):
            return step(2 * p + 2, step(2 * p + 1, act))

        act = lax.fori_loop(0, (n_sub - 1) // 2, two_steps, act0)
        act = lax.cond((n_sub - 1) % 2 == 1, lambda a: step(n_sub - 1, a), lambda a: a, act)
        down(n_sub - 1, act, wdb_ref[...])

    def drain_pending_store():
        for_blocks(pend_ref[1], lambda b: out_copy(pend_ref[0], b).wait())
        pend_ref[1] = 0

    @pl.when((f == MOE_NF - 1) & (nb > 0))
    def _():
        def pack(b):
            r0 = pl.multiple_of(b * MOE_BLOCK, MOE_BLOCK)
            y = acc_ref[pl.ds(r0, MOE_BLOCK), :]
            pkout_ref[pl.ds(r0, MOE_BLOCK), :] = _pack_halves(y[:, :HALF], y[:, HALF:])

        drain_pending_store()
        for_blocks(nb, pack)
        for_blocks(nb, lambda b: out_copy(start, b).start())
        pend_ref[0] = start
        pend_ref[1] = nb

    @pl.when((i == MOE_ITEMS - 1) & (f == MOE_NF - 1))
    def _():
        drain_pending_store()
        pkout_ref[:MOE_BLOCK, :] = jnp.zeros((MOE_BLOCK, HALF), U32)

        def tail_copy(b):
            return pltpu.make_async_copy(
                pkout_ref.at[pl.ds(0, MOE_BLOCK)],
                ys_hbm.at[pl.ds(pl.multiple_of(b * MOE_BLOCK, MOE_BLOCK), MOE_BLOCK)], sem_out)

        def tail_blocks(fn):
            def body(b, _):
                fn(b)
                return 0
            lax.fori_loop(tail_ref[0], n_out_blocks, body, 0)

        tail_blocks(lambda b: tail_copy(b).start())
        tail_blocks(lambda b: tail_copy(b).wait())


def _moe_ffn(item_e, item_start, item_nb, tail_blk, xs, w_gu, b_gu, w_d, b_d, n_rows):
    def wmap(off):
        def im(i, f, ie, is_, nb, tail):
            return ie[i], 0, off + jnp.where(nb[i] > 0, f, MOE_NF - 1)
        return im

    def dmap(i, f, ie, is_, nb, tail):
        return ie[i], jnp.where(nb[i] > 0, f, MOE_NF - 1), 0

    grid_spec = pltpu.PrefetchScalarGridSpec(
        num_scalar_prefetch=4,
        grid=(MOE_ITEMS, MOE_NF),
        in_specs=[pl.BlockSpec(memory_space=pl.ANY),
                  pl.BlockSpec((None, D_MODEL, MOE_TF), wmap(0)),
                  pl.BlockSpec((None, D_MODEL, MOE_TF), wmap(MOE_NF)),
                  pl.BlockSpec((None, 1, MOE_TF), wmap(0)),
                  pl.BlockSpec((None, 1, MOE_TF), wmap(MOE_NF)),
                  pl.BlockSpec((None, MOE_TF, D_MODEL), dmap),
                  pl.BlockSpec((None, 1, D_MODEL), lambda i, f, ie, is_, nb, tail: (ie[i], 0, 0))],
        out_specs=pl.BlockSpec(memory_space=pl.ANY),
        scratch_shapes=[pltpu.VMEM((MOE_ROWS, HALF), U32),
                        pltpu.VMEM((MOE_ROWS, HALF), U32),
                        pltpu.VMEM((MOE_ROWS, D_MODEL), BF16),
                        pltpu.VMEM((MOE_ROWS, D_MODEL), F32),
                        pltpu.VMEM((D_MODEL, MOE_TF), BF16),
                        pltpu.VMEM((D_MODEL, MOE_TF), BF16),
                        pltpu.VMEM((MOE_TF, D_MODEL), BF16),
                        pltpu.SMEM((2,), jnp.int32),
                        pltpu.SemaphoreType.DMA(()),
                        pltpu.SemaphoreType.DMA(())])
    return pl.pallas_call(
        _moe_kernel,
        grid_spec=grid_spec,
        out_shape=jax.ShapeDtypeStruct((n_rows, HALF), U32),
        compiler_params=_cparams(2),
        name="moe_ffn",
    )(item_e, item_start, item_nb, tail_blk, xs, w_gu, w_gu, b_gu, b_gu, w_d, b_d)


def _combine_kernel(y0_ref, y1_ref, y2_ref, y3_ref, gate_ref, h2_ref, g_ref, b_ref, o_ref):
    h2 = h2_ref[...]
    lo = DEEPNORM_ALPHA * h2[:, :HALF]
    hi = DEEPNORM_ALPHA * h2[:, HALF:]
    for k, y_ref in enumerate((y0_ref, y1_ref, y2_ref, y3_ref)):
        ylo, yhi = _unpack_halves(y_ref[...])
        gk = gate_ref[:, k:k + 1]
        lo = lo + gk * ylo
        hi = hi + gk * yhi
    mu = (jnp.sum(lo, -1, keepdims=True) + jnp.sum(hi, -1, keepdims=True)) * (1.0 / D_MODEL)
    lo = lo - mu
    hi = hi - mu
    var = (jnp.sum(lo * lo, -1, keepdims=True) + jnp.sum(hi * hi, -1, keepdims=True)) * (1.0 / D_MODEL)
    inv = lax.rsqrt(var + LN_EPS)
    o_ref[:, :HALF] = lo * inv * g_ref[:, :HALF] + b_ref[:, :HALF]
    o_ref[:, HALF:] = hi * inv * g_ref[:, HALF:] + b_ref[:, HALF:]


def _combine(yg, gates, h2, g, b, tm=256):
    t, d = h2.shape
    full = lambda a: pl.BlockSpec(a.shape, lambda i: (0,) * a.ndim)
    return pl.pallas_call(
        _combine_kernel,
        grid=(t // tm,),
        in_specs=[pl.BlockSpec((None, tm, HALF), functools.partial(lambda k, i: (k, i, 0), k))
                  for k in range(TOP_K)] + [
                  pl.BlockSpec((tm, LANES), lambda i: (i, 0)),
                  pl.BlockSpec((tm, d), lambda i: (i, 0)), full(g), full(b)],
        out_specs=pl.BlockSpec((tm, d), lambda i: (i, 0)),
        out_shape=jax.ShapeDtypeStruct((t, d), F32),
        compiler_params=_cparams(1),
        name="combine_ln",
    )(yg, yg, yg, yg, gates, h2, g, b)


def _routing_tables(sel, counts_f):
    t = sel.shape[0]
    idx = sel[:, :TOP_K].astype(jnp.int32)
    pos = sel[:, TOP_K:2 * TOP_K].astype(jnp.int32)
    counts = counts_f[0, :N_EXPERTS].astype(jnp.int32)
    padded = ((counts + MOE_BLOCK - 1) // MOE_BLOCK) * MOE_BLOCK
    pends = jnp.cumsum(padded)
    pstarts = pends - padded
    dest = (pstarts[idx] + pos).reshape(-1)
    n_rows = t * TOP_K + N_EXPERTS * MOE_BLOCK
    j = jnp.arange(MOE_BLOCK, dtype=jnp.int32)[None, :]
    is_pad = j < (padded - counts)[:, None]
    n_after = jnp.cumsum((~is_pad).reshape(-1).astype(jnp.int32)).reshape(is_pad.shape) - 1
    zero_rows = jnp.where(is_pad, (pstarts + counts)[:, None] + j, pends[-1] + n_after)
    row_of = jnp.concatenate([dest, zero_rows.reshape(-1)])

    nchunk = (padded + MOE_ROWS - 1) // MOE_ROWS
    cend = jnp.cumsum(nchunk)
    cstart = cend - nchunk
    total = cend[-1]
    it = jnp.arange(MOE_ITEMS, dtype=jnp.int32)
    valid = it < total
    it_c = jnp.minimum(it, total - 1)
    ie = jnp.searchsorted(cend, it_c, side="right").astype(jnp.int32)
    c = it_c - cstart[ie]
    istart = (pstarts[ie] + c * MOE_ROWS).astype(jnp.int32)
    inb = jnp.where(valid, jnp.minimum(MOE_ROWS, padded[ie] - c * MOE_ROWS) // MOE_BLOCK, 0)
    tail_blk = (pends[-1] // MOE_BLOCK).astype(jnp.int32).reshape(1)
    return dest, row_of, ie, istart, inb.astype(jnp.int32), tail_blk, n_rows


def kernel(x, positions, ln_in_g, ln_in_b, w_in, q_a_norm, w_q_b, kv_a_norm, w_kv_b, sb_out_norm,
           mla_out_norm, w_o, ln_mix_g, ln_mix_b, w_router, b_router, w_gate_up, b_gate_up, w_down,
           b_down, ln_ffn_g, ln_ffn_b):
    bsz, seq, d = x.shape
    t = bsz * seq
    x2 = x.reshape(t, d)
    row = lambda v: v.reshape(1, -1)

    w_in0 = w_in[0]
    w_sb = w_in0[:, :3 * SB_WIDTH].astype(BF16)
    w_lat = jnp.pad(w_in0[:, 3 * SB_WIDTH:], ((0, 0), (0, LANES - MLA_ROPE))).astype(BF16)
    wq = w_q_b[0].reshape(MLA_Q_RANK, MLA_HEADS, MLA_NOPE + MLA_ROPE)
    w_qn = wq[:, :, :MLA_NOPE].reshape(MLA_Q_RANK, -1).astype(BF16)
    w_qp = jnp.pad(wq[:, :, MLA_NOPE:], ((0, 0), (0, 0), (0, LANES - MLA_ROPE))
                   ).reshape(MLA_Q_RANK, -1).astype(BF16)
    wkv = w_kv_b[0].reshape(MLA_KV_RANK, MLA_HEADS, MLA_NOPE + MLA_V)
    w_kn = wkv[:, :, :MLA_NOPE].reshape(MLA_KV_RANK, -1).astype(BF16)
    w_v = wkv[:, :, MLA_NOPE:].reshape(MLA_KV_RANK, -1).astype(BF16)
    w_r32 = jnp.pad(w_router[0], ((0, 0), (0, LANES - N_EXPERTS)))
    w_r_hi = w_r32.astype(BF16)
    w_r = jnp.concatenate([w_r_hi, (w_r32 - w_r_hi.astype(F32)).astype(BF16)], axis=1)
    b_r = jnp.pad(b_router[0], (0, LANES - N_EXPERTS)).reshape(1, LANES)
    half = MLA_ROPE // 2
    inv_freq = ROPE_THETA ** (-(jnp.arange(half, dtype=F32) * 2.0 / MLA_ROPE))
    invf = jnp.concatenate([inv_freq, inv_freq, jnp.zeros((LANES - MLA_ROPE,), F32)]).reshape(1, LANES)
    pos_f = positions.reshape(t, 1).astype(F32)

    for l in range(DEPTH):
        u_sb, h_b = _ln_inproj(x2, row(ln_in_g), row(ln_in_b), w_sb)
        q_m, k_m, v_m = _mla_prep(h_b, pos_f, invf, w_lat, row(q_a_norm[l]), row(kv_a_norm[l]),
                                  w_qn, w_qp, w_kn, w_v)
        sb_o = _sb_attention(u_sb.reshape(bsz, seq, -1)).reshape(t, -1)
        mla_o = _mla_attention(q_m.reshape(bsz, seq, -1), k_m.reshape(bsz, seq, -1),
                               v_m.reshape(bsz, seq, -1)).reshape(t, -1)
        h2, h2p, sel, gates, counts_f = _mix_router(
            sb_o, mla_o, x2, row(ln_in_g), row(ln_in_b), row(sb_out_norm[l]), row(mla_out_norm[l]),
            w_o[l].astype(BF16), row(ln_mix_g[l]), row(ln_mix_b[l]), w_r, b_r)
        dest, row_of, ie, istart, inb, tail_blk, n_rows = _routing_tables(sel, counts_f)
        xs = _scatter_rows(h2p, TOP_K, row_of, n_rows)
        ys = _moe_ffn(ie, istart, inb, tail_blk, xs, w_gate_up[l], b_gate_up[l].reshape(N_EXPERTS, 1, -1),
                      w_down[l], b_down[l].reshape(N_EXPERTS, 1, -1), n_rows)
        yg = _gather_rows(ys, dest.reshape(t, TOP_K).T.reshape(-1)).reshape(TOP_K, t, HALF)
        out = _combine(yg, gates, h2, row(ln_ffn_g[l]), row(ln_ffn_b[l]))
    return out.reshape(bsz, seq, d)
```

```python
import functools

import jax
import jax.numpy as jnp
from jax import lax
from jax.experimental import pallas as pl
from jax.experimental.pallas import tpu as pltpu

D_MODEL = 2048
DEPTH = 1
CHUNK = 64
SB_HEADS = 8
HEAD_DIM = 128
SB_WIDTH = SB_HEADS * HEAD_DIM
MLA_HEADS = 8
MLA_Q_RANK = 512
MLA_KV_RANK = 256
MLA_NOPE = 128
MLA_ROPE = 64
MLA_V = 128
MLA_WIDTH = MLA_HEADS * MLA_V
MLA_QK_PAD = 256
ROPE_THETA = 10000.0
N_EXPERTS = 32
TOP_K = 4
D_FF = 2048
SWIGLU_LIMIT = 7.0
SWIGLU_ALPHA = 1.702
MOE_BLOCK = 128
LN_EPS = 1e-5
RMS_EPS = 1e-6
DEEPNORM_ALPHA = (2 * DEPTH) ** 0.25

LANES = 128
HALF = D_MODEL // 2
VMEM_LIMIT_BYTES = 56 * 1024 * 1024

BF16 = jnp.bfloat16
F32 = jnp.float32
U32 = jnp.uint32


def _cparams(n_axes, vmem=None, flags=None):
    return pltpu.CompilerParams(dimension_semantics=("arbitrary",) * n_axes,
                                vmem_limit_bytes=vmem or VMEM_LIMIT_BYTES, flags=flags)


def _layer_norm(x, g, b):
    mu = jnp.mean(x, -1, keepdims=True)
    xc = x - mu
    var = jnp.mean(xc * xc, -1, keepdims=True)
    return xc * lax.rsqrt(var + LN_EPS) * g + b


def _rms_norm(x, g):
    ms = jnp.mean(x * x, -1, keepdims=True)
    return x * lax.rsqrt(ms + RMS_EPS) * g


def _pack_halves(lo_f32, hi_f32):
    lo = lax.bitcast_convert_type(lo_f32.astype(BF16).astype(F32), U32)
    hi = lax.bitcast_convert_type(hi_f32.astype(BF16).astype(F32), U32)
    return lax.shift_right_logical(lo, U32(16)) | (hi & U32(0xFFFF0000))


def _unpack_halves(w):
    lo = lax.bitcast_convert_type(lax.shift_left(w, U32(16)), F32)
    hi = lax.bitcast_convert_type(w & U32(0xFFFF0000), F32)
    return lo, hi


def _ln_inproj_kernel(x_ref, g_ref, b_ref, w_ref, u_ref, h_ref):
    h = _layer_norm(x_ref[...], g_ref[...], b_ref[...]).astype(BF16)
    h_ref[...] = h
    u_ref[...] = jnp.dot(h, w_ref[...], preferred_element_type=F32).astype(BF16)


def _ln_inproj(x2, g, b, w_sb, tm=512):
    t, d = x2.shape
    n = w_sb.shape[1]
    return pl.pallas_call(
        _ln_inproj_kernel,
        grid=(t // tm,),
        in_specs=[pl.BlockSpec((tm, d), lambda i: (i, 0)),
                  pl.BlockSpec((1, d), lambda i: (0, 0)),
                  pl.BlockSpec((1, d), lambda i: (0, 0)),
                  pl.BlockSpec((d, n), lambda i: (0, 0))],
        out_specs=[pl.BlockSpec((tm, n), lambda i: (i, 0)),
                   pl.BlockSpec((tm, d), lambda i: (i, 0))],
        out_shape=[jax.ShapeDtypeStruct((t, n), BF16), jax.ShapeDtypeStruct((t, d), BF16)],
        compiler_params=_cparams(1),
        name="ln_inproj",
    )(x2, g, b, w_sb)


def _rope_slab(x, cos, sin_neg, sin_pos):
    return (x * cos + pltpu.roll(x, LANES - MLA_ROPE // 2, 1) * sin_neg
            + pltpu.roll(x, MLA_ROPE // 2, 1) * sin_pos)


def _mla_prep_kernel(h_ref, pos_ref, invf_ref, wl_ref, qg_ref, kg_ref, wqn_ref, wqp_ref,
                     wkn_ref, wv_ref, q_ref, k_ref, v_ref):
    lat = jnp.dot(h_ref[...], wl_ref[...], preferred_element_type=F32)
    qn = _rms_norm(lat[:, :MLA_Q_RANK], qg_ref[...]).astype(BF16)
    kvn = _rms_norm(lat[:, MLA_Q_RANK:MLA_Q_RANK + MLA_KV_RANK], kg_ref[...]).astype(BF16)
    k_rope = lat[:, MLA_Q_RANK + MLA_KV_RANK:]

    ang = pos_ref[...] * invf_ref[...]
    cos = jnp.cos(ang)
    sin = jnp.sin(ang)
    lane = lax.broadcasted_iota(jnp.int32, ang.shape, 1)
    sin_neg = jnp.where(lane < MLA_ROPE // 2, -sin, 0.0)
    sin_pos = jnp.where((lane >= MLA_ROPE // 2) & (lane < MLA_ROPE), sin, 0.0)

    q_nope = jnp.dot(qn, wqn_ref[...], preferred_element_type=F32)
    q_pe = jnp.dot(qn, wqp_ref[...], preferred_element_type=F32)
    k_nope = jnp.dot(kvn, wkn_ref[...], preferred_element_type=F32)
    v_ref[...] = jnp.dot(kvn, wv_ref[...], preferred_element_type=F32).astype(BF16)
    k_pe = _rope_slab(k_rope, cos, sin_neg, sin_pos).astype(BF16)
    for hd in range(MLA_HEADS):
        c0 = hd * MLA_QK_PAD
        s0 = hd * LANES
        q_ref[:, c0:c0 + LANES] = q_nope[:, s0:s0 + LANES].astype(BF16)
        q_ref[:, c0 + LANES:c0 + 2 * LANES] = _rope_slab(
            q_pe[:, s0:s0 + LANES], cos, sin_neg, sin_pos).astype(BF16)
        k_ref[:, c0:c0 + LANES] = k_nope[:, s0:s0 + LANES].astype(BF16)
        k_ref[:, c0 + LANES:c0 + 2 * LANES] = k_pe


def _mla_prep(h_b, pos_f, invf, w_lat, qg, kg, w_qn, w_qp, w_kn, w_v, tm=512):
    t, d = h_b.shape
    full = lambda a: pl.BlockSpec(a.shape, lambda i: (0,) * a.ndim)
    wq = MLA_HEADS * MLA_QK_PAD
    return pl.pallas_call(
        _mla_prep_kernel,
        grid=(t // tm,),
        in_specs=[pl.BlockSpec((tm, d), lambda i: (i, 0)),
                  pl.BlockSpec((tm, 1), lambda i: (i, 0)),
                  full(invf), full(w_lat), full(qg), full(kg), full(w_qn), full(w_qp),
                  full(w_kn), full(w_v)],
        out_specs=[pl.BlockSpec((tm, wq), lambda i: (i, 0)),
                   pl.BlockSpec((tm, wq), lambda i: (i, 0)),
                   pl.BlockSpec((tm, MLA_WIDTH), lambda i: (i, 0))],
        out_shape=[jax.ShapeDtypeStruct((t, wq), BF16), jax.ShapeDtypeStruct((t, wq), BF16),
                   jax.ShapeDtypeStruct((t, MLA_WIDTH), BF16)],
        compiler_params=_cparams(1),
        name="mla_prep",
    )(h_b, pos_f, invf, w_lat, qg, kg, w_qn, w_qp, w_kn, w_v)


def _sb_attn_kernel(q_ref, k_ref, v_ref, o_ref, *, tq, tk, heads, scale):
    i = pl.program_id(2)
    t_idx = i * tq + lax.broadcasted_iota(jnp.int32, (tq, 1), 0)
    r_i = lax.broadcasted_iota(jnp.int32, (tk, tk), 0)
    c_i = lax.broadcasted_iota(jnp.int32, (tk, tk), 1)
    suffix = (r_i >= c_i).astype(BF16)
    suffix2 = jnp.concatenate([suffix, suffix], axis=0)
    nkb = (i + 1) * (tq // tk)

    def block(j, hd, c, acc, diagonal):
        cols = slice(hd * HEAD_DIM, (hd + 1) * HEAD_DIM)
        k0 = pl.multiple_of(j * tk, tk)
        ks = k_ref[pl.ds(k0, tk), cols]
        vs = v_ref[pl.ds(k0, tk), cols]
        z = lax.dot_general(q_ref[:, cols], ks, (((1,), (1,)), ((), ())),
                            preferred_element_type=F32) * scale
        sp = jnp.maximum(z, 0.0) + jnp.log(1.0 + jnp.exp(-jnp.abs(z)))
        if diagonal:
            past = (k0 + lax.broadcasted_iota(jnp.int32, (1, tk), 1)) < t_idx
            sp = jnp.where(past, sp, 0.0)
        hi = sp.astype(BF16)
        lo = (sp - hi.astype(F32)).astype(BF16)
        incl = jnp.dot(jnp.concatenate([hi, lo], axis=1), suffix2, preferred_element_type=F32)
        w = jnp.exp(z - incl - c)
        if diagonal:
            w = jnp.where(past, w, 0.0)
        acc = acc + jnp.dot(w.astype(BF16), vs, preferred_element_type=F32)
        return c + incl[:, 0:1], acc

    bpt = tq // tk

    def trip(p, carry, diagonal):
        out = []
        for hd in range(heads):
            c, acc = carry[hd]
            for bk in range(bpt):
                c, acc = block(nkb - 1 - bpt * p - bk, hd, c, acc, diagonal)
            out.append((c, acc))
        return tuple(out)

    init = tuple((jnp.zeros((tq, 1), F32), jnp.zeros((tq, HEAD_DIM), F32)) for _ in range(heads))
    res = trip(0, init, True)
    res = lax.fori_loop(1, i + 1, functools.partial(trip, diagonal=False), res)
    for hd in range(heads):
        o_ref[:, hd * HEAD_DIM:(hd + 1) * HEAD_DIM] = res[hd][1].astype(o_ref.dtype)


def _sb_attention(u_sb3, tq=256, tk=128, heads=SB_HEADS):
    b, s, _ = u_sb3.shape
    assert tq % tk == 0 and SB_HEADS % heads == 0
    groups = SB_HEADS // heads
    w = heads * HEAD_DIM
    kern = functools.partial(_sb_attn_kernel, tq=tq, tk=tk, heads=heads, scale=HEAD_DIM ** -0.5)
    return pl.pallas_call(
        kern,
        grid=(b, groups, s // tq),
        in_specs=[pl.BlockSpec((None, tq, w), lambda bb, g, i: (bb, i, g)),
                  pl.BlockSpec((None, s, w), lambda bb, g, i: (bb, 0, groups + g)),
                  pl.BlockSpec((None, s, w), lambda bb, g, i: (bb, 0, 2 * groups + g))],
        out_specs=pl.BlockSpec((None, tq, w), lambda bb, g, i: (bb, i, g)),
        out_shape=jax.ShapeDtypeStruct((b, s, SB_WIDTH), BF16),
        compiler_params=_cparams(3),
        name="sb_attention",
    )(u_sb3, u_sb3, u_sb3)


def _mla_attn_kernel(q_ref, k_ref, v_ref, o_ref, *, tq, heads, scale):
    i = pl.program_id(2)
    t_chunk = lax.shift_right_logical(
        i * tq + lax.broadcasted_iota(jnp.int32, (tq, 1), 0), CHUNK.bit_length() - 1)

    def body(j, carry):
        k0 = pl.multiple_of(j * tq, tq)
        s_chunk = lax.shift_right_logical(
            k0 + lax.broadcasted_iota(jnp.int32, (1, tq), 1), CHUNK.bit_length() - 1)
        visible = s_chunk <= t_chunk
        out = []
        for hd in range(heads):
            m, l, acc = carry[hd]
            qk = slice(hd * MLA_QK_PAD, (hd + 1) * MLA_QK_PAD)
            ks = k_ref[pl.ds(k0, tq), qk]
            vs = v_ref[pl.ds(k0, tq), hd * MLA_V:(hd + 1) * MLA_V]
            s = lax.dot_general(q_ref[:, qk], ks, (((1,), (1,)), ((), ())),
                                preferred_element_type=F32) * scale
            s = jnp.where(visible, s, -1e30)
            m_new = jnp.maximum(m, jnp.max(s, -1, keepdims=True))
            alpha = jnp.exp(m - m_new)
            p = jnp.exp(s - m_new)
            l = alpha * l + jnp.sum(p, -1, keepdims=True)
            acc = alpha * acc + jnp.dot(p.astype(BF16), vs, preferred_element_type=F32)
            out.append((m_new, l, acc))
        return tuple(out)

    init = tuple((jnp.full((tq, 1), -1e30, F32), jnp.zeros((tq, 1), F32),
                  jnp.zeros((tq, MLA_V), F32)) for _ in range(heads))
    res = lax.fori_loop(0, i + 1, body, init)
    for hd in range(heads):
        _, l, acc = res[hd]
        o_ref[:, hd * MLA_V:(hd + 1) * MLA_V] = (acc / l).astype(o_ref.dtype)


def _mla_attention(q3, k3, v3, tq=256, heads=MLA_HEADS):
    b, s, _ = q3.shape
    assert MLA_HEADS % heads == 0
    kern = functools.partial(_mla_attn_kernel, tq=tq, heads=heads,
                             scale=(MLA_NOPE + MLA_ROPE) ** -0.5)
    return pl.pallas_call(
        kern,
        grid=(b, MLA_HEADS // heads, s // tq),
        in_specs=[pl.BlockSpec((None, tq, heads * MLA_QK_PAD), lambda bb, g, i: (bb, i, g)),
                  pl.BlockSpec((None, s, heads * MLA_QK_PAD), lambda bb, g, i: (bb, 0, g)),
                  pl.BlockSpec((None, s, heads * MLA_V), lambda bb, g, i: (bb, 0, g))],
        out_specs=pl.BlockSpec((None, tq, heads * MLA_V), lambda bb, g, i: (bb, i, g)),
        out_shape=jax.ShapeDtypeStruct((b, s, MLA_WIDTH), BF16),
        compiler_params=_cparams(3),
        name="mla_attention",
    )(q3, k3, v3)


def _mix_router_kernel(sb_ref, mla_ref, x_ref, lig_ref, lib_ref, sg_ref, mg_ref, wo_ref,
                       lmg_ref, lmb_ref, wr_ref, br_ref,
                       h2_ref, h2p_ref, sel_ref, gate_ref, cnt_ref, run_ref, *, tm):
    step = pl.program_id(0)

    @pl.when(step == 0)
    def _():
        run_ref[...] = jnp.zeros_like(run_ref)

    a = _rms_norm(sb_ref[...].astype(F32), sg_ref[...]).astype(BF16)
    bm = _rms_norm(mla_ref[...].astype(F32), mg_ref[...]).astype(BF16)
    mix = (jnp.dot(a, wo_ref[:SB_WIDTH, :], preferred_element_type=F32)
           + jnp.dot(bm, wo_ref[SB_WIDTH:, :], preferred_element_type=F32))
    h = _layer_norm(x_ref[...], lig_ref[...], lib_ref[...])
    h2 = _layer_norm(DEEPNORM_ALPHA * h + mix, lmg_ref[...], lmb_ref[...])
    h2_ref[...] = h2
    h2p_ref[...] = _pack_halves(h2[:, :HALF], h2[:, HALF:])

    h2_hi = h2.astype(BF16)
    h2_lo = (h2 - h2_hi.astype(F32)).astype(BF16)
    hw = jnp.dot(h2_hi, wr_ref[...], preferred_element_type=F32)
    logits = (hw[:, :LANES] + hw[:, LANES:]
              + jnp.dot(h2_lo, wr_ref[:, :LANES], preferred_element_type=F32) + br_ref[...])
    lane = lax.broadcasted_iota(jnp.int32, (tm, LANES), 1)
    lane_f = lane.astype(F32)
    logits = jnp.where(lane < N_EXPERTS, logits, -jnp.inf)
    vals, hots = [], []
    sel = jnp.zeros((tm, LANES), F32)
    for k in range(TOP_K):
        m = jnp.max(logits, -1, keepdims=True)
        idx = jnp.min(jnp.where(logits == m, lane_f, float(LANES)), -1, keepdims=True)
        hot = lane_f == idx
        logits = jnp.where(hot, -jnp.inf, logits)
        vals.append(m)
        hots.append(hot)
        sel = jnp.where(lane == k, idx, sel)
    exps = [jnp.exp(v - vals[0]) for v in vals]
    denom = exps[0] + exps[1] + exps[2] + exps[3]
    gates = jnp.zeros((tm, LANES), F32)
    for k in range(TOP_K):
        gates = jnp.where(lane == k, exps[k] / denom, gates)
    gate_ref[...] = gates

    onehot = (hots[0] | hots[1] | hots[2] | hots[3]).astype(F32)
    r_i = lax.broadcasted_iota(jnp.int32, (tm, tm), 0)
    c_i = lax.broadcasted_iota(jnp.int32, (tm, tm), 1)
    before = (c_i < r_i).astype(BF16)
    rank = jnp.dot(before, onehot.astype(BF16), preferred_element_type=F32) + run_ref[...]
    for k in range(TOP_K):
        pos = jnp.sum(jnp.where(hots[k], rank, 0.0), -1, keepdims=True)
        sel = jnp.where(lane == TOP_K + k, pos, sel)
    sel_ref[...] = sel
    run_ref[...] = run_ref[...] + jnp.sum(onehot, 0, keepdims=True)
    cnt_ref[...] = run_ref[...]


def _mix_router(sb_o, mla_o, x2, lig, lib, sg, mg, w_o, lmg, lmb, w_r, b_r, tm=256):
    t, d = x2.shape
    full = lambda a: pl.BlockSpec(a.shape, lambda i: (0,) * a.ndim)
    row = lambda w: pl.BlockSpec((tm, w), lambda i: (i, 0))
    return pl.pallas_call(
        functools.partial(_mix_router_kernel, tm=tm),
        grid=(t // tm,),
        in_specs=[row(SB_WIDTH), row(MLA_WIDTH), row(d), full(lig), full(lib), full(sg), full(mg),
                  full(w_o), full(lmg), full(lmb), full(w_r), full(b_r)],
        out_specs=[row(d), row(HALF), row(LANES), row(LANES),
                   pl.BlockSpec((1, LANES), lambda i: (0, 0))],
        out_shape=[jax.ShapeDtypeStruct((t, d), F32), jax.ShapeDtypeStruct((t, HALF), U32),
                   jax.ShapeDtypeStruct((t, LANES), F32), jax.ShapeDtypeStruct((t, LANES), F32),
                   jax.ShapeDtypeStruct((1, LANES), F32)],
        scratch_shapes=[pltpu.VMEM((1, LANES), F32)],
        compiler_params=_cparams(1),
        name="mix_router",
    )(sb_o, mla_o, x2, lig, lib, sg, mg, w_o, lmg, lmb, w_r, b_r)


ROWS_PER_STEP = 2048


def _wait_rows(like_hbm, sem, rows):
    pltpu.make_async_copy(like_hbm.at[pl.ds(0, rows)], like_hbm.at[pl.ds(0, rows)], sem).wait()


def _gather_rows_kernel(idx_ref, src_hbm, out_ref, sem, *, rows):
    def issue(g, _):
        r0 = pl.multiple_of(g * 8, 8)
        for k in range(8):
            pltpu.make_async_copy(src_hbm.at[pl.ds(idx_ref[0, 0, r0 + k], 1)],
                                  out_ref.at[pl.ds(r0 + k, 1)], sem).start()
        return 0

    lax.fori_loop(0, rows // 8, issue, 0)
    _wait_rows(src_hbm, sem, rows)


def _scatter_rows_kernel(idx_ref, src_ref, out_hbm, zero_ref, sem, *, rows, copies, src_steps):
    step = pl.program_id(0)

    @pl.when(step == 0)
    def _():
        zero_ref[...] = jnp.zeros_like(zero_ref)

    @pl.when(step < src_steps)
    def _():
        def issue(g, _):
            r0 = pl.multiple_of(g * 8, 8)
            for k in range(8):
                pltpu.make_async_copy(src_ref.at[pl.ds(g * (8 // copies) + k // copies, 1)],
                                      out_hbm.at[pl.ds(idx_ref[0, 0, r0 + k], 1)], sem).start()
            return 0
        lax.fori_loop(0, rows // 8, issue, 0)

    @pl.when(step >= src_steps)
    def _():
        def issue(a, _):
            pltpu.make_async_copy(zero_ref.at[pl.ds(0, 1)],
                                  out_hbm.at[pl.ds(idx_ref[0, 0, a], 1)], sem).start()
            return 0
        lax.fori_loop(0, rows, issue, 0, unroll=8)

    _wait_rows(out_hbm, sem, rows)


def _gather_rows(src, idx):
    rows = ROWS_PER_STEP
    n, w = idx.shape[0], src.shape[1]
    assert n % rows == 0 and src.shape[0] >= rows
    return pl.pallas_call(
        functools.partial(_gather_rows_kernel, rows=rows),
        grid=(n // rows,),
        in_specs=[pl.BlockSpec((1, 1, rows), lambda i: (i, 0, 0), memory_space=pltpu.SMEM),
                  pl.BlockSpec(memory_space=pl.ANY)],
        out_specs=pl.BlockSpec((rows, w), lambda i: (i, 0)),
        out_shape=jax.ShapeDtypeStruct((n, w), src.dtype),
        scratch_shapes=[pltpu.SemaphoreType.DMA(())],
        compiler_params=_cparams(1),
        name="gather_rows",
    )(idx.reshape(n // rows, 1, rows), src)


def _scatter_rows(src, copies, idx, out_rows):
    rows = ROWS_PER_STEP
    n, w = idx.shape[0], src.shape[1]
    n_src = src.shape[0] * copies
    assert n_src % rows == 0 and n % rows == 0 and n == out_rows and 8 % copies == 0
    src_steps = n_src // rows
    kern = functools.partial(_scatter_rows_kernel, rows=rows, copies=copies, src_steps=src_steps)
    return pl.pallas_call(
        kern,
        grid=(n // rows,),
        in_specs=[pl.BlockSpec((1, 1, rows), lambda i: (i, 0, 0), memory_space=pltpu.SMEM),
                  pl.BlockSpec((rows // copies, w), lambda i: (jnp.minimum(i, src_steps - 1), 0))],
        out_specs=pl.BlockSpec(memory_space=pl.ANY),
        out_shape=jax.ShapeDtypeStruct((out_rows, w), src.dtype),
        scratch_shapes=[pltpu.VMEM((8, w), src.dtype), pltpu.SemaphoreType.DMA(())],
        compiler_params=_cparams(1),
        name="scatter_rows",
    )(idx.reshape(n // rows, 1, rows), src)


MOE_ROWS = 1536
MOE_SUB = 256
MOE_TF = 256
MOE_NF = D_FF // MOE_TF
MOE_ITEMS = N_EXPERTS + (N_EXPERTS * (MOE_BLOCK - 1) + 8192 * TOP_K) // MOE_ROWS + 1


MOE_GU_SLOTS = 2
MOE_WD_SLOTS = 3


def _moe_kernel(ie_ref, is_ref, nb_ref, tail_ref, xs_hbm, wgu_hbm, wd_hbm, bgu_ref, bd_ref,
                ys_hbm, pkin_ref, pkout_ref, x_ref, acc_ref, wg_buf, wu_buf, wd_buf,
                pend_ref, sem_in, sem_out, sem_w):
    i = pl.program_id(0)
    nb = nb_ref[i]
    start = is_ref[i]
    n_out_blocks = ys_hbm.shape[0] // MOE_BLOCK

    def weight_copies(item, f):
        e = ie_ref[item]
        g = item * MOE_NF + f
        gs = lax.rem(g, MOE_GU_SLOTS)
        ws = lax.rem(g, MOE_WD_SLOTS)
        c0 = pl.multiple_of(f * MOE_TF, MOE_TF)
        return (pltpu.make_async_copy(wgu_hbm.at[e, :, pl.ds(c0, MOE_TF)], wg_buf.at[gs],
                                      sem_w.at[gs]),
                pltpu.make_async_copy(wgu_hbm.at[e, :, pl.ds(D_FF + c0, MOE_TF)], wu_buf.at[gs],
                                      sem_w.at[gs]),
                pltpu.make_async_copy(wd_hbm.at[e, pl.ds(c0, MOE_TF), :], wd_buf.at[ws],
                                      sem_w.at[MOE_GU_SLOTS + ws]))

    def vm_block(ref, b):
        return ref.at[pl.ds(pl.multiple_of(b * MOE_BLOCK, MOE_BLOCK), MOE_BLOCK)]

    def hbm_block(ref, first_row, b):
        return ref.at[pl.ds(pl.multiple_of(first_row + b * MOE_BLOCK, MOE_BLOCK), MOE_BLOCK)]

    def in_copy(first_row, b):
        return pltpu.make_async_copy(hbm_block(xs_hbm, first_row, b), vm_block(pkin_ref, b), sem_in)

    def out_copy(first_row, b):
        return pltpu.make_async_copy(vm_block(pkout_ref, b), hbm_block(ys_hbm, first_row, b), sem_out)

    def for_blocks(n, fn):
        def body(b, _):
            fn(b)
            return 0
        lax.fori_loop(0, n, body, 0)

    def next_item_valid():
        return (i + 1 < MOE_ITEMS) & (nb_ref[jnp.minimum(i + 1, MOE_ITEMS - 1)] > 0)

    @pl.when(i == 0)
    def _():
        x_ref[...] = jnp.zeros_like(x_ref)
        acc_ref[...] = jnp.zeros_like(acc_ref)
        pend_ref[0] = 0
        pend_ref[1] = 0
        for_blocks(nb, lambda b: in_copy(start, b).start())
        for c in weight_copies(0, 0):
            c.start()

    @pl.when(nb > 0)
    def _():
        def unpack(b):
            r0 = pl.multiple_of(b * MOE_BLOCK, MOE_BLOCK)
            lo, hi = _unpack_halves(pkin_ref[pl.ds(r0, MOE_BLOCK), :])
            x_ref[pl.ds(r0, MOE_BLOCK), :HALF] = lo.astype(BF16)
            x_ref[pl.ds(r0, MOE_BLOCK), HALF:] = hi.astype(BF16)

        for_blocks(nb, lambda b: in_copy(start, b).wait())
        for_blocks(nb, unpack)

        @pl.when(next_item_valid())
        def _():
            nxt = jnp.minimum(i + 1, MOE_ITEMS - 1)
            for_blocks(nb_ref[nxt], lambda b: in_copy(is_ref[nxt], b).start())

        n_sub = (nb * MOE_BLOCK + MOE_SUB - 1) // MOE_SUB

        def gate_up(f, s, gs):
            x = x_ref[pl.ds(pl.multiple_of(s * MOE_SUB, MOE_SUB), MOE_SUB), :]
            g = (jnp.dot(x, wg_buf[gs].astype(BF16), preferred_element_type=F32)
                 + bgu_ref[pl.ds(f, 1), :])
            u = (jnp.dot(x, wu_buf[gs].astype(BF16), preferred_element_type=F32)
                 + bgu_ref[pl.ds(MOE_NF + f, 1), :])
            g = jnp.minimum(g, SWIGLU_LIMIT)
            u = jnp.clip(u, -SWIGLU_LIMIT, SWIGLU_LIMIT)
            return ((u + 1.0) * (g * jax.nn.sigmoid(SWIGLU_ALPHA * g))).astype(BF16)

        def down(act, r0, ws, first):
            r0 = pl.multiple_of(r0, MOE_SUB)
            y = jnp.dot(act, wd_buf[ws].astype(BF16), preferred_element_type=F32)
            prev = jnp.where(first == 1, bd_ref[...], acc_ref[pl.ds(r0, MOE_SUB), :])
            acc_ref[pl.ds(r0, MOE_SUB), :] = prev + y

        def tile(f, carry):
            for c in weight_copies(i, f):
                c.wait()

            @pl.when(f + 1 < MOE_NF)
            def _():
                for c in weight_copies(i, f + 1):
                    c.start()

            @pl.when((f + 1 == MOE_NF) & next_item_valid())
            def _():
                for c in weight_copies(i + 1, 0):
                    c.start()

            g = i * MOE_NF + f
            gs = lax.rem(g, MOE_GU_SLOTS)
            ws = lax.rem(g, MOE_WD_SLOTS)
            first = jnp.asarray(f == 0, jnp.int32)

            def sub(s, c):
                act, r_prev, ws_prev, first_prev = c
                down(act, r_prev, ws_prev, first_prev)
                return gate_up(f, s, gs), s * MOE_SUB, ws, first

            return lax.fori_loop(0, n_sub, sub, carry)

        carry0 = (jnp.zeros((MOE_SUB, MOE_TF), BF16), jnp.int32(MOE_ROWS),
                  lax.rem(i * MOE_NF, MOE_WD_SLOTS), jnp.int32(0))
        down(*lax.fori_loop(0, MOE_NF, tile, carry0))

    def drain_pending_store():
        for_blocks(pend_ref[1], lambda b: out_copy(pend_ref[0], b).wait())
        pend_ref[1] = 0

    @pl.when(nb > 0)
    def _():
        def pack(b):
            r0 = pl.multiple_of(b * MOE_BLOCK, MOE_BLOCK)
            y = acc_ref[pl.ds(r0, MOE_BLOCK), :]
            pkout_ref[pl.ds(r0, MOE_BLOCK), :] = _pack_halves(y[:, :HALF], y[:, HALF:])

        drain_pending_store()
        for_blocks(nb, pack)
        for_blocks(nb, lambda b: out_copy(start, b).start())
        pend_ref[0] = start
        pend_ref[1] = nb

    @pl.when(i == MOE_ITEMS - 1)
    def _():
        drain_pending_store()
        pkout_ref[:MOE_BLOCK, :] = jnp.zeros((MOE_BLOCK, HALF), U32)

        def tail_copy(b):
            return pltpu.make_async_copy(
                pkout_ref.at[pl.ds(0, MOE_BLOCK)],
                ys_hbm.at[pl.ds(pl.multiple_of(b * MOE_BLOCK, MOE_BLOCK), MOE_BLOCK)], sem_out)

        def tail_blocks(fn):
            def body(b, _):
                fn(b)
                return 0
            lax.fori_loop(tail_ref[0], n_out_blocks, body, 0)

        tail_blocks(lambda b: tail_copy(b).start())
        tail_blocks(lambda b: tail_copy(b).wait())


def _moe_ffn(item_e, item_start, item_nb, tail_blk, xs, w_gu, b_gu, w_d, b_d, n_rows):
    by_expert = lambda i, ie, is_, nb, tail: (ie[i], 0, 0)
    grid_spec = pltpu.PrefetchScalarGridSpec(
        num_scalar_prefetch=4,
        grid=(MOE_ITEMS,),
        in_specs=[pl.BlockSpec(memory_space=pl.ANY),
                  pl.BlockSpec(memory_space=pl.ANY),
                  pl.BlockSpec(memory_space=pl.ANY),
                  pl.BlockSpec((None, 2 * MOE_NF, MOE_TF), by_expert),
                  pl.BlockSpec((None, 1, D_MODEL), by_expert)],
        out_specs=pl.BlockSpec(memory_space=pl.ANY),
        scratch_shapes=[pltpu.VMEM((MOE_ROWS, HALF), U32),
                        pltpu.VMEM((MOE_ROWS, HALF), U32),
                        pltpu.VMEM((MOE_ROWS, D_MODEL), BF16),
                        pltpu.VMEM((MOE_ROWS + MOE_SUB, D_MODEL), F32),
                        pltpu.VMEM((MOE_GU_SLOTS, D_MODEL, MOE_TF), F32),
                        pltpu.VMEM((MOE_GU_SLOTS, D_MODEL, MOE_TF), F32),
                        pltpu.VMEM((MOE_WD_SLOTS, MOE_TF, D_MODEL), F32),
                        pltpu.SMEM((2,), jnp.int32),
                        pltpu.SemaphoreType.DMA(()),
                        pltpu.SemaphoreType.DMA(()),
                        pltpu.SemaphoreType.DMA((MOE_GU_SLOTS + MOE_WD_SLOTS,))])
    return pl.pallas_call(
        _moe_kernel,
        grid_spec=grid_spec,
        out_shape=jax.ShapeDtypeStruct((n_rows, HALF), U32),
        compiler_params=_cparams(1),
        name="moe_ffn",
    )(item_e, item_start, item_nb, tail_blk, xs, w_gu, w_d, b_gu, b_d)


def _combine_kernel(y0_ref, y1_ref, y2_ref, y3_ref, gate_ref, h2_ref, g_ref, b_ref, o_ref):
    h2 = h2_ref[...]
    lo = DEEPNORM_ALPHA * h2[:, :HALF]
    hi = DEEPNORM_ALPHA * h2[:, HALF:]
    for k, y_ref in enumerate((y0_ref, y1_ref, y2_ref, y3_ref)):
        ylo, yhi = _unpack_halves(y_ref[...])
        gk = gate_ref[:, k:k + 1]
        lo = lo + gk * ylo
        hi = hi + gk * yhi
    mu = (jnp.sum(lo, -1, keepdims=True) + jnp.sum(hi, -1, keepdims=True)) * (1.0 / D_MODEL)
    lo = lo - mu
    hi = hi - mu
    var = (jnp.sum(lo * lo, -1, keepdims=True) + jnp.sum(hi * hi, -1, keepdims=True)) * (1.0 / D_MODEL)
    inv = lax.rsqrt(var + LN_EPS)
    o_ref[:, :HALF] = lo * inv * g_ref[:, :HALF] + b_ref[:, :HALF]
    o_ref[:, HALF:] = hi * inv * g_ref[:, HALF:] + b_ref[:, HALF:]


def _combine(yg, gates, h2, g, b, tm=256):
    t, d = h2.shape
    full = lambda a: pl.BlockSpec(a.shape, lambda i: (0,) * a.ndim)
    return pl.pallas_call(
        _combine_kernel,
        grid=(t // tm,),
        in_specs=[pl.BlockSpec((None, tm, HALF), functools.partial(lambda k, i: (k, i, 0), k))
                  for k in range(TOP_K)] + [
                  pl.BlockSpec((tm, LANES), lambda i: (i, 0)),
                  pl.BlockSpec((tm, d), lambda i: (i, 0)), full(g), full(b)],
        out_specs=pl.BlockSpec((tm, d), lambda i: (i, 0)),
        out_shape=jax.ShapeDtypeStruct((t, d), F32),
        compiler_params=_cparams(1),
        name="combine_ln",
    )(yg, yg, yg, yg, gates, h2, g, b)


def _routing_tables(sel, counts_f):
    t = sel.shape[0]
    idx = sel[:, :TOP_K].astype(jnp.int32)
    pos = sel[:, TOP_K:2 * TOP_K].astype(jnp.int32)
    counts = counts_f[0, :N_EXPERTS].astype(jnp.int32)
    padded = ((counts + MOE_BLOCK - 1) // MOE_BLOCK) * MOE_BLOCK
    pends = jnp.cumsum(padded)
    pstarts = pends - padded
    dest = (pstarts[idx] + pos).reshape(-1)
    n_rows = t * TOP_K + N_EXPERTS * MOE_BLOCK
    j = jnp.arange(MOE_BLOCK, dtype=jnp.int32)[None, :]
    is_pad = j < (padded - counts)[:, None]
    n_after = jnp.cumsum((~is_pad).reshape(-1).astype(jnp.int32)).reshape(is_pad.shape) - 1
    zero_rows = jnp.where(is_pad, (pstarts + counts)[:, None] + j, pends[-1] + n_after)
    row_of = jnp.concatenate([dest, zero_rows.reshape(-1)])

    nchunk = (padded + MOE_ROWS - 1) // MOE_ROWS
    cend = jnp.cumsum(nchunk)
    cstart = cend - nchunk
    total = cend[-1]
    it = jnp.arange(MOE_ITEMS, dtype=jnp.int32)
    valid = it < total
    it_c = jnp.minimum(it, total - 1)
    ie = jnp.searchsorted(cend, it_c, side="right").astype(jnp.int32)
    c = it_c - cstart[ie]
    istart = (pstarts[ie] + c * MOE_ROWS).astype(jnp.int32)
    inb = jnp.where(valid, jnp.minimum(MOE_ROWS, padded[ie] - c * MOE_ROWS) // MOE_BLOCK, 0)
    tail_blk = (pends[-1] // MOE_BLOCK).astype(jnp.int32).reshape(1)
    return dest, row_of, ie, istart, inb.astype(jnp.int32), tail_blk, n_rows


def kernel(x, positions, ln_in_g, ln_in_b, w_in, q_a_norm, w_q_b, kv_a_norm, w_kv_b, sb_out_norm,
           mla_out_norm, w_o, ln_mix_g, ln_mix_b, w_router, b_router, w_gate_up, b_gate_up, w_down,
           b_down, ln_ffn_g, ln_ffn_b):
    bsz, seq, d = x.shape
    t = bsz * seq
    x2 = x.reshape(t, d)
    row = lambda v: v.reshape(1, -1)

    w_in0 = w_in[0]
    w_sb = w_in0[:, :3 * SB_WIDTH].astype(BF16)
    w_lat = jnp.pad(w_in0[:, 3 * SB_WIDTH:], ((0, 0), (0, LANES - MLA_ROPE))).astype(BF16)
    wq = w_q_b[0].reshape(MLA_Q_RANK, MLA_HEADS, MLA_NOPE + MLA_ROPE)
    w_qn = wq[:, :, :MLA_NOPE].reshape(MLA_Q_RANK, -1).astype(BF16)
    w_qp = jnp.pad(wq[:, :, MLA_NOPE:], ((0, 0), (0, 0), (0, LANES - MLA_ROPE))
                   ).reshape(MLA_Q_RANK, -1).astype(BF16)
    wkv = w_kv_b[0].reshape(MLA_KV_RANK, MLA_HEADS, MLA_NOPE + MLA_V)
    w_kn = wkv[:, :, :MLA_NOPE].reshape(MLA_KV_RANK, -1).astype(BF16)
    w_v = wkv[:, :, MLA_NOPE:].reshape(MLA_KV_RANK, -1).astype(BF16)
    w_r32 = jnp.pad(w_router[0], ((0, 0), (0, LANES - N_EXPERTS)))
    w_r_hi = w_r32.astype(BF16)
    w_r = jnp.concatenate([w_r_hi, (w_r32 - w_r_hi.astype(F32)).astype(BF16)], axis=1)
    b_r = jnp.pad(b_router[0], (0, LANES - N_EXPERTS)).reshape(1, LANES)
    half = MLA_ROPE // 2
    inv_freq = ROPE_THETA ** (-(jnp.arange(half, dtype=F32) * 2.0 / MLA_ROPE))
    invf = jnp.concatenate([inv_freq, inv_freq, jnp.zeros((LANES - MLA_ROPE,), F32)]).reshape(1, LANES)
    pos_f = positions.reshape(t, 1).astype(F32)

    for l in range(DEPTH):
        u_sb, h_b = _ln_inproj(x2, row(ln_in_g), row(ln_in_b), w_sb)
        q_m, k_m, v_m = _mla_prep(h_b, pos_f, invf, w_lat, row(q_a_norm[l]), row(kv_a_norm[l]),
                                  w_qn, w_qp, w_kn, w_v)
        sb_o = _sb_attention(u_sb.reshape(bsz, seq, -1)).reshape(t, -1)
        mla_o = _mla_attention(q_m.reshape(bsz, seq, -1), k_m.reshape(bsz, seq, -1),
                               v_m.reshape(bsz, seq, -1)).reshape(t, -1)
        h2, h2p, sel, gates, counts_f = _mix_router(
            sb_o, mla_o, x2, row(ln_in_g), row(ln_in_b), row(sb_out_norm[l]), row(mla_out_norm[l]),
            w_o[l].astype(BF16), row(ln_mix_g[l]), row(ln_mix_b[l]), w_r, b_r)
        dest, row_of, ie, istart, inb, tail_blk, n_rows = _routing_tables(sel, counts_f)
        xs = _scatter_rows(h2p, TOP_K, row_of, n_rows)
        ys = _moe_ffn(ie, istart, inb, tail_blk, xs, w_gate_up[l],
                      b_gate_up[l].reshape(N_EXPERTS, 2 * MOE_NF, MOE_TF),
                      w_down[l], b_down[l].reshape(N_EXPERTS, 1, -1), n_rows)
        yg = _gather_rows(ys, dest.reshape(t, TOP_K).T.reshape(-1)).reshape(TOP_K, t, HALF)
        out = _combine(yg, gates, h2, row(ln_ffn_g[l]), row(ln_ffn_b[l]))
    return out.reshape(bsz, seq, d)
```

```python
import functools

import jax
import jax.numpy as jnp
from jax import lax
from jax.experimental import pallas as pl
from jax.experimental.pallas import tpu as pltpu

D_MODEL = 2048
DEPTH = 1
CHUNK = 64
SB_HEADS = 8
HEAD_DIM = 128
SB_WIDTH = SB_HEADS * HEAD_DIM
MLA_HEADS = 8
MLA_Q_RANK = 512
MLA_KV_RANK = 256
MLA_NOPE = 128
MLA_ROPE = 64
MLA_V = 128
MLA_WIDTH = MLA_HEADS * MLA_V
MLA_QK_PAD = 256
ROPE_THETA = 10000.0
N_EXPERTS = 32
TOP_K = 4
D_FF = 2048
SWIGLU_LIMIT = 7.0
SWIGLU_ALPHA = 1.702
MOE_BLOCK = 128
LN_EPS = 1e-5
RMS_EPS = 1e-6
DEEPNORM_ALPHA = (2 * DEPTH) ** 0.25

LANES = 128
HALF = D_MODEL // 2
VMEM_LIMIT_BYTES = 56 * 1024 * 1024

BF16 = jnp.bfloat16
F32 = jnp.float32
U32 = jnp.uint32


def _cparams(n_axes, vmem=None, flags=None):
    return pltpu.CompilerParams(dimension_semantics=("arbitrary",) * n_axes,
                                vmem_limit_bytes=vmem or VMEM_LIMIT_BYTES, flags=flags)


def _layer_norm(x, g, b):
    mu = jnp.mean(x, -1, keepdims=True)
    xc = x - mu
    var = jnp.mean(xc * xc, -1, keepdims=True)
    return xc * lax.rsqrt(var + LN_EPS) * g + b


def _rms_norm(x, g):
    ms = jnp.mean(x * x, -1, keepdims=True)
    return x * lax.rsqrt(ms + RMS_EPS) * g


def _pack_halves(lo_f32, hi_f32):
    lo = lax.bitcast_convert_type(lo_f32.astype(BF16).astype(F32), U32)
    hi = lax.bitcast_convert_type(hi_f32.astype(BF16).astype(F32), U32)
    return lax.shift_right_logical(lo, U32(16)) | (hi & U32(0xFFFF0000))


def _unpack_halves(w):
    lo = lax.bitcast_convert_type(lax.shift_left(w, U32(16)), F32)
    hi = lax.bitcast_convert_type(w & U32(0xFFFF0000), F32)
    return lo, hi


def _ln_inproj_kernel(x_ref, g_ref, b_ref, w_ref, u_ref, h_ref):
    h = _layer_norm(x_ref[...], g_ref[...], b_ref[...]).astype(BF16)
    h_ref[...] = h
    u_ref[...] = jnp.dot(h, w_ref[...], preferred_element_type=F32).astype(BF16)


def _ln_inproj(x2, g, b, w_sb, tm=512):
    t, d = x2.shape
    n = w_sb.shape[1]
    return pl.pallas_call(
        _ln_inproj_kernel,
        grid=(t // tm,),
        in_specs=[pl.BlockSpec((tm, d), lambda i: (i, 0)),
                  pl.BlockSpec((1, d), lambda i: (0, 0)),
                  pl.BlockSpec((1, d), lambda i: (0, 0)),
                  pl.BlockSpec((d, n), lambda i: (0, 0))],
        out_specs=[pl.BlockSpec((tm, n), lambda i: (i, 0)),
                   pl.BlockSpec((tm, d), lambda i: (i, 0))],
        out_shape=[jax.ShapeDtypeStruct((t, n), BF16), jax.ShapeDtypeStruct((t, d), BF16)],
        compiler_params=_cparams(1),
        name="ln_inproj",
    )(x2, g, b, w_sb)


def _rope_slab(x, cos, sin_neg, sin_pos):
    return (x * cos + pltpu.roll(x, LANES - MLA_ROPE // 2, 1) * sin_neg
            + pltpu.roll(x, MLA_ROPE // 2, 1) * sin_pos)


def _mla_prep_kernel(h_ref, pos_ref, invf_ref, wl_ref, qg_ref, kg_ref, wqn_ref, wqp_ref,
                     wkn_ref, wv_ref, q_ref, k_ref, v_ref):
    lat = jnp.dot(h_ref[...], wl_ref[...], preferred_element_type=F32)
    qn = _rms_norm(lat[:, :MLA_Q_RANK], qg_ref[...]).astype(BF16)
    kvn = _rms_norm(lat[:, MLA_Q_RANK:MLA_Q_RANK + MLA_KV_RANK], kg_ref[...]).astype(BF16)
    k_rope = lat[:, MLA_Q_RANK + MLA_KV_RANK:]

    ang = pos_ref[...] * invf_ref[...]
    cos = jnp.cos(ang)
    sin = jnp.sin(ang)
    lane = lax.broadcasted_iota(jnp.int32, ang.shape, 1)
    sin_neg = jnp.where(lane < MLA_ROPE // 2, -sin, 0.0)
    sin_pos = jnp.where((lane >= MLA_ROPE // 2) & (lane < MLA_ROPE), sin, 0.0)

    q_nope = jnp.dot(qn, wqn_ref[...], preferred_element_type=F32)
    q_pe = jnp.dot(qn, wqp_ref[...], preferred_element_type=F32)
    k_nope = jnp.dot(kvn, wkn_ref[...], preferred_element_type=F32)
    v_ref[...] = jnp.dot(kvn, wv_ref[...], preferred_element_type=F32).astype(BF16)
    k_pe = _rope_slab(k_rope, cos, sin_neg, sin_pos).astype(BF16)
    for hd in range(MLA_HEADS):
        c0 = hd * MLA_QK_PAD
        s0 = hd * LANES
        q_ref[:, c0:c0 + LANES] = q_nope[:, s0:s0 + LANES].astype(BF16)
        q_ref[:, c0 + LANES:c0 + 2 * LANES] = _rope_slab(
            q_pe[:, s0:s0 + LANES], cos, sin_neg, sin_pos).astype(BF16)
        k_ref[:, c0:c0 + LANES] = k_nope[:, s0:s0 + LANES].astype(BF16)
        k_ref[:, c0 + LANES:c0 + 2 * LANES] = k_pe


def _mla_prep(h_b, pos_f, invf, w_lat, qg, kg, w_qn, w_qp, w_kn, w_v, tm=512):
    t, d = h_b.shape
    full = lambda a: pl.BlockSpec(a.shape, lambda i: (0,) * a.ndim)
    wq = MLA_HEADS * MLA_QK_PAD
    return pl.pallas_call(
        _mla_prep_kernel,
        grid=(t // tm,),
        in_specs=[pl.BlockSpec((tm, d), lambda i: (i, 0)),
                  pl.BlockSpec((tm, 1), lambda i: (i, 0)),
                  full(invf), full(w_lat), full(qg), full(kg), full(w_qn), full(w_qp),
                  full(w_kn), full(w_v)],
        out_specs=[pl.BlockSpec((tm, wq), lambda i: (i, 0)),
                   pl.BlockSpec((tm, wq), lambda i: (i, 0)),
                   pl.BlockSpec((tm, MLA_WIDTH), lambda i: (i, 0))],
        out_shape=[jax.ShapeDtypeStruct((t, wq), BF16), jax.ShapeDtypeStruct((t, wq), BF16),
                   jax.ShapeDtypeStruct((t, MLA_WIDTH), BF16)],
        compiler_params=_cparams(1),
        name="mla_prep",
    )(h_b, pos_f, invf, w_lat, qg, kg, w_qn, w_qp, w_kn, w_v)


def _sb_attn_kernel(q_ref, k_ref, v_ref, o_ref, *, tq, tk, heads, scale):
    i = pl.program_id(2)
    t_idx = i * tq + lax.broadcasted_iota(jnp.int32, (tq, 1), 0)
    r_i = lax.broadcasted_iota(jnp.int32, (tk, tk), 0)
    c_i = lax.broadcasted_iota(jnp.int32, (tk, tk), 1)
    suffix = (r_i >= c_i).astype(BF16)
    suffix2 = jnp.concatenate([suffix, suffix], axis=0)
    nkb = (i + 1) * (tq // tk)

    def block(j, hd, c, acc, diagonal):
        cols = slice(hd * HEAD_DIM, (hd + 1) * HEAD_DIM)
        k0 = pl.multiple_of(j * tk, tk)
        ks = k_ref[pl.ds(k0, tk), cols]
        vs = v_ref[pl.ds(k0, tk), cols]
        z = lax.dot_general(q_ref[:, cols], ks, (((1,), (1,)), ((), ())),
                            preferred_element_type=F32) * scale
        sp = jnp.maximum(z, 0.0) + jnp.log(1.0 + jnp.exp(-jnp.abs(z)))
        if diagonal:
            past = (k0 + lax.broadcasted_iota(jnp.int32, (1, tk), 1)) < t_idx
            sp = jnp.where(past, sp, 0.0)
        hi = sp.astype(BF16)
        lo = (sp - hi.astype(F32)).astype(BF16)
        incl = jnp.dot(jnp.concatenate([hi, lo], axis=1), suffix2, preferred_element_type=F32)
        w = jnp.exp(z - incl - c)
        if diagonal:
            w = jnp.where(past, w, 0.0)
        acc = acc + jnp.dot(w.astype(BF16), vs, preferred_element_type=F32)
        return c + incl[:, 0:1], acc

    bpt = tq // tk

    def trip(p, carry, diagonal):
        out = []
        for hd in range(heads):
            c, acc = carry[hd]
            for bk in range(bpt):
                c, acc = block(nkb - 1 - bpt * p - bk, hd, c, acc, diagonal)
            out.append((c, acc))
        return tuple(out)

    init = tuple((jnp.zeros((tq, 1), F32), jnp.zeros((tq, HEAD_DIM), F32)) for _ in range(heads))
    res = trip(0, init, True)
    res = lax.fori_loop(1, i + 1, functools.partial(trip, diagonal=False), res)
    for hd in range(heads):
        o_ref[:, hd * HEAD_DIM:(hd + 1) * HEAD_DIM] = res[hd][1].astype(o_ref.dtype)


def _sb_attention(u_sb3, tq=256, tk=128, heads=SB_HEADS):
    b, s, _ = u_sb3.shape
    assert tq % tk == 0 and SB_HEADS % heads == 0
    groups = SB_HEADS // heads
    w = heads * HEAD_DIM
    kern = functools.partial(_sb_attn_kernel, tq=tq, tk=tk, heads=heads, scale=HEAD_DIM ** -0.5)
    return pl.pallas_call(
        kern,
        grid=(b, groups, s // tq),
        in_specs=[pl.BlockSpec((None, tq, w), lambda bb, g, i: (bb, i, g)),
                  pl.BlockSpec((None, s, w), lambda bb, g, i: (bb, 0, groups + g)),
                  pl.BlockSpec((None, s, w), lambda bb, g, i: (bb, 0, 2 * groups + g))],
        out_specs=pl.BlockSpec((None, tq, w), lambda bb, g, i: (bb, i, g)),
        out_shape=jax.ShapeDtypeStruct((b, s, SB_WIDTH), BF16),
        compiler_params=_cparams(3),
        name="sb_attention",
    )(u_sb3, u_sb3, u_sb3)


def _mla_attn_kernel(q_ref, k_ref, v_ref, o_ref, *, tq, tk, heads, scale):
    i = pl.program_id(2)
    t_chunk = lax.shift_right_logical(
        i * tq + lax.broadcasted_iota(jnp.int32, (tq, 1), 0), CHUNK.bit_length() - 1)

    def body(j, carry):
        k0 = pl.multiple_of(j * tk, tk)
        s_chunk = lax.shift_right_logical(
            k0 + lax.broadcasted_iota(jnp.int32, (1, tk), 1), CHUNK.bit_length() - 1)
        visible = s_chunk <= t_chunk
        def qk_scores(hd):
            qk = slice(hd * MLA_QK_PAD, (hd + 1) * MLA_QK_PAD)
            return lax.dot_general(q_ref[:, qk], k_ref[pl.ds(k0, tk), qk],
                                   (((1,), (1,)), ((), ())), preferred_element_type=F32)

        out = []
        nxt = qk_scores(0)
        for hd in range(heads):
            m, l, acc = carry[hd]
            s = nxt
            if hd + 1 < heads:
                nxt = qk_scores(hd + 1)
            s = jnp.where(visible, s * scale, -1e30)
            m_new = jnp.maximum(m, jnp.max(s, -1, keepdims=True))
            alpha = jnp.exp(m - m_new)
            p = jnp.exp(s - m_new)
            l = alpha * l + jnp.sum(p, -1, keepdims=True)
            vs = v_ref[pl.ds(k0, tk), hd * MLA_V:(hd + 1) * MLA_V]
            acc = alpha * acc + jnp.dot(p.astype(BF16), vs, preferred_element_type=F32)
            out.append((m_new, l, acc))
        return tuple(out)

    init = tuple((jnp.full((tq, 1), -1e30, F32), jnp.zeros((tq, 1), F32),
                  jnp.zeros((tq, MLA_V), F32)) for _ in range(heads))
    res = lax.fori_loop(0, (i + 1) * (tq // tk), body, init)
    for hd in range(heads):
        _, l, acc = res[hd]
        o_ref[:, hd * MLA_V:(hd + 1) * MLA_V] = (acc / l).astype(o_ref.dtype)


def _mla_attention(q3, k3, v3, tq=256, tk=256, heads=MLA_HEADS):
    b, s, _ = q3.shape
    assert MLA_HEADS % heads == 0 and tq % tk == 0 and tk % CHUNK == 0
    kern = functools.partial(_mla_attn_kernel, tq=tq, tk=tk, heads=heads,
                             scale=(MLA_NOPE + MLA_ROPE) ** -0.5)
    return pl.pallas_call(
        kern,
        grid=(b, MLA_HEADS // heads, s // tq),
        in_specs=[pl.BlockSpec((None, tq, heads * MLA_QK_PAD), lambda bb, g, i: (bb, i, g)),
                  pl.BlockSpec((None, s, heads * MLA_QK_PAD), lambda bb, g, i: (bb, 0, g)),
                  pl.BlockSpec((None, s, heads * MLA_V), lambda bb, g, i: (bb, 0, g))],
        out_specs=pl.BlockSpec((None, tq, heads * MLA_V), lambda bb, g, i: (bb, i, g)),
        out_shape=jax.ShapeDtypeStruct((b, s, MLA_WIDTH), BF16),
        compiler_params=_cparams(3),
        name="mla_attention",
    )(q3, k3, v3)


def _mix_router_kernel(sb_ref, mla_ref, x_ref, lig_ref, lib_ref, sg_ref, mg_ref, wo_ref,
                       lmg_ref, lmb_ref, wr_ref, br_ref,
                       h2_ref, h2p_ref, sel_ref, gate_ref, cnt_ref, run_ref, *, tm):
    step = pl.program_id(0)

    @pl.when(step == 0)
    def _():
        run_ref[...] = jnp.zeros_like(run_ref)

    a = _rms_norm(sb_ref[...].astype(F32), sg_ref[...]).astype(BF16)
    bm = _rms_norm(mla_ref[...].astype(F32), mg_ref[...]).astype(BF16)
    mix = (jnp.dot(a, wo_ref[:SB_WIDTH, :], preferred_element_type=F32)
           + jnp.dot(bm, wo_ref[SB_WIDTH:, :], preferred_element_type=F32))
    h = _layer_norm(x_ref[...], lig_ref[...], lib_ref[...])
    h2 = _layer_norm(DEEPNORM_ALPHA * h + mix, lmg_ref[...], lmb_ref[...])
    h2_ref[...] = h2
    h2p_ref[...] = _pack_halves(h2[:, :HALF], h2[:, HALF:])

    h2_hi = h2.astype(BF16)
    h2_lo = (h2 - h2_hi.astype(F32)).astype(BF16)
    hw = jnp.dot(h2_hi, wr_ref[...], preferred_element_type=F32)
    logits = (hw[:, :LANES] + hw[:, LANES:]
              + jnp.dot(h2_lo, wr_ref[:, :LANES], preferred_element_type=F32) + br_ref[...])
    lane = lax.broadcasted_iota(jnp.int32, (tm, LANES), 1)
    lane_f = lane.astype(F32)
    logits = jnp.where(lane < N_EXPERTS, logits, -jnp.inf)
    vals, hots = [], []
    sel = jnp.zeros((tm, LANES), F32)
    for k in range(TOP_K):
        m = jnp.max(logits, -1, keepdims=True)
        idx = jnp.min(jnp.where(logits == m, lane_f, float(LANES)), -1, keepdims=True)
        hot = lane_f == idx
        logits = jnp.where(hot, -jnp.inf, logits)
        vals.append(m)
        hots.append(hot)
        sel = jnp.where(lane == k, idx, sel)
    exps = [jnp.exp(v - vals[0]) for v in vals]
    denom = exps[0] + exps[1] + exps[2] + exps[3]
    gates = jnp.zeros((tm, LANES), F32)
    for k in range(TOP_K):
        gates = jnp.where(lane == k, exps[k] / denom, gates)
    gate_ref[...] = gates

    onehot = (hots[0] | hots[1] | hots[2] | hots[3]).astype(F32)
    r_i = lax.broadcasted_iota(jnp.int32, (tm, tm), 0)
    c_i = lax.broadcasted_iota(jnp.int32, (tm, tm), 1)
    before = (c_i < r_i).astype(BF16)
    rank = jnp.dot(before, onehot.astype(BF16), preferred_element_type=F32) + run_ref[...]
    for k in range(TOP_K):
        pos = jnp.sum(jnp.where(hots[k], rank, 0.0), -1, keepdims=True)
        sel = jnp.where(lane == TOP_K + k, pos, sel)
    sel_ref[...] = sel.T[:2 * TOP_K, :]
    run_ref[...] = run_ref[...] + jnp.sum(onehot, 0, keepdims=True)
    cnt_ref[...] = run_ref[...]


def _mix_router(sb_o, mla_o, x2, lig, lib, sg, mg, w_o, lmg, lmb, w_r, b_r, tm=256):
    t, d = x2.shape
    full = lambda a: pl.BlockSpec(a.shape, lambda i: (0,) * a.ndim)
    row = lambda w: pl.BlockSpec((tm, w), lambda i: (i, 0))
    return pl.pallas_call(
        functools.partial(_mix_router_kernel, tm=tm),
        grid=(t // tm,),
        in_specs=[row(SB_WIDTH), row(MLA_WIDTH), row(d), full(lig), full(lib), full(sg), full(mg),
                  full(w_o), full(lmg), full(lmb), full(w_r), full(b_r)],
        out_specs=[row(d), row(HALF), pl.BlockSpec((2 * TOP_K, tm), lambda i: (0, i)), row(LANES),
                   pl.BlockSpec((1, LANES), lambda i: (0, 0))],
        out_shape=[jax.ShapeDtypeStruct((t, d), F32), jax.ShapeDtypeStruct((t, HALF), U32),
                   jax.ShapeDtypeStruct((2 * TOP_K, t), F32), jax.ShapeDtypeStruct((t, LANES), F32),
                   jax.ShapeDtypeStruct((1, LANES), F32)],
        scratch_shapes=[pltpu.VMEM((1, LANES), F32)],
        compiler_params=_cparams(1),
        name="mix_router",
    )(sb_o, mla_o, x2, lig, lib, sg, mg, w_o, lmg, lmb, w_r, b_r)


ROWS_PER_STEP = 2048


def _wait_rows(like_hbm, sem, rows):
    pltpu.make_async_copy(like_hbm.at[pl.ds(0, rows)], like_hbm.at[pl.ds(0, rows)], sem).wait()


def _gather_rows_kernel(idx_ref, src_hbm, out_ref, sem, *, rows):
    def issue(g, _):
        r0 = pl.multiple_of(g * 8, 8)
        for k in range(8):
            pltpu.make_async_copy(src_hbm.at[pl.ds(idx_ref[0, 0, r0 + k], 1)],
                                  out_ref.at[pl.ds(r0 + k, 1)], sem).start()
        return 0

    lax.fori_loop(0, rows // 8, issue, 0)
    _wait_rows(src_hbm, sem, rows)


def _scatter_rows_kernel(idx_ref, src_ref, out_hbm, zero_ref, sem, *, rows, src_steps):
    step = pl.program_id(0)

    @pl.when(step == 0)
    def _():
        zero_ref[...] = jnp.zeros_like(zero_ref)

    @pl.when(step < src_steps)
    def _():
        def issue(g, _):
            r0 = pl.multiple_of(g * 8, 8)
            for k in range(8):
                pltpu.make_async_copy(src_ref.at[pl.ds(r0 + k, 1)],
                                      out_hbm.at[pl.ds(idx_ref[0, 0, r0 + k], 1)], sem).start()
            return 0
        lax.fori_loop(0, rows // 8, issue, 0)

    @pl.when(step >= src_steps)
    def _():
        def issue(a, _):
            pltpu.make_async_copy(zero_ref.at[pl.ds(0, 1)],
                                  out_hbm.at[pl.ds(idx_ref[0, 0, a], 1)], sem).start()
            return 0
        lax.fori_loop(0, rows, issue, 0, unroll=8)

    _wait_rows(out_hbm, sem, rows)


def _gather_rows(src, idx):
    rows = ROWS_PER_STEP
    n, w = idx.shape[0], src.shape[1]
    assert n % rows == 0 and src.shape[0] >= rows
    return pl.pallas_call(
        functools.partial(_gather_rows_kernel, rows=rows),
        grid=(n // rows,),
        in_specs=[pl.BlockSpec((1, 1, rows), lambda i: (i, 0, 0), memory_space=pltpu.SMEM),
                  pl.BlockSpec(memory_space=pl.ANY)],
        out_specs=pl.BlockSpec((rows, w), lambda i: (i, 0)),
        out_shape=jax.ShapeDtypeStruct((n, w), src.dtype),
        scratch_shapes=[pltpu.SemaphoreType.DMA(())],
        compiler_params=_cparams(1),
        name="gather_rows",
    )(idx.reshape(n // rows, 1, rows), src)


def _scatter_rows(src, passes, idx, out_rows):
    rows = ROWS_PER_STEP
    n, w = idx.shape[0], src.shape[1]
    src_blocks = src.shape[0] // rows
    assert src.shape[0] % rows == 0 and n % rows == 0 and n == out_rows
    src_steps = passes * src_blocks
    kern = functools.partial(_scatter_rows_kernel, rows=rows, src_steps=src_steps)
    return pl.pallas_call(
        kern,
        grid=(n // rows,),
        in_specs=[pl.BlockSpec((1, 1, rows), lambda i: (i, 0, 0), memory_space=pltpu.SMEM),
                  pl.BlockSpec((rows, w), lambda i: (lax.rem(i, src_blocks), 0))],
        out_specs=pl.BlockSpec(memory_space=pl.ANY),
        out_shape=jax.ShapeDtypeStruct((out_rows, w), src.dtype),
        scratch_shapes=[pltpu.VMEM((8, w), src.dtype), pltpu.SemaphoreType.DMA(())],
        compiler_params=_cparams(1),
        name="scatter_rows",
    )(idx.reshape(n // rows, 1, rows), src)


MOE_ROWS = 1536
MOE_SUB = 512
MOE_TF = 256
MOE_NF = D_FF // MOE_TF
MOE_ITEMS = N_EXPERTS + (N_EXPERTS * (MOE_BLOCK - 1) + 8192 * TOP_K) // MOE_ROWS + 1


MOE_GU_SLOTS = 2
MOE_WD_SLOTS = 3


def _moe_kernel(ie_ref, is_ref, nb_ref, tail_ref, xs_hbm, wgu_hbm, wd_hbm, bgu_ref, bd_ref,
                ys_hbm, pkin_ref, pkout_ref, x_ref, acc_ref, wg_buf, wu_buf, wd_buf,
                pend_ref, sem_in, sem_out, sem_w):
    i = pl.program_id(0)
    nb = nb_ref[i]
    start = is_ref[i]
    n_out_blocks = ys_hbm.shape[0] // MOE_BLOCK

    def weight_copies(item, f):
        e = ie_ref[item]
        g = item * MOE_NF + f
        gs = lax.rem(g, MOE_GU_SLOTS)
        ws = lax.rem(g, MOE_WD_SLOTS)
        c0 = pl.multiple_of(f * MOE_TF, MOE_TF)
        return (pltpu.make_async_copy(wgu_hbm.at[e, :, pl.ds(c0, MOE_TF)], wg_buf.at[gs],
                                      sem_w.at[gs]),
                pltpu.make_async_copy(wgu_hbm.at[e, :, pl.ds(D_FF + c0, MOE_TF)], wu_buf.at[gs],
                                      sem_w.at[gs]),
                pltpu.make_async_copy(wd_hbm.at[e, pl.ds(c0, MOE_TF), :], wd_buf.at[ws],
                                      sem_w.at[MOE_GU_SLOTS + ws]))

    def vm_block(ref, b):
        return ref.at[pl.ds(pl.multiple_of(b * MOE_BLOCK, MOE_BLOCK), MOE_BLOCK)]

    def hbm_block(ref, first_row, b):
        return ref.at[pl.ds(pl.multiple_of(first_row + b * MOE_BLOCK, MOE_BLOCK), MOE_BLOCK)]

    def in_copy(first_row, b):
        return pltpu.make_async_copy(hbm_block(xs_hbm, first_row, b), vm_block(pkin_ref, b), sem_in)

    def out_copy(first_row, b):
        return pltpu.make_async_copy(vm_block(pkout_ref, b), hbm_block(ys_hbm, first_row, b), sem_out)

    def for_blocks(n, fn):
        def body(b, _):
            fn(b)
            return 0
        lax.fori_loop(0, n, body, 0)

    def next_item_valid():
        return (i + 1 < MOE_ITEMS) & (nb_ref[jnp.minimum(i + 1, MOE_ITEMS - 1)] > 0)

    @pl.when(i == 0)
    def _():
        x_ref[...] = jnp.zeros_like(x_ref)
        acc_ref[...] = jnp.zeros_like(acc_ref)
        pend_ref[0] = 0
        pend_ref[1] = 0
        for_blocks(nb, lambda b: in_copy(start, b).start())
        for c in weight_copies(0, 0):
            c.start()

    @pl.when(nb > 0)
    def _():
        def unpack(b):
            r0 = pl.multiple_of(b * MOE_BLOCK, MOE_BLOCK)
            lo, hi = _unpack_halves(pkin_ref[pl.ds(r0, MOE_BLOCK), :])
            x_ref[pl.ds(r0, MOE_BLOCK), :HALF] = lo.astype(BF16)
            x_ref[pl.ds(r0, MOE_BLOCK), HALF:] = hi.astype(BF16)

        for_blocks(nb, lambda b: in_copy(start, b).wait())
        for_blocks(nb, unpack)

        @pl.when(next_item_valid())
        def _():
            nxt = jnp.minimum(i + 1, MOE_ITEMS - 1)
            for_blocks(nb_ref[nxt], lambda b: in_copy(is_ref[nxt], b).start())

        n_sub = (nb * MOE_BLOCK) // MOE_SUB
        n_tail = nb - n_sub * (MOE_SUB // MOE_BLOCK)

        def gate_up(f, r0, rows, gs):
            x = x_ref[pl.ds(pl.multiple_of(r0, MOE_BLOCK), rows), :]
            g = (jnp.dot(x, wg_buf[gs].astype(BF16), preferred_element_type=F32)
                 + bgu_ref[pl.ds(f, 1), :])
            u = (jnp.dot(x, wu_buf[gs].astype(BF16), preferred_element_type=F32)
                 + bgu_ref[pl.ds(MOE_NF + f, 1), :])
            g = jnp.minimum(g, SWIGLU_LIMIT)
            u = jnp.clip(u, -SWIGLU_LIMIT, SWIGLU_LIMIT)
            return ((u + 1.0) * (g * jax.nn.sigmoid(SWIGLU_ALPHA * g))).astype(BF16)

        def down(act, r0, ws, first):
            rows = act.shape[0]
            r0 = pl.multiple_of(r0, MOE_BLOCK)
            y = jnp.dot(act, wd_buf[ws].astype(BF16), preferred_element_type=F32)
            prev = jnp.where(first == 1, bd_ref[...], acc_ref[pl.ds(r0, rows), :])
            acc_ref[pl.ds(r0, rows), :] = prev + y

        def tile(f, carry):
            for c in weight_copies(i, f):
                c.wait()

            @pl.when(f + 1 < MOE_NF)
            def _():
                for c in weight_copies(i, f + 1):
                    c.start()

            @pl.when((f + 1 == MOE_NF) & next_item_valid())
            def _():
                for c in weight_copies(i + 1, 0):
                    c.start()

            g = i * MOE_NF + f
            gs = lax.rem(g, MOE_GU_SLOTS)
            ws = lax.rem(g, MOE_WD_SLOTS)
            first = jnp.asarray(f == 0, jnp.int32)

            def sub(s, c):
                act, r_prev, ws_prev, first_prev = c
                down(act, r_prev, ws_prev, first_prev)
                return gate_up(f, s * MOE_SUB, MOE_SUB, gs), s * MOE_SUB, ws, first

            carry = lax.fori_loop(0, n_sub, sub, carry)

            def tail(t, _):
                r0 = n_sub * MOE_SUB + t * MOE_BLOCK
                down(gate_up(f, r0, MOE_BLOCK, gs), r0, ws, first)
                return 0

            lax.fori_loop(0, n_tail, tail, 0)
            return carry

        assert MOE_ROWS % MOE_SUB == 0 and MOE_SUB % MOE_BLOCK == 0
        carry0 = (jnp.zeros((MOE_SUB, MOE_TF), BF16), jnp.int32(MOE_ROWS),
                  lax.rem(i * MOE_NF, MOE_WD_SLOTS), jnp.int32(0))
        down(*lax.fori_loop(0, MOE_NF, tile, carry0))

    def drain_pending_store():
        for_blocks(pend_ref[1], lambda b: out_copy(pend_ref[0], b).wait())
        pend_ref[1] = 0

    @pl.when(nb > 0)
    def _():
        def pack(b):
            r0 = pl.multiple_of(b * MOE_BLOCK, MOE_BLOCK)
            y = acc_ref[pl.ds(r0, MOE_BLOCK), :]
            pkout_ref[pl.ds(r0, MOE_BLOCK), :] = _pack_halves(y[:, :HALF], y[:, HALF:])

        drain_pending_store()
        for_blocks(nb, pack)
        for_blocks(nb, lambda b: out_copy(start, b).start())
        pend_ref[0] = start
        pend_ref[1] = nb

    @pl.when(i == MOE_ITEMS - 1)
    def _():
        drain_pending_store()
        pkout_ref[:MOE_BLOCK, :] = jnp.zeros((MOE_BLOCK, HALF), U32)

        def tail_copy(b):
            return pltpu.make_async_copy(
                pkout_ref.at[pl.ds(0, MOE_BLOCK)],
                ys_hbm.at[pl.ds(pl.multiple_of(b * MOE_BLOCK, MOE_BLOCK), MOE_BLOCK)], sem_out)

        def tail_blocks(fn):
            def body(b, _):
                fn(b)
                return 0
            lax.fori_loop(tail_ref[0], n_out_blocks, body, 0)

        tail_blocks(lambda b: tail_copy(b).start())
        tail_blocks(lambda b: tail_copy(b).wait())


def _moe_ffn(item_e, item_start, item_nb, tail_blk, xs, w_gu, b_gu, w_d, b_d, n_rows):
    by_expert = lambda i, ie, is_, nb, tail: (ie[i], 0, 0)
    grid_spec = pltpu.PrefetchScalarGridSpec(
        num_scalar_prefetch=4,
        grid=(MOE_ITEMS,),
        in_specs=[pl.BlockSpec(memory_space=pl.ANY),
                  pl.BlockSpec(memory_space=pl.ANY),
                  pl.BlockSpec(memory_space=pl.ANY),
                  pl.BlockSpec((None, 2 * MOE_NF, MOE_TF), by_expert),
                  pl.BlockSpec((None, 1, D_MODEL), by_expert)],
        out_specs=pl.BlockSpec(memory_space=pl.ANY),
        scratch_shapes=[pltpu.VMEM((MOE_ROWS, HALF), U32),
                        pltpu.VMEM((MOE_ROWS, HALF), U32),
                        pltpu.VMEM((MOE_ROWS, D_MODEL), BF16),
                        pltpu.VMEM((MOE_ROWS + MOE_SUB, D_MODEL), F32),
                        pltpu.VMEM((MOE_GU_SLOTS, D_MODEL, MOE_TF), F32),
                        pltpu.VMEM((MOE_GU_SLOTS, D_MODEL, MOE_TF), F32),
                        pltpu.VMEM((MOE_WD_SLOTS, MOE_TF, D_MODEL), F32),
                        pltpu.SMEM((2,), jnp.int32),
                        pltpu.SemaphoreType.DMA(()),
                        pltpu.SemaphoreType.DMA(()),
                        pltpu.SemaphoreType.DMA((MOE_GU_SLOTS + MOE_WD_SLOTS,))])
    return pl.pallas_call(
        _moe_kernel,
        grid_spec=grid_spec,
        out_shape=jax.ShapeDtypeStruct((n_rows, HALF), U32),
        compiler_params=_cparams(1),
        name="moe_ffn",
    )(item_e, item_start, item_nb, tail_blk, xs, w_gu, w_d, b_gu, b_d)


def _combine_kernel(y0_ref, y1_ref, y2_ref, y3_ref, gate_ref, h2_ref, g_ref, b_ref, o_ref):
    h2 = h2_ref[...]
    lo = DEEPNORM_ALPHA * h2[:, :HALF]
    hi = DEEPNORM_ALPHA * h2[:, HALF:]
    for k, y_ref in enumerate((y0_ref, y1_ref, y2_ref, y3_ref)):
        ylo, yhi = _unpack_halves(y_ref[...])
        gk = gate_ref[:, k:k + 1]
        lo = lo + gk * ylo
        hi = hi + gk * yhi
    mu = (jnp.sum(lo, -1, keepdims=True) + jnp.sum(hi, -1, keepdims=True)) * (1.0 / D_MODEL)
    lo = lo - mu
    hi = hi - mu
    var = (jnp.sum(lo * lo, -1, keepdims=True) + jnp.sum(hi * hi, -1, keepdims=True)) * (1.0 / D_MODEL)
    inv = lax.rsqrt(var + LN_EPS)
    o_ref[:, :HALF] = lo * inv * g_ref[:, :HALF] + b_ref[:, :HALF]
    o_ref[:, HALF:] = hi * inv * g_ref[:, HALF:] + b_ref[:, HALF:]


def _combine(yg, gates, h2, g, b, tm=256):
    t, d = h2.shape
    full = lambda a: pl.BlockSpec(a.shape, lambda i: (0,) * a.ndim)
    return pl.pallas_call(
        _combine_kernel,
        grid=(t // tm,),
        in_specs=[pl.BlockSpec((None, tm, HALF), functools.partial(lambda k, i: (k, i, 0), k))
                  for k in range(TOP_K)] + [
                  pl.BlockSpec((tm, LANES), lambda i: (i, 0)),
                  pl.BlockSpec((tm, d), lambda i: (i, 0)), full(g), full(b)],
        out_specs=pl.BlockSpec((tm, d), lambda i: (i, 0)),
        out_shape=jax.ShapeDtypeStruct((t, d), F32),
        compiler_params=_cparams(1),
        name="combine_ln",
    )(yg, yg, yg, yg, gates, h2, g, b)


def _routing_tables(sel, counts_f):
    t = sel.shape[1]
    idx = sel[:TOP_K].astype(jnp.int32)
    pos = sel[TOP_K:].astype(jnp.int32)
    counts = counts_f[0, :N_EXPERTS].astype(jnp.int32)
    padded = ((counts + MOE_BLOCK - 1) // MOE_BLOCK) * MOE_BLOCK
    pends = jnp.cumsum(padded)
    pstarts = pends - padded
    dest = (pstarts[idx] + pos).reshape(-1)
    n_rows = t * TOP_K + N_EXPERTS * MOE_BLOCK
    j = jnp.arange(MOE_BLOCK, dtype=jnp.int32)[None, :]
    is_pad = j < (padded - counts)[:, None]
    n_after = jnp.cumsum((~is_pad).reshape(-1).astype(jnp.int32)).reshape(is_pad.shape) - 1
    zero_rows = jnp.where(is_pad, (pstarts + counts)[:, None] + j, pends[-1] + n_after)
    row_of = jnp.concatenate([dest, zero_rows.reshape(-1)])

    nchunk = (padded + MOE_ROWS - 1) // MOE_ROWS
    cend = jnp.cumsum(nchunk)
    cstart = cend - nchunk
    total = cend[-1]
    it = jnp.arange(MOE_ITEMS, dtype=jnp.int32)
    valid = it < total
    it_c = jnp.minimum(it, total - 1)
    ie = jnp.searchsorted(cend, it_c, side="right").astype(jnp.int32)
    c = it_c - cstart[ie]
    istart = (pstarts[ie] + c * MOE_ROWS).astype(jnp.int32)
    inb = jnp.where(valid, jnp.minimum(MOE_ROWS, padded[ie] - c * MOE_ROWS) // MOE_BLOCK, 0)
    tail_blk = (pends[-1] // MOE_BLOCK).astype(jnp.int32).reshape(1)
    return dest, row_of, ie, istart, inb.astype(jnp.int32), tail_blk, n_rows


def kernel(x, positions, ln_in_g, ln_in_b, w_in, q_a_norm, w_q_b, kv_a_norm, w_kv_b, sb_out_norm,
           mla_out_norm, w_o, ln_mix_g, ln_mix_b, w_router, b_router, w_gate_up, b_gate_up, w_down,
           b_down, ln_ffn_g, ln_ffn_b):
    bsz, seq, d = x.shape
    t = bsz * seq
    x2 = x.reshape(t, d)
    row = lambda v: v.reshape(1, -1)

    w_in0 = w_in[0]
    w_sb = w_in0[:, :3 * SB_WIDTH].astype(BF16)
    w_lat = jnp.pad(w_in0[:, 3 * SB_WIDTH:], ((0, 0), (0, LANES - MLA_ROPE))).astype(BF16)
    wq = w_q_b[0].reshape(MLA_Q_RANK, MLA_HEADS, MLA_NOPE + MLA_ROPE)
    w_qn = wq[:, :, :MLA_NOPE].reshape(MLA_Q_RANK, -1).astype(BF16)
    w_qp = jnp.pad(wq[:, :, MLA_NOPE:], ((0, 0), (0, 0), (0, LANES - MLA_ROPE))
                   ).reshape(MLA_Q_RANK, -1).astype(BF16)
    wkv = w_kv_b[0].reshape(MLA_KV_RANK, MLA_HEADS, MLA_NOPE + MLA_V)
    w_kn = wkv[:, :, :MLA_NOPE].reshape(MLA_KV_RANK, -1).astype(BF16)
    w_v = wkv[:, :, MLA_NOPE:].reshape(MLA_KV_RANK, -1).astype(BF16)
    w_r32 = jnp.pad(w_router[0], ((0, 0), (0, LANES - N_EXPERTS)))
    w_r_hi = w_r32.astype(BF16)
    w_r = jnp.concatenate([w_r_hi, (w_r32 - w_r_hi.astype(F32)).astype(BF16)], axis=1)
    b_r = jnp.pad(b_router[0], (0, LANES - N_EXPERTS)).reshape(1, LANES)
    half = MLA_ROPE // 2
    inv_freq = ROPE_THETA ** (-(jnp.arange(half, dtype=F32) * 2.0 / MLA_ROPE))
    invf = jnp.concatenate([inv_freq, inv_freq, jnp.zeros((LANES - MLA_ROPE,), F32)]).reshape(1, LANES)
    pos_f = positions.reshape(t, 1).astype(F32)

    for l in range(DEPTH):
        u_sb, h_b = _ln_inproj(x2, row(ln_in_g), row(ln_in_b), w_sb)
        q_m, k_m, v_m = _mla_prep(h_b, pos_f, invf, w_lat, row(q_a_norm[l]), row(kv_a_norm[l]),
                                  w_qn, w_qp, w_kn, w_v)
        sb_o = _sb_attention(u_sb.reshape(bsz, seq, -1)).reshape(t, -1)
        mla_o = _mla_attention(q_m.reshape(bsz, seq, -1), k_m.reshape(bsz, seq, -1),
                               v_m.reshape(bsz, seq, -1)).reshape(t, -1)
        h2, h2p, sel, gates, counts_f = _mix_router(
            sb_o, mla_o, x2, row(ln_in_g), row(ln_in_b), row(sb_out_norm[l]), row(mla_out_norm[l]),
            w_o[l].astype(BF16), row(ln_mix_g[l]), row(ln_mix_b[l]), w_r, b_r)
        dest, row_of, ie, istart, inb, tail_blk, n_rows = _routing_tables(sel, counts_f)
        xs = _scatter_rows(h2p, TOP_K, row_of, n_rows)
        ys = _moe_ffn(ie, istart, inb, tail_blk, xs, w_gate_up[l],
                      b_gate_up[l].reshape(N_EXPERTS, 2 * MOE_NF, MOE_TF),
                      w_down[l], b_down[l].reshape(N_EXPERTS, 1, -1), n_rows)
        yg = _gather_rows(ys, dest).reshape(TOP_K, t, HALF)
        out = _combine(yg, gates, h2, row(ln_ffn_g[l]), row(ln_ffn_b[l]))
    return out.reshape(bsz, seq, d)
```

```python
import functools

import jax
import jax.numpy as jnp
from jax import lax
from jax.experimental import pallas as pl
from jax.experimental.pallas import tpu as pltpu

D_MODEL = 2048
DEPTH = 1
CHUNK = 64
SB_HEADS = 8
HEAD_DIM = 128
SB_WIDTH = SB_HEADS * HEAD_DIM
MLA_HEADS = 8
MLA_Q_RANK = 512
MLA_KV_RANK = 256
MLA_NOPE = 128
MLA_ROPE = 64
MLA_V = 128
MLA_WIDTH = MLA_HEADS * MLA_V
MLA_QK_PAD = 256
ROPE_THETA = 10000.0
N_EXPERTS = 32
TOP_K = 4
D_FF = 2048
SWIGLU_LIMIT = 7.0
SWIGLU_ALPHA = 1.702
MOE_BLOCK = 128
LN_EPS = 1e-5
RMS_EPS = 1e-6
DEEPNORM_ALPHA = (2 * DEPTH) ** 0.25

LANES = 128
HALF = D_MODEL // 2
VMEM_LIMIT_BYTES = 56 * 1024 * 1024

BF16 = jnp.bfloat16
F32 = jnp.float32
U32 = jnp.uint32


def _cparams(n_axes, vmem=None, flags=None):
    return pltpu.CompilerParams(dimension_semantics=("arbitrary",) * n_axes,
                                vmem_limit_bytes=vmem or VMEM_LIMIT_BYTES, flags=flags)


def _layer_norm(x, g, b):
    mu = jnp.mean(x, -1, keepdims=True)
    xc = x - mu
    var = jnp.mean(xc * xc, -1, keepdims=True)
    return xc * lax.rsqrt(var + LN_EPS) * g + b


def _rms_norm(x, g):
    ms = jnp.mean(x * x, -1, keepdims=True)
    return x * lax.rsqrt(ms + RMS_EPS) * g


def _pack_halves(lo_f32, hi_f32):
    lo = lax.bitcast_convert_type(lo_f32.astype(BF16).astype(F32), U32)
    hi = lax.bitcast_convert_type(hi_f32.astype(BF16).astype(F32), U32)
    return lax.shift_right_logical(lo, U32(16)) | (hi & U32(0xFFFF0000))


def _unpack_halves(w):
    lo = lax.bitcast_convert_type(lax.shift_left(w, U32(16)), F32)
    hi = lax.bitcast_convert_type(w & U32(0xFFFF0000), F32)
    return lo, hi


def _ln_inproj_kernel(x_ref, g_ref, b_ref, w_ref, u_ref, h_ref):
    h = _layer_norm(x_ref[...], g_ref[...], b_ref[...]).astype(BF16)
    h_ref[...] = h
    u_ref[...] = jnp.dot(h, w_ref[...], preferred_element_type=F32).astype(BF16)


def _ln_inproj(x2, g, b, w_sb, tm=512):
    t, d = x2.shape
    n = w_sb.shape[1]
    return pl.pallas_call(
        _ln_inproj_kernel,
        grid=(t // tm,),
        in_specs=[pl.BlockSpec((tm, d), lambda i: (i, 0)),
                  pl.BlockSpec((1, d), lambda i: (0, 0)),
                  pl.BlockSpec((1, d), lambda i: (0, 0)),
                  pl.BlockSpec((d, n), lambda i: (0, 0))],
        out_specs=[pl.BlockSpec((tm, n), lambda i: (i, 0)),
                   pl.BlockSpec((tm, d), lambda i: (i, 0))],
        out_shape=[jax.ShapeDtypeStruct((t, n), BF16), jax.ShapeDtypeStruct((t, d), BF16)],
        compiler_params=_cparams(1),
        name="ln_inproj",
    )(x2, g, b, w_sb)


def _rope_slab(x, cos, sin_neg, sin_pos):
    return (x * cos + pltpu.roll(x, LANES - MLA_ROPE // 2, 1) * sin_neg
            + pltpu.roll(x, MLA_ROPE // 2, 1) * sin_pos)


def _mla_prep_kernel(h_ref, pos_ref, invf_ref, wl_ref, qg_ref, kg_ref, wqn_ref, wqp_ref,
                     wkn_ref, wv_ref, q_ref, k_ref, v_ref):
    lat = jnp.dot(h_ref[...], wl_ref[...], preferred_element_type=F32)
    qn = _rms_norm(lat[:, :MLA_Q_RANK], qg_ref[...]).astype(BF16)
    kvn = _rms_norm(lat[:, MLA_Q_RANK:MLA_Q_RANK + MLA_KV_RANK], kg_ref[...]).astype(BF16)
    k_rope = lat[:, MLA_Q_RANK + MLA_KV_RANK:]

    ang = pos_ref[...] * invf_ref[...]
    cos = jnp.cos(ang)
    sin = jnp.sin(ang)
    lane = lax.broadcasted_iota(jnp.int32, ang.shape, 1)
    sin_neg = jnp.where(lane < MLA_ROPE // 2, -sin, 0.0)
    sin_pos = jnp.where((lane >= MLA_ROPE // 2) & (lane < MLA_ROPE), sin, 0.0)

    q_nope = jnp.dot(qn, wqn_ref[...], preferred_element_type=F32)
    q_pe = jnp.dot(qn, wqp_ref[...], preferred_element_type=F32)
    k_nope = jnp.dot(kvn, wkn_ref[...], preferred_element_type=F32)
    v_ref[...] = jnp.dot(kvn, wv_ref[...], preferred_element_type=F32).astype(BF16)
    k_pe = _rope_slab(k_rope, cos, sin_neg, sin_pos).astype(BF16)
    for hd in range(MLA_HEADS):
        c0 = hd * MLA_QK_PAD
        s0 = hd * LANES
        q_ref[:, c0:c0 + LANES] = q_nope[:, s0:s0 + LANES].astype(BF16)
        q_ref[:, c0 + LANES:c0 + 2 * LANES] = _rope_slab(
            q_pe[:, s0:s0 + LANES], cos, sin_neg, sin_pos).astype(BF16)
        k_ref[:, c0:c0 + LANES] = k_nope[:, s0:s0 + LANES].astype(BF16)
        k_ref[:, c0 + LANES:c0 + 2 * LANES] = k_pe


def _mla_prep(h_b, pos_f, invf, w_lat, qg, kg, w_qn, w_qp, w_kn, w_v, tm=512):
    t, d = h_b.shape
    full = lambda a: pl.BlockSpec(a.shape, lambda i: (0,) * a.ndim)
    wq = MLA_HEADS * MLA_QK_PAD
    return pl.pallas_call(
        _mla_prep_kernel,
        grid=(t // tm,),
        in_specs=[pl.BlockSpec((tm, d), lambda i: (i, 0)),
                  pl.BlockSpec((tm, 1), lambda i: (i, 0)),
                  full(invf), full(w_lat), full(qg), full(kg), full(w_qn), full(w_qp),
                  full(w_kn), full(w_v)],
        out_specs=[pl.BlockSpec((tm, wq), lambda i: (i, 0)),
                   pl.BlockSpec((tm, wq), lambda i: (i, 0)),
                   pl.BlockSpec((tm, MLA_WIDTH), lambda i: (i, 0))],
        out_shape=[jax.ShapeDtypeStruct((t, wq), BF16), jax.ShapeDtypeStruct((t, wq), BF16),
                   jax.ShapeDtypeStruct((t, MLA_WIDTH), BF16)],
        compiler_params=_cparams(1),
        name="mla_prep",
    )(h_b, pos_f, invf, w_lat, qg, kg, w_qn, w_qp, w_kn, w_v)


def _sb_attn_kernel(q_ref, k_ref, v_ref, o_ref, *, tq, tk, heads, scale):
    i = pl.program_id(2)
    t_idx = i * tq + lax.broadcasted_iota(jnp.int32, (tq, 1), 0)
    r_i = lax.broadcasted_iota(jnp.int32, (tk, tk), 0)
    c_i = lax.broadcasted_iota(jnp.int32, (tk, tk), 1)
    suffix = (r_i >= c_i).astype(BF16)
    suffix2 = jnp.concatenate([suffix, suffix], axis=0)
    nkb = (i + 1) * (tq // tk)

    def block(j, hd, c, acc, diagonal):
        cols = slice(hd * HEAD_DIM, (hd + 1) * HEAD_DIM)
        k0 = pl.multiple_of(j * tk, tk)
        ks = k_ref[pl.ds(k0, tk), cols]
        vs = v_ref[pl.ds(k0, tk), cols]
        z = lax.dot_general(q_ref[:, cols], ks, (((1,), (1,)), ((), ())),
                            preferred_element_type=F32) * scale
        sp = jnp.maximum(z, 0.0) + jnp.log(1.0 + jnp.exp(-jnp.abs(z)))
        if diagonal:
            past = (k0 + lax.broadcasted_iota(jnp.int32, (1, tk), 1)) < t_idx
            sp = jnp.where(past, sp, 0.0)
        hi = sp.astype(BF16)
        lo = (sp - hi.astype(F32)).astype(BF16)
        incl = jnp.dot(jnp.concatenate([hi, lo], axis=1), suffix2, preferred_element_type=F32)
        w = jnp.exp(z - incl - c)
        if diagonal:
            w = jnp.where(past, w, 0.0)
        acc = acc + jnp.dot(w.astype(BF16), vs, preferred_element_type=F32)
        return c + incl[:, 0:1], acc

    bpt = tq // tk

    def trip(p, carry, diagonal):
        out = []
        for hd in range(heads):
            c, acc = carry[hd]
            for bk in range(bpt):
                c, acc = block(nkb - 1 - bpt * p - bk, hd, c, acc, diagonal)
            out.append((c, acc))
        return tuple(out)

    init = tuple((jnp.zeros((tq, 1), F32), jnp.zeros((tq, HEAD_DIM), F32)) for _ in range(heads))
    res = trip(0, init, True)
    res = lax.fori_loop(1, i + 1, functools.partial(trip, diagonal=False), res)
    for hd in range(heads):
        o_ref[:, hd * HEAD_DIM:(hd + 1) * HEAD_DIM] = res[hd][1].astype(o_ref.dtype)


def _sb_attention(u_sb3, tq=256, tk=128, heads=SB_HEADS):
    b, s, _ = u_sb3.shape
    assert tq % tk == 0 and SB_HEADS % heads == 0
    groups = SB_HEADS // heads
    w = heads * HEAD_DIM
    kern = functools.partial(_sb_attn_kernel, tq=tq, tk=tk, heads=heads, scale=HEAD_DIM ** -0.5)
    return pl.pallas_call(
        kern,
        grid=(b, groups, s // tq),
        in_specs=[pl.BlockSpec((None, tq, w), lambda bb, g, i: (bb, i, g)),
                  pl.BlockSpec((None, s, w), lambda bb, g, i: (bb, 0, groups + g)),
                  pl.BlockSpec((None, s, w), lambda bb, g, i: (bb, 0, 2 * groups + g))],
        out_specs=pl.BlockSpec((None, tq, w), lambda bb, g, i: (bb, i, g)),
        out_shape=jax.ShapeDtypeStruct((b, s, SB_WIDTH), BF16),
        compiler_params=_cparams(3),
        name="sb_attention",
    )(u_sb3, u_sb3, u_sb3)


def _mla_attn_kernel(q_ref, k_ref, v_ref, o_ref, *, tq, tk, heads, scale):
    i = pl.program_id(2)
    t_chunk = lax.shift_right_logical(
        i * tq + lax.broadcasted_iota(jnp.int32, (tq, 1), 0), CHUNK.bit_length() - 1)

    def body(j, carry):
        k0 = pl.multiple_of(j * tk, tk)
        s_chunk = lax.shift_right_logical(
            k0 + lax.broadcasted_iota(jnp.int32, (1, tk), 1), CHUNK.bit_length() - 1)
        visible = s_chunk <= t_chunk
        def qk_scores(hd):
            qk = slice(hd * MLA_QK_PAD, (hd + 1) * MLA_QK_PAD)
            return lax.dot_general(q_ref[:, qk], k_ref[pl.ds(k0, tk), qk],
                                   (((1,), (1,)), ((), ())), preferred_element_type=F32)

        out = []
        nxt = qk_scores(0)
        for hd in range(heads):
            m, l, acc = carry[hd]
            s = nxt
            if hd + 1 < heads:
                nxt = qk_scores(hd + 1)
            s = jnp.where(visible, s * scale, -1e30)
            m_new = jnp.maximum(m, jnp.max(s, -1, keepdims=True))
            alpha = jnp.exp(m - m_new)
            p = jnp.exp(s - m_new)
            l = alpha * l + jnp.sum(p, -1, keepdims=True)
            vs = v_ref[pl.ds(k0, tk), hd * MLA_V:(hd + 1) * MLA_V]
            acc = alpha * acc + jnp.dot(p.astype(BF16), vs, preferred_element_type=F32)
            out.append((m_new, l, acc))
        return tuple(out)

    init = tuple((jnp.full((tq, 1), -1e30, F32), jnp.zeros((tq, 1), F32),
                  jnp.zeros((tq, MLA_V), F32)) for _ in range(heads))
    res = lax.fori_loop(0, (i + 1) * (tq // tk), body, init)
    for hd in range(heads):
        _, l, acc = res[hd]
        o_ref[:, hd * MLA_V:(hd + 1) * MLA_V] = (acc / l).astype(o_ref.dtype)


def _mla_attention(q3, k3, v3, tq=256, tk=256, heads=MLA_HEADS):
    b, s, _ = q3.shape
    assert MLA_HEADS % heads == 0 and tq % tk == 0 and tk % CHUNK == 0
    kern = functools.partial(_mla_attn_kernel, tq=tq, tk=tk, heads=heads,
                             scale=(MLA_NOPE + MLA_ROPE) ** -0.5)
    return pl.pallas_call(
        kern,
        grid=(b, MLA_HEADS // heads, s // tq),
        in_specs=[pl.BlockSpec((None, tq, heads * MLA_QK_PAD), lambda bb, g, i: (bb, i, g)),
                  pl.BlockSpec((None, s, heads * MLA_QK_PAD), lambda bb, g, i: (bb, 0, g)),
                  pl.BlockSpec((None, s, heads * MLA_V), lambda bb, g, i: (bb, 0, g))],
        out_specs=pl.BlockSpec((None, tq, heads * MLA_V), lambda bb, g, i: (bb, i, g)),
        out_shape=jax.ShapeDtypeStruct((b, s, MLA_WIDTH), BF16),
        compiler_params=_cparams(3),
        name="mla_attention",
    )(q3, k3, v3)


def _mix_router_kernel(sb_ref, mla_ref, x_ref, lig_ref, lib_ref, sg_ref, mg_ref, wo_ref,
                       lmg_ref, lmb_ref, wr_ref, br_ref,
                       h2_ref, h2p_ref, sel_ref, gate_ref, cnt_ref, run_ref, *, tm):
    step = pl.program_id(0)

    @pl.when(step == 0)
    def _():
        run_ref[...] = jnp.zeros_like(run_ref)

    a = _rms_norm(sb_ref[...].astype(F32), sg_ref[...]).astype(BF16)
    bm = _rms_norm(mla_ref[...].astype(F32), mg_ref[...]).astype(BF16)
    mix = (jnp.dot(a, wo_ref[:SB_WIDTH, :], preferred_element_type=F32)
           + jnp.dot(bm, wo_ref[SB_WIDTH:, :], preferred_element_type=F32))
    h = _layer_norm(x_ref[...], lig_ref[...], lib_ref[...])
    h2 = _layer_norm(DEEPNORM_ALPHA * h + mix, lmg_ref[...], lmb_ref[...])
    h2_ref[...] = h2
    h2p_ref[...] = _pack_halves(h2[:, :HALF], h2[:, HALF:])

    h2_hi = h2.astype(BF16)
    h2_lo = (h2 - h2_hi.astype(F32)).astype(BF16)
    hw = jnp.dot(h2_hi, wr_ref[...], preferred_element_type=F32)
    logits = (hw[:, :LANES] + hw[:, LANES:]
              + jnp.dot(h2_lo, wr_ref[:, :LANES], preferred_element_type=F32) + br_ref[...])
    lane = lax.broadcasted_iota(jnp.int32, (tm, LANES), 1)
    lane_f = lane.astype(F32)
    logits = jnp.where(lane < N_EXPERTS, logits, -jnp.inf)
    vals, hots = [], []
    sel = jnp.zeros((tm, LANES), F32)
    for k in range(TOP_K):
        m = jnp.max(logits, -1, keepdims=True)
        idx = jnp.min(jnp.where(logits == m, lane_f, float(LANES)), -1, keepdims=True)
        hot = lane_f == idx
        logits = jnp.where(hot, -jnp.inf, logits)
        vals.append(m)
        hots.append(hot)
        sel = jnp.where(lane == k, idx, sel)
    exps = [jnp.exp(v - vals[0]) for v in vals]
    denom = exps[0] + exps[1] + exps[2] + exps[3]
    gates = jnp.zeros((tm, LANES), F32)
    for k in range(TOP_K):
        gates = jnp.where(lane == k, exps[k] / denom, gates)
    gate_ref[...] = gates

    onehot = (hots[0] | hots[1] | hots[2] | hots[3]).astype(F32)
    r_i = lax.broadcasted_iota(jnp.int32, (tm, tm), 0)
    c_i = lax.broadcasted_iota(jnp.int32, (tm, tm), 1)
    before = (c_i < r_i).astype(BF16)
    rank = jnp.dot(before, onehot.astype(BF16), preferred_element_type=F32) + run_ref[...]
    for k in range(TOP_K):
        pos = jnp.sum(jnp.where(hots[k], rank, 0.0), -1, keepdims=True)
        sel = jnp.where(lane == TOP_K + k, pos, sel)
    sel_ref[...] = sel.T[:2 * TOP_K, :]
    run_ref[...] = run_ref[...] + jnp.sum(onehot, 0, keepdims=True)
    cnt_ref[...] = run_ref[...]


def _mix_router(sb_o, mla_o, x2, lig, lib, sg, mg, w_o, lmg, lmb, w_r, b_r, tm=512):
    t, d = x2.shape
    full = lambda a: pl.BlockSpec(a.shape, lambda i: (0,) * a.ndim)
    row = lambda w: pl.BlockSpec((tm, w), lambda i: (i, 0))
    return pl.pallas_call(
        functools.partial(_mix_router_kernel, tm=tm),
        grid=(t // tm,),
        in_specs=[row(SB_WIDTH), row(MLA_WIDTH), row(d), full(lig), full(lib), full(sg), full(mg),
                  full(w_o), full(lmg), full(lmb), full(w_r), full(b_r)],
        out_specs=[row(d), row(HALF), pl.BlockSpec((2 * TOP_K, tm), lambda i: (0, i)), row(LANES),
                   pl.BlockSpec((1, LANES), lambda i: (0, 0))],
        out_shape=[jax.ShapeDtypeStruct((t, d), F32), jax.ShapeDtypeStruct((t, HALF), U32),
                   jax.ShapeDtypeStruct((2 * TOP_K, t), F32), jax.ShapeDtypeStruct((t, LANES), F32),
                   jax.ShapeDtypeStruct((1, LANES), F32)],
        scratch_shapes=[pltpu.VMEM((1, LANES), F32)],
        compiler_params=_cparams(1),
        name="mix_router",
    )(sb_o, mla_o, x2, lig, lib, sg, mg, w_o, lmg, lmb, w_r, b_r)


ROWS_PER_STEP = 2048


def _wait_rows(like_hbm, sem, rows):
    pltpu.make_async_copy(like_hbm.at[pl.ds(0, rows)], like_hbm.at[pl.ds(0, rows)], sem).wait()


def _gather_rows_kernel(idx_ref, src_hbm, out_ref, sem, *, rows):
    def issue(g, _):
        r0 = pl.multiple_of(g * 8, 8)
        for k in range(8):
            pltpu.make_async_copy(src_hbm.at[pl.ds(idx_ref[0, 0, r0 + k], 1)],
                                  out_ref.at[pl.ds(r0 + k, 1)], sem).start()
        return 0

    lax.fori_loop(0, rows // 8, issue, 0)
    _wait_rows(src_hbm, sem, rows)


def _scatter_rows_kernel(idx_ref, src_ref, out_hbm, zero_ref, sem, *, rows, src_steps):
    step = pl.program_id(0)

    @pl.when(step == 0)
    def _():
        zero_ref[...] = jnp.zeros_like(zero_ref)

    @pl.when(step < src_steps)
    def _():
        def issue(g, _):
            r0 = pl.multiple_of(g * 8, 8)
            for k in range(8):
                pltpu.make_async_copy(src_ref.at[pl.ds(r0 + k, 1)],
                                      out_hbm.at[pl.ds(idx_ref[0, 0, r0 + k], 1)], sem).start()
            return 0
        lax.fori_loop(0, rows // 8, issue, 0)

    @pl.when(step >= src_steps)
    def _():
        def issue(a, _):
            pltpu.make_async_copy(zero_ref.at[pl.ds(0, 1)],
                                  out_hbm.at[pl.ds(idx_ref[0, 0, a], 1)], sem).start()
            return 0
        lax.fori_loop(0, rows, issue, 0, unroll=8)

    _wait_rows(out_hbm, sem, rows)


def _gather_rows(src, idx):
    rows = ROWS_PER_STEP
    n, w = idx.shape[0], src.shape[1]
    assert n % rows == 0 and src.shape[0] >= rows
    return pl.pallas_call(
        functools.partial(_gather_rows_kernel, rows=rows),
        grid=(n // rows,),
        in_specs=[pl.BlockSpec((1, 1, rows), lambda i: (i, 0, 0), memory_space=pltpu.SMEM),
                  pl.BlockSpec(memory_space=pl.ANY)],
        out_specs=pl.BlockSpec((rows, w), lambda i: (i, 0)),
        out_shape=jax.ShapeDtypeStruct((n, w), src.dtype),
        scratch_shapes=[pltpu.SemaphoreType.DMA(())],
        compiler_params=_cparams(1),
        name="gather_rows",
    )(idx.reshape(n // rows, 1, rows), src)


def _scatter_rows(src, passes, idx, out_rows):
    rows = ROWS_PER_STEP
    n, w = idx.shape[0], src.shape[1]
    src_blocks = src.shape[0] // rows
    assert src.shape[0] % rows == 0 and n % rows == 0 and n == out_rows
    src_steps = passes * src_blocks
    kern = functools.partial(_scatter_rows_kernel, rows=rows, src_steps=src_steps)

    def src_block(i):
        return jnp.minimum(i // passes, src_blocks - 1), 0

    def idx_block(i):
        return jnp.where(i < src_steps, lax.rem(i, passes) * src_blocks + i // passes, i), 0, 0

    return pl.pallas_call(
        kern,
        grid=(n // rows,),
        in_specs=[pl.BlockSpec((1, 1, rows), idx_block, memory_space=pltpu.SMEM),
                  pl.BlockSpec((rows, w), src_block)],
        out_specs=pl.BlockSpec(memory_space=pl.ANY),
        out_shape=jax.ShapeDtypeStruct((out_rows, w), src.dtype),
        scratch_shapes=[pltpu.VMEM((8, w), src.dtype), pltpu.SemaphoreType.DMA(())],
        compiler_params=_cparams(1),
        name="scatter_rows",
    )(idx.reshape(n // rows, 1, rows), src)


MOE_ROWS = 1536
MOE_SUB = 512
MOE_TF = 256
MOE_NF = D_FF // MOE_TF
MOE_ITEMS = N_EXPERTS + (N_EXPERTS * (MOE_BLOCK - 1) + 8192 * TOP_K) // MOE_ROWS + 1


MOE_GU_SLOTS = 2
MOE_WD_SLOTS = 3


def _moe_kernel(ie_ref, is_ref, nb_ref, tail_ref, xs_hbm, wgu_hbm, wd_hbm, bgu_ref, bd_ref,
                ys_hbm, pkin_ref, pkout_ref, x_ref, acc_ref, wg_buf, wu_buf, wd_buf,
                pend_ref, sem_in, sem_out, sem_w):
    i = pl.program_id(0)
    nb = nb_ref[i]
    start = is_ref[i]
    n_out_blocks = ys_hbm.shape[0] // MOE_BLOCK

    def weight_copies(item, f):
        e = ie_ref[item]
        g = item * MOE_NF + f
        gs = lax.rem(g, MOE_GU_SLOTS)
        ws = lax.rem(g, MOE_WD_SLOTS)
        c0 = pl.multiple_of(f * MOE_TF, MOE_TF)
        return (pltpu.make_async_copy(wgu_hbm.at[e, :, pl.ds(c0, MOE_TF)], wg_buf.at[gs],
                                      sem_w.at[gs]),
                pltpu.make_async_copy(wgu_hbm.at[e, :, pl.ds(D_FF + c0, MOE_TF)], wu_buf.at[gs],
                                      sem_w.at[gs]),
                pltpu.make_async_copy(wd_hbm.at[e, pl.ds(c0, MOE_TF), :], wd_buf.at[ws],
                                      sem_w.at[MOE_GU_SLOTS + ws]))

    def vm_block(ref, b):
        return ref.at[pl.ds(pl.multiple_of(b * MOE_BLOCK, MOE_BLOCK), MOE_BLOCK)]

    def hbm_block(ref, first_row, b):
        return ref.at[pl.ds(pl.multiple_of(first_row + b * MOE_BLOCK, MOE_BLOCK), MOE_BLOCK)]

    def in_copy(first_row, b):
        return pltpu.make_async_copy(hbm_block(xs_hbm, first_row, b), vm_block(pkin_ref, b), sem_in)

    def out_copy(first_row, b):
        return pltpu.make_async_copy(vm_block(pkout_ref, b), hbm_block(ys_hbm, first_row, b), sem_out)

    def for_blocks(n, fn):
        def body(b, _):
            fn(b)
            return 0
        lax.fori_loop(0, n, body, 0)

    def next_item_valid():
        return (i + 1 < MOE_ITEMS) & (nb_ref[jnp.minimum(i + 1, MOE_ITEMS - 1)] > 0)

    @pl.when(i == 0)
    def _():
        x_ref[...] = jnp.zeros_like(x_ref)
        acc_ref[...] = jnp.zeros_like(acc_ref)
        pend_ref[0] = 0
        pend_ref[1] = 0
        for_blocks(nb, lambda b: in_copy(start, b).start())
        for c in weight_copies(0, 0):
            c.start()

    @pl.when(nb > 0)
    def _():
        def unpack(b):
            r0 = pl.multiple_of(b * MOE_BLOCK, MOE_BLOCK)
            lo, hi = _unpack_halves(pkin_ref[pl.ds(r0, MOE_BLOCK), :])
            x_ref[pl.ds(r0, MOE_BLOCK), :HALF] = lo.astype(BF16)
            x_ref[pl.ds(r0, MOE_BLOCK), HALF:] = hi.astype(BF16)

        for_blocks(nb, lambda b: in_copy(start, b).wait())
        for_blocks(nb, unpack)

        @pl.when(next_item_valid())
        def _():
            nxt = jnp.minimum(i + 1, MOE_ITEMS - 1)
            for_blocks(nb_ref[nxt], lambda b: in_copy(is_ref[nxt], b).start())

        n_sub = (nb * MOE_BLOCK) // MOE_SUB
        n_tail = nb - n_sub * (MOE_SUB // MOE_BLOCK)

        def gate_up(f, r0, rows, gs):
            x = x_ref[pl.ds(pl.multiple_of(r0, MOE_BLOCK), rows), :]
            g = (jnp.dot(x, wg_buf[gs].astype(BF16), preferred_element_type=F32)
                 + bgu_ref[pl.ds(f, 1), :])
            u = (jnp.dot(x, wu_buf[gs].astype(BF16), preferred_element_type=F32)
                 + bgu_ref[pl.ds(MOE_NF + f, 1), :])
            g = jnp.minimum(g, SWIGLU_LIMIT)
            u = jnp.clip(u, -SWIGLU_LIMIT, SWIGLU_LIMIT)
            return ((u + 1.0) * (g * jax.nn.sigmoid(SWIGLU_ALPHA * g))).astype(BF16)

        def down(act, r0, ws, first):
            rows = act.shape[0]
            r0 = pl.multiple_of(r0, MOE_BLOCK)
            y = jnp.dot(act, wd_buf[ws].astype(BF16), preferred_element_type=F32)
            prev = jnp.where(first == 1, bd_ref[...], acc_ref[pl.ds(r0, rows), :])
            acc_ref[pl.ds(r0, rows), :] = prev + y

        def tile(f, carry):
            for c in weight_copies(i, f):
                c.wait()

            @pl.when(f + 1 < MOE_NF)
            def _():
                for c in weight_copies(i, f + 1):
                    c.start()

            @pl.when((f + 1 == MOE_NF) & next_item_valid())
            def _():
                for c in weight_copies(i + 1, 0):
                    c.start()

            g = i * MOE_NF + f
            gs = lax.rem(g, MOE_GU_SLOTS)
            ws = lax.rem(g, MOE_WD_SLOTS)
            first = jnp.asarray(f == 0, jnp.int32)

            def sub(s, c):
                act, r_prev, ws_prev, first_prev = c
                down(act, r_prev, ws_prev, first_prev)
                return gate_up(f, s * MOE_SUB, MOE_SUB, gs), s * MOE_SUB, ws, first

            carry = lax.fori_loop(0, n_sub, sub, carry)

            def tail(t, _):
                r0 = n_sub * MOE_SUB + t * MOE_BLOCK
                down(gate_up(f, r0, MOE_BLOCK, gs), r0, ws, first)
                return 0

            lax.fori_loop(0, n_tail, tail, 0)
            return carry

        assert MOE_ROWS % MOE_SUB == 0 and MOE_SUB % MOE_BLOCK == 0
        carry0 = (jnp.zeros((MOE_SUB, MOE_TF), BF16), jnp.int32(MOE_ROWS),
                  lax.rem(i * MOE_NF, MOE_WD_SLOTS), jnp.int32(0))
        down(*lax.fori_loop(0, MOE_NF, tile, carry0))

    def drain_pending_store():
        for_blocks(pend_ref[1], lambda b: out_copy(pend_ref[0], b).wait())
        pend_ref[1] = 0

    @pl.when(nb > 0)
    def _():
        def pack(b):
            r0 = pl.multiple_of(b * MOE_BLOCK, MOE_BLOCK)
            y = acc_ref[pl.ds(r0, MOE_BLOCK), :]
            pkout_ref[pl.ds(r0, MOE_BLOCK), :] = _pack_halves(y[:, :HALF], y[:, HALF:])

        drain_pending_store()
        for_blocks(nb, pack)
        for_blocks(nb, lambda b: out_copy(start, b).start())
        pend_ref[0] = start
        pend_ref[1] = nb

    @pl.when(i == MOE_ITEMS - 1)
    def _():
        drain_pending_store()
        pkout_ref[:MOE_BLOCK, :] = jnp.zeros((MOE_BLOCK, HALF), U32)

        def tail_copy(b):
            return pltpu.make_async_copy(
                pkout_ref.at[pl.ds(0, MOE_BLOCK)],
                ys_hbm.at[pl.ds(pl.multiple_of(b * MOE_BLOCK, MOE_BLOCK), MOE_BLOCK)], sem_out)

        def tail_blocks(fn):
            def body(b, _):
                fn(b)
                return 0
            lax.fori_loop(tail_ref[0], n_out_blocks, body, 0)

        tail_blocks(lambda b: tail_copy(b).start())
        tail_blocks(lambda b: tail_copy(b).wait())


def _moe_ffn(item_e, item_start, item_nb, tail_blk, xs, w_gu, b_gu, w_d, b_d, n_rows):
    by_expert = lambda i, ie, is_, nb, tail: (ie[i], 0, 0)
    grid_spec = pltpu.PrefetchScalarGridSpec(
        num_scalar_prefetch=4,
        grid=(MOE_ITEMS,),
        in_specs=[pl.BlockSpec(memory_space=pl.ANY),
                  pl.BlockSpec(memory_space=pl.ANY),
                  pl.BlockSpec(memory_space=pl.ANY),
                  pl.BlockSpec((None, 2 * MOE_NF, MOE_TF), by_expert),
                  pl.BlockSpec((None, 1, D_MODEL), by_expert)],
        out_specs=pl.BlockSpec(memory_space=pl.ANY),
        scratch_shapes=[pltpu.VMEM((MOE_ROWS, HALF), U32),
                        pltpu.VMEM((MOE_ROWS, HALF), U32),
                        pltpu.VMEM((MOE_ROWS, D_MODEL), BF16),
                        pltpu.VMEM((MOE_ROWS + MOE_SUB, D_MODEL), F32),
                        pltpu.VMEM((MOE_GU_SLOTS, D_MODEL, MOE_TF), F32),
                        pltpu.VMEM((MOE_GU_SLOTS, D_MODEL, MOE_TF), F32),
                        pltpu.VMEM((MOE_WD_SLOTS, MOE_TF, D_MODEL), F32),
                        pltpu.SMEM((2,), jnp.int32),
                        pltpu.SemaphoreType.DMA(()),
                        pltpu.SemaphoreType.DMA(()),
                        pltpu.SemaphoreType.DMA((MOE_GU_SLOTS + MOE_WD_SLOTS,))])
    return pl.pallas_call(
        _moe_kernel,
        grid_spec=grid_spec,
        out_shape=jax.ShapeDtypeStruct((n_rows, HALF), U32),
        compiler_params=_cparams(1),
        name="moe_ffn",
    )(item_e, item_start, item_nb, tail_blk, xs, w_gu, w_d, b_gu, b_d)


def _combine_kernel(y0_ref, y1_ref, y2_ref, y3_ref, gate_ref, h2_ref, g_ref, b_ref, o_ref):
    h2 = h2_ref[...]
    lo = DEEPNORM_ALPHA * h2[:, :HALF]
    hi = DEEPNORM_ALPHA * h2[:, HALF:]
    for k, y_ref in enumerate((y0_ref, y1_ref, y2_ref, y3_ref)):
        ylo, yhi = _unpack_halves(y_ref[...])
        gk = gate_ref[:, k:k + 1]
        lo = lo + gk * ylo
        hi = hi + gk * yhi
    mu = (jnp.sum(lo, -1, keepdims=True) + jnp.sum(hi, -1, keepdims=True)) * (1.0 / D_MODEL)
    lo = lo - mu
    hi = hi - mu
    var = (jnp.sum(lo * lo, -1, keepdims=True) + jnp.sum(hi * hi, -1, keepdims=True)) * (1.0 / D_MODEL)
    inv = lax.rsqrt(var + LN_EPS)
    o_ref[:, :HALF] = lo * inv * g_ref[:, :HALF] + b_ref[:, :HALF]
    o_ref[:, HALF:] = hi * inv * g_ref[:, HALF:] + b_ref[:, HALF:]


def _combine(yg, gates, h2, g, b, tm=256):
    t, d = h2.shape
    full = lambda a: pl.BlockSpec(a.shape, lambda i: (0,) * a.ndim)
    return pl.pallas_call(
        _combine_kernel,
        grid=(t // tm,),
        in_specs=[pl.BlockSpec((None, tm, HALF), functools.partial(lambda k, i: (k, i, 0), k))
                  for k in range(TOP_K)] + [
                  pl.BlockSpec((tm, LANES), lambda i: (i, 0)),
                  pl.BlockSpec((tm, d), lambda i: (i, 0)), full(g), full(b)],
        out_specs=pl.BlockSpec((tm, d), lambda i: (i, 0)),
        out_shape=jax.ShapeDtypeStruct((t, d), F32),
        compiler_params=_cparams(1),
        name="combine_ln",
    )(yg, yg, yg, yg, gates, h2, g, b)


def _routing_tables(sel, counts_f):
    t = sel.shape[1]
    idx = sel[:TOP_K].astype(jnp.int32)
    pos = sel[TOP_K:].astype(jnp.int32)
    counts = counts_f[0, :N_EXPERTS].astype(jnp.int32)
    padded = ((counts + MOE_BLOCK - 1) // MOE_BLOCK) * MOE_BLOCK
    pends = jnp.cumsum(padded)
    pstarts = pends - padded
    experts = jnp.arange(N_EXPERTS, dtype=jnp.int32)[:, None, None]
    first_row = jnp.sum(jnp.where(idx[None] == experts, pstarts[:, None, None], 0), axis=0)
    dest = (first_row + pos).reshape(-1)
    n_rows = t * TOP_K + N_EXPERTS * MOE_BLOCK
    j = jnp.arange(MOE_BLOCK, dtype=jnp.int32)[None, :]
    is_pad = j < (padded - counts)[:, None]
    n_after = jnp.cumsum((~is_pad).reshape(-1).astype(jnp.int32)).reshape(is_pad.shape) - 1
    zero_rows = jnp.where(is_pad, (pstarts + counts)[:, None] + j, pends[-1] + n_after)
    row_of = jnp.concatenate([dest, zero_rows.reshape(-1)])

    nchunk = (padded + MOE_ROWS - 1) // MOE_ROWS
    cend = jnp.cumsum(nchunk)
    cstart = cend - nchunk
    total = cend[-1]
    it = jnp.arange(MOE_ITEMS, dtype=jnp.int32)
    valid = it < total
    it_c = jnp.minimum(it, total - 1)
    ie = jnp.searchsorted(cend, it_c, side="right").astype(jnp.int32)
    c = it_c - cstart[ie]
    istart = (pstarts[ie] + c * MOE_ROWS).astype(jnp.int32)
    inb = jnp.where(valid, jnp.minimum(MOE_ROWS, padded[ie] - c * MOE_ROWS) // MOE_BLOCK, 0)
    tail_blk = (pends[-1] // MOE_BLOCK).astype(jnp.int32).reshape(1)
    return dest, row_of, ie, istart, inb.astype(jnp.int32), tail_blk, n_rows


def kernel(x, positions, ln_in_g, ln_in_b, w_in, q_a_norm, w_q_b, kv_a_norm, w_kv_b, sb_out_norm,
           mla_out_norm, w_o, ln_mix_g, ln_mix_b, w_router, b_router, w_gate_up, b_gate_up, w_down,
           b_down, ln_ffn_g, ln_ffn_b):
    bsz, seq, d = x.shape
    t = bsz * seq
    x2 = x.reshape(t, d)
    row = lambda v: v.reshape(1, -1)

    w_in0 = w_in[0]
    w_sb = w_in0[:, :3 * SB_WIDTH].astype(BF16)
    w_lat = jnp.pad(w_in0[:, 3 * SB_WIDTH:], ((0, 0), (0, LANES - MLA_ROPE))).astype(BF16)
    wq = w_q_b[0].reshape(MLA_Q_RANK, MLA_HEADS, MLA_NOPE + MLA_ROPE)
    w_qn = wq[:, :, :MLA_NOPE].reshape(MLA_Q_RANK, -1).astype(BF16)
    w_qp = jnp.pad(wq[:, :, MLA_NOPE:], ((0, 0), (0, 0), (0, LANES - MLA_ROPE))
                   ).reshape(MLA_Q_RANK, -1).astype(BF16)
    wkv = w_kv_b[0].reshape(MLA_KV_RANK, MLA_HEADS, MLA_NOPE + MLA_V)
    w_kn = wkv[:, :, :MLA_NOPE].reshape(MLA_KV_RANK, -1).astype(BF16)
    w_v = wkv[:, :, MLA_NOPE:].reshape(MLA_KV_RANK, -1).astype(BF16)
    w_r32 = jnp.pad(w_router[0], ((0, 0), (0, LANES - N_EXPERTS)))
    w_r_hi = w_r32.astype(BF16)
    w_r = jnp.concatenate([w_r_hi, (w_r32 - w_r_hi.astype(F32)).astype(BF16)], axis=1)
    b_r = jnp.pad(b_router[0], (0, LANES - N_EXPERTS)).reshape(1, LANES)
    half = MLA_ROPE // 2
    inv_freq = ROPE_THETA ** (-(jnp.arange(half, dtype=F32) * 2.0 / MLA_ROPE))
    invf = jnp.concatenate([inv_freq, inv_freq, jnp.zeros((LANES - MLA_ROPE,), F32)]).reshape(1, LANES)
    pos_f = positions.reshape(t, 1).astype(F32)

    for l in range(DEPTH):
        u_sb, h_b = _ln_inproj(x2, row(ln_in_g), row(ln_in_b), w_sb)
        q_m, k_m, v_m = _mla_prep(h_b, pos_f, invf, w_lat, row(q_a_norm[l]), row(kv_a_norm[l]),
                                  w_qn, w_qp, w_kn, w_v)
        sb_o = _sb_attention(u_sb.reshape(bsz, seq, -1)).reshape(t, -1)
        mla_o = _mla_attention(q_m.reshape(bsz, seq, -1), k_m.reshape(bsz, seq, -1),
                               v_m.reshape(bsz, seq, -1)).reshape(t, -1)
        h2, h2p, sel, gates, counts_f = _mix_router(
            sb_o, mla_o, x2, row(ln_in_g), row(ln_in_b), row(sb_out_norm[l]), row(mla_out_norm[l]),
            w_o[l].astype(BF16), row(ln_mix_g[l]), row(ln_mix_b[l]), w_r, b_r)
        dest, row_of, ie, istart, inb, tail_blk, n_rows = _routing_tables(sel, counts_f)
        xs = _scatter_rows(h2p, TOP_K, row_of, n_rows)
        ys = _moe_ffn(ie, istart, inb, tail_blk, xs, w_gate_up[l],
                      b_gate_up[l].reshape(N_EXPERTS, 2 * MOE_NF, MOE_TF),
                      w_down[l], b_down[l].reshape(N_EXPERTS, 1, -1), n_rows)
        yg = _gather_rows(ys, dest).reshape(TOP_K, t, HALF)
        out = _combine(yg, gates, h2, row(ln_ffn_g[l]), row(ln_ffn_b[l]))
    return out.reshape(bsz, seq, d)
```

```python
import functools

import jax
import jax.numpy as jnp
from jax import lax
from jax.experimental import pallas as pl
from jax.experimental.pallas import tpu as pltpu

D_MODEL = 2048
DEPTH = 1
CHUNK = 64
SB_HEADS = 8
HEAD_DIM = 128
SB_WIDTH = SB_HEADS * HEAD_DIM
MLA_HEADS = 8
MLA_Q_RANK = 512
MLA_KV_RANK = 256
MLA_NOPE = 128
MLA_ROPE = 64
MLA_V = 128
MLA_WIDTH = MLA_HEADS * MLA_V
MLA_QK_PAD = 256
ROPE_THETA = 10000.0
N_EXPERTS = 32
TOP_K = 4
D_FF = 2048
SWIGLU_LIMIT = 7.0
SWIGLU_ALPHA = 1.702
MOE_BLOCK = 128
LN_EPS = 1e-5
RMS_EPS = 1e-6
DEEPNORM_ALPHA = (2 * DEPTH) ** 0.25

LANES = 128
HALF = D_MODEL // 2
VMEM_LIMIT_BYTES = 56 * 1024 * 1024

BF16 = jnp.bfloat16
F32 = jnp.float32
U32 = jnp.uint32


def _cparams(n_axes, vmem=None, flags=None):
    return pltpu.CompilerParams(dimension_semantics=("arbitrary",) * n_axes,
                                vmem_limit_bytes=vmem or VMEM_LIMIT_BYTES, flags=flags)


def _layer_norm(x, g, b):
    mu = jnp.mean(x, -1, keepdims=True)
    xc = x - mu
    var = jnp.mean(xc * xc, -1, keepdims=True)
    return xc * lax.rsqrt(var + LN_EPS) * g + b


def _rms_norm(x, g):
    ms = jnp.mean(x * x, -1, keepdims=True)
    return x * lax.rsqrt(ms + RMS_EPS) * g


def _pack_halves(lo_f32, hi_f32):
    lo = lax.bitcast_convert_type(lo_f32.astype(BF16).astype(F32), U32)
    hi = lax.bitcast_convert_type(hi_f32.astype(BF16).astype(F32), U32)
    return lax.shift_right_logical(lo, U32(16)) | (hi & U32(0xFFFF0000))


def _unpack_halves(w):
    lo = lax.bitcast_convert_type(lax.shift_left(w, U32(16)), F32)
    hi = lax.bitcast_convert_type(w & U32(0xFFFF0000), F32)
    return lo, hi


def _ln_inproj_kernel(x_ref, g_ref, b_ref, w_ref, u_ref, h_ref):
    h = _layer_norm(x_ref[...], g_ref[...], b_ref[...]).astype(BF16)
    h_ref[...] = h
    u_ref[...] = jnp.dot(h, w_ref[...], preferred_element_type=F32).astype(BF16)


def _ln_inproj(x2, g, b, w_sb, tm=512):
    t, d = x2.shape
    n = w_sb.shape[1]
    return pl.pallas_call(
        _ln_inproj_kernel,
        grid=(t // tm,),
        in_specs=[pl.BlockSpec((tm, d), lambda i: (i, 0)),
                  pl.BlockSpec((1, d), lambda i: (0, 0)),
                  pl.BlockSpec((1, d), lambda i: (0, 0)),
                  pl.BlockSpec((d, n), lambda i: (0, 0))],
        out_specs=[pl.BlockSpec((tm, n), lambda i: (i, 0)),
                   pl.BlockSpec((tm, d), lambda i: (i, 0))],
        out_shape=[jax.ShapeDtypeStruct((t, n), BF16), jax.ShapeDtypeStruct((t, d), BF16)],
        compiler_params=_cparams(1),
        name="ln_inproj",
    )(x2, g, b, w_sb)


def _rope_slab(x, cos, sin_neg, sin_pos):
    return (x * cos + pltpu.roll(x, LANES - MLA_ROPE // 2, 1) * sin_neg
            + pltpu.roll(x, MLA_ROPE // 2, 1) * sin_pos)


def _mla_prep_kernel(h_ref, pos_ref, invf_ref, wl_ref, qg_ref, kg_ref, wqn_ref, wqp_ref,
                     wkn_ref, wv_ref, q_ref, k_ref, v_ref):
    lat = jnp.dot(h_ref[...], wl_ref[...], preferred_element_type=F32)
    qn = _rms_norm(lat[:, :MLA_Q_RANK], qg_ref[...]).astype(BF16)
    kvn = _rms_norm(lat[:, MLA_Q_RANK:MLA_Q_RANK + MLA_KV_RANK], kg_ref[...]).astype(BF16)
    k_rope = lat[:, MLA_Q_RANK + MLA_KV_RANK:]

    ang = pos_ref[...] * invf_ref[...]
    cos = jnp.cos(ang)
    sin = jnp.sin(ang)
    lane = lax.broadcasted_iota(jnp.int32, ang.shape, 1)
    sin_neg = jnp.where(lane < MLA_ROPE // 2, -sin, 0.0)
    sin_pos = jnp.where((lane >= MLA_ROPE // 2) & (lane < MLA_ROPE), sin, 0.0)

    q_nope = jnp.dot(qn, wqn_ref[...], preferred_element_type=F32)
    q_pe = jnp.dot(qn, wqp_ref[...], preferred_element_type=F32)
    k_nope = jnp.dot(kvn, wkn_ref[...], preferred_element_type=F32)
    v_ref[...] = jnp.dot(kvn, wv_ref[...], preferred_element_type=F32).astype(BF16)
    k_pe = _rope_slab(k_rope, cos, sin_neg, sin_pos).astype(BF16)
    for hd in range(MLA_HEADS):
        c0 = hd * MLA_QK_PAD
        s0 = hd * LANES
        q_ref[:, c0:c0 + LANES] = q_nope[:, s0:s0 + LANES].astype(BF16)
        q_ref[:, c0 + LANES:c0 + 2 * LANES] = _rope_slab(
            q_pe[:, s0:s0 + LANES], cos, sin_neg, sin_pos).astype(BF16)
        k_ref[:, c0:c0 + LANES] = k_nope[:, s0:s0 + LANES].astype(BF16)
        k_ref[:, c0 + LANES:c0 + 2 * LANES] = k_pe


def _mla_prep(h_b, pos_f, invf, w_lat, qg, kg, w_qn, w_qp, w_kn, w_v, tm=512):
    t, d = h_b.shape
    full = lambda a: pl.BlockSpec(a.shape, lambda i: (0,) * a.ndim)
    wq = MLA_HEADS * MLA_QK_PAD
    return pl.pallas_call(
        _mla_prep_kernel,
        grid=(t // tm,),
        in_specs=[pl.BlockSpec((tm, d), lambda i: (i, 0)),
                  pl.BlockSpec((tm, 1), lambda i: (i, 0)),
                  full(invf), full(w_lat), full(qg), full(kg), full(w_qn), full(w_qp),
                  full(w_kn), full(w_v)],
        out_specs=[pl.BlockSpec((tm, wq), lambda i: (i, 0)),
                   pl.BlockSpec((tm, wq), lambda i: (i, 0)),
                   pl.BlockSpec((tm, MLA_WIDTH), lambda i: (i, 0))],
        out_shape=[jax.ShapeDtypeStruct((t, wq), BF16), jax.ShapeDtypeStruct((t, wq), BF16),
                   jax.ShapeDtypeStruct((t, MLA_WIDTH), BF16)],
        compiler_params=_cparams(1),
        name="mla_prep",
    )(h_b, pos_f, invf, w_lat, qg, kg, w_qn, w_qp, w_kn, w_v)


def _sb_attn_kernel(q_ref, k_ref, v_ref, o_ref, *, tq, tk, heads, scale):
    i = pl.program_id(2)
    t_idx = i * tq + lax.broadcasted_iota(jnp.int32, (tq, 1), 0)
    r_i = lax.broadcasted_iota(jnp.int32, (tk, tk), 0)
    c_i = lax.broadcasted_iota(jnp.int32, (tk, tk), 1)
    suffix = (r_i >= c_i).astype(BF16)
    suffix2 = jnp.concatenate([suffix, suffix], axis=0)
    nkb = (i + 1) * (tq // tk)

    def key_block(j, carry, diagonal):
        k0 = pl.multiple_of(j * tk, tk)
        if diagonal:
            past = (k0 + lax.broadcasted_iota(jnp.int32, (1, tk), 1)) < t_idx
        zs, splits = [], []
        for hd in range(heads):
            cols = slice(hd * HEAD_DIM, (hd + 1) * HEAD_DIM)
            z = lax.dot_general(q_ref[:, cols], k_ref[pl.ds(k0, tk), cols], (((1,), (1,)), ((), ())),
                                preferred_element_type=F32) * scale
            sp = jnp.maximum(z, 0.0) + jnp.log(1.0 + jnp.exp(-jnp.abs(z)))
            if diagonal:
                sp = jnp.where(past, sp, 0.0)
            hi = sp.astype(BF16)
            lo = (sp - hi.astype(F32)).astype(BF16)
            zs.append(z)
            splits.append(jnp.concatenate([hi, lo], axis=1))
        incl_all = jnp.dot(jnp.concatenate(splits, axis=0), suffix2, preferred_element_type=F32)
        out = []
        for hd in range(heads):
            c, acc = carry[hd]
            cols = slice(hd * HEAD_DIM, (hd + 1) * HEAD_DIM)
            incl = incl_all[hd * tq:(hd + 1) * tq]
            w = jnp.exp(zs[hd] - incl - c)
            if diagonal:
                w = jnp.where(past, w, 0.0)
            acc = acc + jnp.dot(w.astype(BF16), v_ref[pl.ds(k0, tk), cols],
                                preferred_element_type=F32)
            out.append((c + incl[:, 0:1], acc))
        return tuple(out)

    bpt = tq // tk

    def trip(p, carry, diagonal):
        for bk in range(bpt):
            carry = key_block(nkb - 1 - bpt * p - bk, carry, diagonal)
        return carry

    init = tuple((jnp.zeros((tq, 1), F32), jnp.zeros((tq, HEAD_DIM), F32)) for _ in range(heads))
    res = trip(0, init, True)
    res = lax.fori_loop(1, i + 1, functools.partial(trip, diagonal=False), res)
    for hd in range(heads):
        o_ref[:, hd * HEAD_DIM:(hd + 1) * HEAD_DIM] = res[hd][1].astype(o_ref.dtype)


def _sb_attention(u_sb3, tq=256, tk=128, heads=SB_HEADS):
    b, s, _ = u_sb3.shape
    assert tq % tk == 0 and SB_HEADS % heads == 0
    groups = SB_HEADS // heads
    w = heads * HEAD_DIM
    kern = functools.partial(_sb_attn_kernel, tq=tq, tk=tk, heads=heads, scale=HEAD_DIM ** -0.5)
    return pl.pallas_call(
        kern,
        grid=(b, groups, s // tq),
        in_specs=[pl.BlockSpec((None, tq, w), lambda bb, g, i: (bb, i, g)),
                  pl.BlockSpec((None, s, w), lambda bb, g, i: (bb, 0, groups + g)),
                  pl.BlockSpec((None, s, w), lambda bb, g, i: (bb, 0, 2 * groups + g))],
        out_specs=pl.BlockSpec((None, tq, w), lambda bb, g, i: (bb, i, g)),
        out_shape=jax.ShapeDtypeStruct((b, s, SB_WIDTH), BF16),
        compiler_params=_cparams(3),
        name="sb_attention",
    )(u_sb3, u_sb3, u_sb3)


def _mla_attn_kernel(q_ref, k_ref, v_ref, o_ref, *, tq, tk, heads, scale):
    i = pl.program_id(2)
    t_chunk = lax.shift_right_logical(
        i * tq + lax.broadcasted_iota(jnp.int32, (tq, 1), 0), CHUNK.bit_length() - 1)

    def body(j, carry):
        k0 = pl.multiple_of(j * tk, tk)
        s_chunk = lax.shift_right_logical(
            k0 + lax.broadcasted_iota(jnp.int32, (1, tk), 1), CHUNK.bit_length() - 1)
        visible = s_chunk <= t_chunk
        def qk_scores(hd):
            qk = slice(hd * MLA_QK_PAD, (hd + 1) * MLA_QK_PAD)
            return lax.dot_general(q_ref[:, qk], k_ref[pl.ds(k0, tk), qk],
                                   (((1,), (1,)), ((), ())), preferred_element_type=F32)

        out = []
        nxt = qk_scores(0)
        for hd in range(heads):
            m, l, acc = carry[hd]
            s = nxt
            if hd + 1 < heads:
                nxt = qk_scores(hd + 1)
            s = jnp.where(visible, s * scale, -1e30)
            m_new = jnp.maximum(m, jnp.max(s, -1, keepdims=True))
            alpha = jnp.exp(m - m_new)
            p = jnp.exp(s - m_new)
            l = alpha * l + jnp.sum(p, -1, keepdims=True)
            vs = v_ref[pl.ds(k0, tk), hd * MLA_V:(hd + 1) * MLA_V]
            acc = alpha * acc + jnp.dot(p.astype(BF16), vs, preferred_element_type=F32)
            out.append((m_new, l, acc))
        return tuple(out)

    init = tuple((jnp.full((tq, 1), -1e30, F32), jnp.zeros((tq, 1), F32),
                  jnp.zeros((tq, MLA_V), F32)) for _ in range(heads))
    res = lax.fori_loop(0, (i + 1) * (tq // tk), body, init)
    for hd in range(heads):
        _, l, acc = res[hd]
        o_ref[:, hd * MLA_V:(hd + 1) * MLA_V] = (acc / l).astype(o_ref.dtype)


def _mla_attention(q3, k3, v3, tq=256, tk=256, heads=MLA_HEADS):
    b, s, _ = q3.shape
    assert MLA_HEADS % heads == 0 and tq % tk == 0 and tk % CHUNK == 0
    kern = functools.partial(_mla_attn_kernel, tq=tq, tk=tk, heads=heads,
                             scale=(MLA_NOPE + MLA_ROPE) ** -0.5)
    return pl.pallas_call(
        kern,
        grid=(b, MLA_HEADS // heads, s // tq),
        in_specs=[pl.BlockSpec((None, tq, heads * MLA_QK_PAD), lambda bb, g, i: (bb, i, g)),
                  pl.BlockSpec((None, s, heads * MLA_QK_PAD), lambda bb, g, i: (bb, 0, g)),
                  pl.BlockSpec((None, s, heads * MLA_V), lambda bb, g, i: (bb, 0, g))],
        out_specs=pl.BlockSpec((None, tq, heads * MLA_V), lambda bb, g, i: (bb, i, g)),
        out_shape=jax.ShapeDtypeStruct((b, s, MLA_WIDTH), BF16),
        compiler_params=_cparams(3),
        name="mla_attention",
    )(q3, k3, v3)


def _mix_router_kernel(sb_ref, mla_ref, x_ref, lig_ref, lib_ref, sg_ref, mg_ref, wo_ref,
                       lmg_ref, lmb_ref, wr_ref, br_ref,
                       h2_ref, h2p_ref, sel_ref, gate_ref, cnt_ref, run_ref, *, tm):
    step = pl.program_id(0)

    @pl.when(step == 0)
    def _():
        run_ref[...] = jnp.zeros_like(run_ref)

    a = _rms_norm(sb_ref[...].astype(F32), sg_ref[...]).astype(BF16)
    bm = _rms_norm(mla_ref[...].astype(F32), mg_ref[...]).astype(BF16)
    mix = (jnp.dot(a, wo_ref[:SB_WIDTH, :], preferred_element_type=F32)
           + jnp.dot(bm, wo_ref[SB_WIDTH:, :], preferred_element_type=F32))
    h = _layer_norm(x_ref[...], lig_ref[...], lib_ref[...])
    h2 = _layer_norm(DEEPNORM_ALPHA * h + mix, lmg_ref[...], lmb_ref[...])
    h2_ref[...] = h2
    h2p_ref[...] = _pack_halves(h2[:, :HALF], h2[:, HALF:])

    h2_hi = h2.astype(BF16)
    h2_lo = (h2 - h2_hi.astype(F32)).astype(BF16)
    hw = jnp.dot(h2_hi, wr_ref[...], preferred_element_type=F32)
    logits = (hw[:, :LANES] + hw[:, LANES:]
              + jnp.dot(h2_lo, wr_ref[:, :LANES], preferred_element_type=F32) + br_ref[...])
    lane = lax.broadcasted_iota(jnp.int32, (tm, LANES), 1)
    lane_f = lane.astype(F32)
    logits = jnp.where(lane < N_EXPERTS, logits, -jnp.inf)
    vals, hots = [], []
    sel = jnp.zeros((tm, LANES), F32)
    for k in range(TOP_K):
        m = jnp.max(logits, -1, keepdims=True)
        idx = jnp.min(jnp.where(logits == m, lane_f, float(LANES)), -1, keepdims=True)
        hot = lane_f == idx
        logits = jnp.where(hot, -jnp.inf, logits)
        vals.append(m)
        hots.append(hot)
        sel = jnp.where(lane == k, idx, sel)
    exps = [jnp.exp(v - vals[0]) for v in vals]
    denom = exps[0] + exps[1] + exps[2] + exps[3]
    gates = jnp.zeros((tm, LANES), F32)
    for k in range(TOP_K):
        gates = jnp.where(lane == k, exps[k] / denom, gates)
    gate_ref[...] = gates

    onehot = (hots[0] | hots[1] | hots[2] | hots[3]).astype(F32)
    r_i = lax.broadcasted_iota(jnp.int32, (tm, tm), 0)
    c_i = lax.broadcasted_iota(jnp.int32, (tm, tm), 1)
    before = (c_i < r_i).astype(BF16)
    rank = jnp.dot(before, onehot.astype(BF16), preferred_element_type=F32) + run_ref[...]
    for k in range(TOP_K):
        pos = jnp.sum(jnp.where(hots[k], rank, 0.0), -1, keepdims=True)
        sel = jnp.where(lane == TOP_K + k, pos, sel)
    sel_ref[...] = sel.T[:2 * TOP_K, :]
    run_ref[...] = run_ref[...] + jnp.sum(onehot, 0, keepdims=True)
    cnt_ref[...] = run_ref[...]


def _mix_router(sb_o, mla_o, x2, lig, lib, sg, mg, w_o, lmg, lmb, w_r, b_r, tm=512):
    t, d = x2.shape
    full = lambda a: pl.BlockSpec(a.shape, lambda i: (0,) * a.ndim)
    row = lambda w: pl.BlockSpec((tm, w), lambda i: (i, 0))
    return pl.pallas_call(
        functools.partial(_mix_router_kernel, tm=tm),
        grid=(t // tm,),
        in_specs=[row(SB_WIDTH), row(MLA_WIDTH), row(d), full(lig), full(lib), full(sg), full(mg),
                  full(w_o), full(lmg), full(lmb), full(w_r), full(b_r)],
        out_specs=[row(d), row(HALF), pl.BlockSpec((2 * TOP_K, tm), lambda i: (0, i)), row(LANES),
                   pl.BlockSpec((1, LANES), lambda i: (0, 0))],
        out_shape=[jax.ShapeDtypeStruct((t, d), F32), jax.ShapeDtypeStruct((t, HALF), U32),
                   jax.ShapeDtypeStruct((2 * TOP_K, t), F32), jax.ShapeDtypeStruct((t, LANES), F32),
                   jax.ShapeDtypeStruct((1, LANES), F32)],
        scratch_shapes=[pltpu.VMEM((1, LANES), F32)],
        compiler_params=_cparams(1),
        name="mix_router",
    )(sb_o, mla_o, x2, lig, lib, sg, mg, w_o, lmg, lmb, w_r, b_r)


ROWS_PER_STEP = 2048


def _wait_rows(like_hbm, sem, rows):
    pltpu.make_async_copy(like_hbm.at[pl.ds(0, rows)], like_hbm.at[pl.ds(0, rows)], sem).wait()


def _gather_rows_kernel(idx_ref, src_hbm, out_ref, sem, *, rows):
    def issue(g, _):
        r0 = pl.multiple_of(g * 8, 8)
        for k in range(8):
            pltpu.make_async_copy(src_hbm.at[pl.ds(idx_ref[0, 0, r0 + k], 1)],
                                  out_ref.at[pl.ds(r0 + k, 1)], sem).start()
        return 0

    lax.fori_loop(0, rows // 8, issue, 0)
    _wait_rows(src_hbm, sem, rows)


def _scatter_rows_kernel(idx_ref, src_ref, out_hbm, zero_ref, sem, *, rows, src_steps):
    step = pl.program_id(0)

    @pl.when(step == 0)
    def _():
        zero_ref[...] = jnp.zeros_like(zero_ref)

    @pl.when(step < src_steps)
    def _():
        def issue(g, _):
            r0 = pl.multiple_of(g * 8, 8)
            for k in range(8):
                pltpu.make_async_copy(src_ref.at[pl.ds(r0 + k, 1)],
                                      out_hbm.at[pl.ds(idx_ref[0, 0, r0 + k], 1)], sem).start()
            return 0
        lax.fori_loop(0, rows // 8, issue, 0)

    @pl.when(step >= src_steps)
    def _():
        def issue(a, _):
            pltpu.make_async_copy(zero_ref.at[pl.ds(0, 1)],
                                  out_hbm.at[pl.ds(idx_ref[0, 0, a], 1)], sem).start()
            return 0
        lax.fori_loop(0, rows, issue, 0, unroll=8)

    _wait_rows(out_hbm, sem, rows)


def _gather_rows(src, idx):
    rows = ROWS_PER_STEP
    n, w = idx.shape[0], src.shape[1]
    assert n % rows == 0 and src.shape[0] >= rows
    return pl.pallas_call(
        functools.partial(_gather_rows_kernel, rows=rows),
        grid=(n // rows,),
        in_specs=[pl.BlockSpec((1, 1, rows), lambda i: (i, 0, 0), memory_space=pltpu.SMEM),
                  pl.BlockSpec(memory_space=pl.ANY)],
        out_specs=pl.BlockSpec((rows, w), lambda i: (i, 0)),
        out_shape=jax.ShapeDtypeStruct((n, w), src.dtype),
        scratch_shapes=[pltpu.SemaphoreType.DMA(())],
        compiler_params=_cparams(1),
        name="gather_rows",
    )(idx.reshape(n // rows, 1, rows), src)


def _scatter_rows(src, passes, idx, out_rows):
    rows = ROWS_PER_STEP
    n, w = idx.shape[0], src.shape[1]
    src_blocks = src.shape[0] // rows
    assert src.shape[0] % rows == 0 and n % rows == 0 and n == out_rows
    src_steps = passes * src_blocks
    kern = functools.partial(_scatter_rows_kernel, rows=rows, src_steps=src_steps)

    def src_block(i):
        return jnp.minimum(i // passes, src_blocks - 1), 0

    def idx_block(i):
        return jnp.where(i < src_steps, lax.rem(i, passes) * src_blocks + i // passes, i), 0, 0

    return pl.pallas_call(
        kern,
        grid=(n // rows,),
        in_specs=[pl.BlockSpec((1, 1, rows), idx_block, memory_space=pltpu.SMEM),
                  pl.BlockSpec((rows, w), src_block)],
        out_specs=pl.BlockSpec(memory_space=pl.ANY),
        out_shape=jax.ShapeDtypeStruct((out_rows, w), src.dtype),
        scratch_shapes=[pltpu.VMEM((8, w), src.dtype), pltpu.SemaphoreType.DMA(())],
        compiler_params=_cparams(1),
        name="scatter_rows",
    )(idx.reshape(n // rows, 1, rows), src)


MOE_ROWS = 1536
MOE_SUB = 512
MOE_TF = 256
MOE_NF = D_FF // MOE_TF
MOE_ITEMS = N_EXPERTS + (N_EXPERTS * (MOE_BLOCK - 1) + 8192 * TOP_K) // MOE_ROWS + 1


MOE_GU_SLOTS = 2
MOE_WD_SLOTS = 3


def _moe_kernel(ie_ref, is_ref, nb_ref, tail_ref, xs_hbm, wgu_hbm, wd_hbm, bgu_ref, bd_ref,
                ys_hbm, pkin_ref, pkout_ref, x_ref, acc_ref, wg_buf, wu_buf, wd_buf,
                pend_ref, sem_in, sem_out, sem_w):
    i = pl.program_id(0)
    nb = nb_ref[i]
    start = is_ref[i]
    n_out_blocks = ys_hbm.shape[0] // MOE_BLOCK

    def weight_copies(item, f):
        e = ie_ref[item]
        g = item * MOE_NF + f
        gs = lax.rem(g, MOE_GU_SLOTS)
        ws = lax.rem(g, MOE_WD_SLOTS)
        c0 = pl.multiple_of(f * MOE_TF, MOE_TF)
        return (pltpu.make_async_copy(wgu_hbm.at[e, :, pl.ds(c0, MOE_TF)], wg_buf.at[gs],
                                      sem_w.at[gs]),
                pltpu.make_async_copy(wgu_hbm.at[e, :, pl.ds(D_FF + c0, MOE_TF)], wu_buf.at[gs],
                                      sem_w.at[gs]),
                pltpu.make_async_copy(wd_hbm.at[e, pl.ds(c0, MOE_TF), :], wd_buf.at[ws],
                                      sem_w.at[MOE_GU_SLOTS + ws]))

    def vm_block(ref, b):
        return ref.at[pl.ds(pl.multiple_of(b * MOE_BLOCK, MOE_BLOCK), MOE_BLOCK)]

    def hbm_block(ref, first_row, b):
        return ref.at[pl.ds(pl.multiple_of(first_row + b * MOE_BLOCK, MOE_BLOCK), MOE_BLOCK)]

    def in_copy(first_row, b):
        return pltpu.make_async_copy(hbm_block(xs_hbm, first_row, b), vm_block(pkin_ref, b), sem_in)

    def out_copy(first_row, b):
        return pltpu.make_async_copy(vm_block(pkout_ref, b), hbm_block(ys_hbm, first_row, b), sem_out)

    def for_blocks(n, fn):
        def body(b, _):
            fn(b)
            return 0
        lax.fori_loop(0, n, body, 0)

    def next_item_valid():
        return (i + 1 < MOE_ITEMS) & (nb_ref[jnp.minimum(i + 1, MOE_ITEMS - 1)] > 0)

    @pl.when(i == 0)
    def _():
        x_ref[...] = jnp.zeros_like(x_ref)
        acc_ref[...] = jnp.zeros_like(acc_ref)
        pend_ref[0] = 0
        pend_ref[1] = 0
        for_blocks(nb, lambda b: in_copy(start, b).start())
        for c in weight_copies(0, 0):
            c.start()

    @pl.when(nb > 0)
    def _():
        def unpack(b):
            r0 = pl.multiple_of(b * MOE_BLOCK, MOE_BLOCK)
            lo, hi = _unpack_halves(pkin_ref[pl.ds(r0, MOE_BLOCK), :])
            x_ref[pl.ds(r0, MOE_BLOCK), :HALF] = lo.astype(BF16)
            x_ref[pl.ds(r0, MOE_BLOCK), HALF:] = hi.astype(BF16)

        for_blocks(nb, lambda b: in_copy(start, b).wait())
        for_blocks(nb, unpack)

        @pl.when(next_item_valid())
        def _():
            nxt = jnp.minimum(i + 1, MOE_ITEMS - 1)
            for_blocks(nb_ref[nxt], lambda b: in_copy(is_ref[nxt], b).start())

        n_sub = (nb * MOE_BLOCK) // MOE_SUB
        n_tail = nb - n_sub * (MOE_SUB // MOE_BLOCK)

        def gate_up(f, r0, rows, gs):
            x = x_ref[pl.ds(pl.multiple_of(r0, MOE_BLOCK), rows), :]
            g = (jnp.dot(x, wg_buf[gs].astype(BF16), preferred_element_type=F32)
                 + bgu_ref[pl.ds(f, 1), :])
            u = (jnp.dot(x, wu_buf[gs].astype(BF16), preferred_element_type=F32)
                 + bgu_ref[pl.ds(MOE_NF + f, 1), :])
            g = jnp.minimum(g, SWIGLU_LIMIT)
            u = jnp.clip(u, -SWIGLU_LIMIT, SWIGLU_LIMIT)
            return ((u + 1.0) * (g * jax.nn.sigmoid(SWIGLU_ALPHA * g))).astype(BF16)

        def down(act, r0, ws, first):
            rows = act.shape[0]
            r0 = pl.multiple_of(r0, MOE_BLOCK)
            y = jnp.dot(act, wd_buf[ws].astype(BF16), preferred_element_type=F32)
            prev = jnp.where(first == 1, bd_ref[...], acc_ref[pl.ds(r0, rows), :])
            acc_ref[pl.ds(r0, rows), :] = prev + y

        def tile(f, carry):
            for c in weight_copies(i, f):
                c.wait()

            @pl.when(f + 1 < MOE_NF)
            def _():
                for c in weight_copies(i, f + 1):
                    c.start()

            @pl.when((f + 1 == MOE_NF) & next_item_valid())
            def _():
                for c in weight_copies(i + 1, 0):
                    c.start()

            g = i * MOE_NF + f
            gs = lax.rem(g, MOE_GU_SLOTS)
            ws = lax.rem(g, MOE_WD_SLOTS)
            first = jnp.asarray(f == 0, jnp.int32)

            def sub(s, c):
                act, r_prev, ws_prev, first_prev = c
                down(act, r_prev, ws_prev, first_prev)
                return gate_up(f, s * MOE_SUB, MOE_SUB, gs), s * MOE_SUB, ws, first

            carry = lax.fori_loop(0, n_sub, sub, carry)

            def tail(t, _):
                r0 = n_sub * MOE_SUB + t * MOE_BLOCK
                down(gate_up(f, r0, MOE_BLOCK, gs), r0, ws, first)
                return 0

            lax.fori_loop(0, n_tail, tail, 0)
            return carry

        assert MOE_ROWS % MOE_SUB == 0 and MOE_SUB % MOE_BLOCK == 0
        carry0 = (jnp.zeros((MOE_SUB, MOE_TF), BF16), jnp.int32(MOE_ROWS),
                  lax.rem(i * MOE_NF, MOE_WD_SLOTS), jnp.int32(0))
        down(*lax.fori_loop(0, MOE_NF, tile, carry0))

    def drain_pending_store():
        for_blocks(pend_ref[1], lambda b: out_copy(pend_ref[0], b).wait())
        pend_ref[1] = 0

    @pl.when(nb > 0)
    def _():
        def pack(b):
            r0 = pl.multiple_of(b * MOE_BLOCK, MOE_BLOCK)
            y = acc_ref[pl.ds(r0, MOE_BLOCK), :]
            pkout_ref[pl.ds(r0, MOE_BLOCK), :] = _pack_halves(y[:, :HALF], y[:, HALF:])

        drain_pending_store()
        for_blocks(nb, pack)
        for_blocks(nb, lambda b: out_copy(start, b).start())
        pend_ref[0] = start
        pend_ref[1] = nb

    @pl.when(i == MOE_ITEMS - 1)
    def _():
        drain_pending_store()
        pkout_ref[:MOE_BLOCK, :] = jnp.zeros((MOE_BLOCK, HALF), U32)

        def tail_copy(b):
            return pltpu.make_async_copy(
                pkout_ref.at[pl.ds(0, MOE_BLOCK)],
                ys_hbm.at[pl.ds(pl.multiple_of(b * MOE_BLOCK, MOE_BLOCK), MOE_BLOCK)], sem_out)

        def tail_blocks(fn):
            def body(b, _):
                fn(b)
                return 0
            lax.fori_loop(tail_ref[0], n_out_blocks, body, 0)

        tail_blocks(lambda b: tail_copy(b).start())
        tail_blocks(lambda b: tail_copy(b).wait())


def _moe_ffn(item_e, item_start, item_nb, tail_blk, xs, w_gu, b_gu, w_d, b_d, n_rows):
    by_expert = lambda i, ie, is_, nb, tail: (ie[i], 0, 0)
    grid_spec = pltpu.PrefetchScalarGridSpec(
        num_scalar_prefetch=4,
        grid=(MOE_ITEMS,),
        in_specs=[pl.BlockSpec(memory_space=pl.ANY),
                  pl.BlockSpec(memory_space=pl.ANY),
                  pl.BlockSpec(memory_space=pl.ANY),
                  pl.BlockSpec((None, 2 * MOE_NF, MOE_TF), by_expert),
                  pl.BlockSpec((None, 1, D_MODEL), by_expert)],
        out_specs=pl.BlockSpec(memory_space=pl.ANY),
        scratch_shapes=[pltpu.VMEM((MOE_ROWS, HALF), U32),
                        pltpu.VMEM((MOE_ROWS, HALF), U32),
                        pltpu.VMEM((MOE_ROWS, D_MODEL), BF16),
                        pltpu.VMEM((MOE_ROWS + MOE_SUB, D_MODEL), F32),
                        pltpu.VMEM((MOE_GU_SLOTS, D_MODEL, MOE_TF), F32),
                        pltpu.VMEM((MOE_GU_SLOTS, D_MODEL, MOE_TF), F32),
                        pltpu.VMEM((MOE_WD_SLOTS, MOE_TF, D_MODEL), F32),
                        pltpu.SMEM((2,), jnp.int32),
                        pltpu.SemaphoreType.DMA(()),
                        pltpu.SemaphoreType.DMA(()),
                        pltpu.SemaphoreType.DMA((MOE_GU_SLOTS + MOE_WD_SLOTS,))])
    return pl.pallas_call(
        _moe_kernel,
        grid_spec=grid_spec,
        out_shape=jax.ShapeDtypeStruct((n_rows, HALF), U32),
        compiler_params=_cparams(1),
        name="moe_ffn",
    )(item_e, item_start, item_nb, tail_blk, xs, w_gu, w_d, b_gu, b_d)


def _combine_kernel(y0_ref, y1_ref, y2_ref, y3_ref, gate_ref, h2_ref, g_ref, b_ref, o_ref):
    h2 = h2_ref[...]
    lo = DEEPNORM_ALPHA * h2[:, :HALF]
    hi = DEEPNORM_ALPHA * h2[:, HALF:]
    for k, y_ref in enumerate((y0_ref, y1_ref, y2_ref, y3_ref)):
        ylo, yhi = _unpack_halves(y_ref[...])
        gk = gate_ref[:, k:k + 1]
        lo = lo + gk * ylo
        hi = hi + gk * yhi
    mu = (jnp.sum(lo, -1, keepdims=True) + jnp.sum(hi, -1, keepdims=True)) * (1.0 / D_MODEL)
    lo = lo - mu
    hi = hi - mu
    var = (jnp.sum(lo * lo, -1, keepdims=True) + jnp.sum(hi * hi, -1, keepdims=True)) * (1.0 / D_MODEL)
    inv = lax.rsqrt(var + LN_EPS)
    o_ref[:, :HALF] = lo * inv * g_ref[:, :HALF] + b_ref[:, :HALF]
    o_ref[:, HALF:] = hi * inv * g_ref[:, HALF:] + b_ref[:, HALF:]


def _combine(yg, gates, h2, g, b, tm=256):
    t, d = h2.shape
    full = lambda a: pl.BlockSpec(a.shape, lambda i: (0,) * a.ndim)
    return pl.pallas_call(
        _combine_kernel,
        grid=(t // tm,),
        in_specs=[pl.BlockSpec((None, tm, HALF), functools.partial(lambda k, i: (k, i, 0), k))
                  for k in range(TOP_K)] + [
                  pl.BlockSpec((tm, LANES), lambda i: (i, 0)),
                  pl.BlockSpec((tm, d), lambda i: (i, 0)), full(g), full(b)],
        out_specs=pl.BlockSpec((tm, d), lambda i: (i, 0)),
        out_shape=jax.ShapeDtypeStruct((t, d), F32),
        compiler_params=_cparams(1),
        name="combine_ln",
    )(yg, yg, yg, yg, gates, h2, g, b)


def _routing_tables(sel, counts_f):
    t = sel.shape[1]
    idx = sel[:TOP_K].astype(jnp.int32)
    pos = sel[TOP_K:].astype(jnp.int32)
    counts = counts_f[0, :N_EXPERTS].astype(jnp.int32)
    padded = ((counts + MOE_BLOCK - 1) // MOE_BLOCK) * MOE_BLOCK
    pends = jnp.cumsum(padded)
    pstarts = pends - padded
    experts = jnp.arange(N_EXPERTS, dtype=jnp.int32)[:, None, None]
    first_row = jnp.sum(jnp.where(idx[None] == experts, pstarts[:, None, None], 0), axis=0)
    dest = (first_row + pos).reshape(-1)
    n_rows = t * TOP_K + N_EXPERTS * MOE_BLOCK
    j = jnp.arange(MOE_BLOCK, dtype=jnp.int32)[None, :]
    is_pad = j < (padded - counts)[:, None]
    n_after = jnp.cumsum((~is_pad).reshape(-1).astype(jnp.int32)).reshape(is_pad.shape) - 1
    zero_rows = jnp.where(is_pad, (pstarts + counts)[:, None] + j, pends[-1] + n_after)
    row_of = jnp.concatenate([dest, zero_rows.reshape(-1)])

    nchunk = (padded + MOE_ROWS - 1) // MOE_ROWS
    cend = jnp.cumsum(nchunk)
    cstart = cend - nchunk
    total = cend[-1]
    it = jnp.arange(MOE_ITEMS, dtype=jnp.int32)
    valid = it < total
    it_c = jnp.minimum(it, total - 1)
    ie = jnp.searchsorted(cend, it_c, side="right").astype(jnp.int32)
    c = it_c - cstart[ie]
    istart = (pstarts[ie] + c * MOE_ROWS).astype(jnp.int32)
    inb = jnp.where(valid, jnp.minimum(MOE_ROWS, padded[ie] - c * MOE_ROWS) // MOE_BLOCK, 0)
    tail_blk = (pends[-1] // MOE_BLOCK).astype(jnp.int32).reshape(1)
    return dest, row_of, ie, istart, inb.astype(jnp.int32), tail_blk, n_rows


def kernel(x, positions, ln_in_g, ln_in_b, w_in, q_a_norm, w_q_b, kv_a_norm, w_kv_b, sb_out_norm,
           mla_out_norm, w_o, ln_mix_g, ln_mix_b, w_router, b_router, w_gate_up, b_gate_up, w_down,
           b_down, ln_ffn_g, ln_ffn_b):
    bsz, seq, d = x.shape
    t = bsz * seq
    x2 = x.reshape(t, d)
    row = lambda v: v.reshape(1, -1)

    w_in0 = w_in[0]
    w_sb = w_in0[:, :3 * SB_WIDTH].astype(BF16)
    w_lat = jnp.pad(w_in0[:, 3 * SB_WIDTH:], ((0, 0), (0, LANES - MLA_ROPE))).astype(BF16)
    wq = w_q_b[0].reshape(MLA_Q_RANK, MLA_HEADS, MLA_NOPE + MLA_ROPE)
    w_qn = wq[:, :, :MLA_NOPE].reshape(MLA_Q_RANK, -1).astype(BF16)
    w_qp = jnp.pad(wq[:, :, MLA_NOPE:], ((0, 0), (0, 0), (0, LANES - MLA_ROPE))
                   ).reshape(MLA_Q_RANK, -1).astype(BF16)
    wkv = w_kv_b[0].reshape(MLA_KV_RANK, MLA_HEADS, MLA_NOPE + MLA_V)
    w_kn = wkv[:, :, :MLA_NOPE].reshape(MLA_KV_RANK, -1).astype(BF16)
    w_v = wkv[:, :, MLA_NOPE:].reshape(MLA_KV_RANK, -1).astype(BF16)
    w_r32 = jnp.pad(w_router[0], ((0, 0), (0, LANES - N_EXPERTS)))
    w_r_hi = w_r32.astype(BF16)
    w_r = jnp.concatenate([w_r_hi, (w_r32 - w_r_hi.astype(F32)).astype(BF16)], axis=1)
    b_r = jnp.pad(b_router[0], (0, LANES - N_EXPERTS)).reshape(1, LANES)
    half = MLA_ROPE // 2
    inv_freq = ROPE_THETA ** (-(jnp.arange(half, dtype=F32) * 2.0 / MLA_ROPE))
    invf = jnp.concatenate([inv_freq, inv_freq, jnp.zeros((LANES - MLA_ROPE,), F32)]).reshape(1, LANES)
    pos_f = positions.reshape(t, 1).astype(F32)

    for l in range(DEPTH):
        u_sb, h_b = _ln_inproj(x2, row(ln_in_g), row(ln_in_b), w_sb)
        q_m, k_m, v_m = _mla_prep(h_b, pos_f, invf, w_lat, row(q_a_norm[l]), row(kv_a_norm[l]),
                                  w_qn, w_qp, w_kn, w_v)
        sb_o = _sb_attention(u_sb.reshape(bsz, seq, -1)).reshape(t, -1)
        mla_o = _mla_attention(q_m.reshape(bsz, seq, -1), k_m.reshape(bsz, seq, -1),
                               v_m.reshape(bsz, seq, -1)).reshape(t, -1)
        h2, h2p, sel, gates, counts_f = _mix_router(
            sb_o, mla_o, x2, row(ln_in_g), row(ln_in_b), row(sb_out_norm[l]), row(mla_out_norm[l]),
            w_o[l].astype(BF16), row(ln_mix_g[l]), row(ln_mix_b[l]), w_r, b_r)
        dest, row_of, ie, istart, inb, tail_blk, n_rows = _routing_tables(sel, counts_f)
        xs = _scatter_rows(h2p, TOP_K, row_of, n_rows)
        ys = _moe_ffn(ie, istart, inb, tail_blk, xs, w_gate_up[l],
                      b_gate_up[l].reshape(N_EXPERTS, 2 * MOE_NF, MOE_TF),
                      w_down[l], b_down[l].reshape(N_EXPERTS, 1, -1), n_rows)
        yg = _gather_rows(ys, dest).reshape(TOP_K, t, HALF)
        out = _combine(yg, gates, h2, row(ln_ffn_g[l]), row(ln_ffn_b[l]))
    return out.reshape(bsz, seq, d)
```

```python
import functools

import jax
import jax.numpy as jnp
from jax import lax
from jax.experimental import pallas as pl
from jax.experimental.pallas import tpu as pltpu

D_MODEL = 2048
DEPTH = 1
CHUNK = 64
SB_HEADS = 8
HEAD_DIM = 128
SB_WIDTH = SB_HEADS * HEAD_DIM
MLA_HEADS = 8
MLA_Q_RANK = 512
MLA_KV_RANK = 256
MLA_NOPE = 128
MLA_ROPE = 64
MLA_V = 128
MLA_WIDTH = MLA_HEADS * MLA_V
MLA_QK_PAD = 256
ROPE_THETA = 10000.0
N_EXPERTS = 32
TOP_K = 4
D_FF = 2048
SWIGLU_LIMIT = 7.0
SWIGLU_ALPHA = 1.702
MOE_BLOCK = 128
LN_EPS = 1e-5
RMS_EPS = 1e-6
DEEPNORM_ALPHA = (2 * DEPTH) ** 0.25

LANES = 128
SUBLANES = 8
HALF = D_MODEL // 2
VMEM_LIMIT_BYTES = 56 * 1024 * 1024

BF16 = jnp.bfloat16
F32 = jnp.float32
U32 = jnp.uint32


def _cparams(n_axes, vmem=None, flags=None):
    return pltpu.CompilerParams(dimension_semantics=("arbitrary",) * n_axes,
                                vmem_limit_bytes=vmem or VMEM_LIMIT_BYTES, flags=flags)


def _layer_norm(x, g, b):
    mu = jnp.mean(x, -1, keepdims=True)
    xc = x - mu
    var = jnp.mean(xc * xc, -1, keepdims=True)
    return xc * lax.rsqrt(var + LN_EPS) * g + b


def _rms_norm(x, g):
    ms = jnp.mean(x * x, -1, keepdims=True)
    return x * lax.rsqrt(ms + RMS_EPS) * g


def _pack_halves(lo_f32, hi_f32):
    lo = lax.bitcast_convert_type(lo_f32.astype(BF16).astype(F32), U32)
    hi = lax.bitcast_convert_type(hi_f32.astype(BF16).astype(F32), U32)
    return lax.shift_right_logical(lo, U32(16)) | (hi & U32(0xFFFF0000))


def _unpack_halves(w):
    lo = lax.bitcast_convert_type(lax.shift_left(w, U32(16)), F32)
    hi = lax.bitcast_convert_type(w & U32(0xFFFF0000), F32)
    return lo, hi


def _ln_inproj_kernel(x_ref, g_ref, b_ref, w_ref, u_ref, h_ref):
    h = _layer_norm(x_ref[...], g_ref[...], b_ref[...]).astype(BF16)
    h_ref[...] = h
    u_ref[...] = jnp.dot(h, w_ref[...], preferred_element_type=F32).astype(BF16)


def _ln_inproj(x2, g, b, w_sb, n, tm=512):
    t, d = x2.shape
    return pl.pallas_call(
        _ln_inproj_kernel,
        grid=(t // tm,),
        in_specs=[pl.BlockSpec((tm, d), lambda i: (i, 0)),
                  pl.BlockSpec((1, d), lambda i: (0, 0)),
                  pl.BlockSpec((1, d), lambda i: (0, 0)),
                  pl.BlockSpec((d, n), lambda i: (0, 0))],
        out_specs=[pl.BlockSpec((tm, n), lambda i: (i, 0)),
                   pl.BlockSpec((tm, d), lambda i: (i, 0))],
        out_shape=[jax.ShapeDtypeStruct((t, n), BF16), jax.ShapeDtypeStruct((t, d), BF16)],
        compiler_params=_cparams(1),
        name="ln_inproj",
    )(x2, g, b, w_sb)


def _rope_slab(x, cos, sin_neg, sin_pos):
    return (x * cos + pltpu.roll(x, LANES - MLA_ROPE // 2, 1) * sin_neg
            + pltpu.roll(x, MLA_ROPE // 2, 1) * sin_pos)


def _mla_prep_kernel(h_ref, pos_ref, invf_ref, wl_ref, qg_ref, kg_ref, wqn_ref, wqp_ref,
                     wkn_ref, wv_ref, q_ref, k_ref, v_ref):
    lat = jnp.dot(h_ref[...], wl_ref[...], preferred_element_type=F32)
    qn = _rms_norm(lat[:, :MLA_Q_RANK], qg_ref[...]).astype(BF16)
    kvn = _rms_norm(lat[:, MLA_Q_RANK:MLA_Q_RANK + MLA_KV_RANK], kg_ref[...]).astype(BF16)
    k_rope = lat[:, MLA_Q_RANK + MLA_KV_RANK:]

    ang = pos_ref[...] * invf_ref[...]
    cos = jnp.cos(ang)
    sin = jnp.sin(ang)
    lane = lax.broadcasted_iota(jnp.int32, ang.shape, 1)
    sin_neg = jnp.where(lane < MLA_ROPE // 2, -sin, 0.0)
    sin_pos = jnp.where((lane >= MLA_ROPE // 2) & (lane < MLA_ROPE), sin, 0.0)

    q_nope = jnp.dot(qn, wqn_ref[...], preferred_element_type=F32)
    q_pe = jnp.dot(qn, wqp_ref[...], preferred_element_type=F32)
    k_nope = jnp.dot(kvn, wkn_ref[...], preferred_element_type=F32)
    v_ref[...] = jnp.dot(kvn, wv_ref[...], preferred_element_type=F32).astype(BF16)
    k_pe = _rope_slab(k_rope, cos, sin_neg, sin_pos).astype(BF16)
    for hd in range(MLA_HEADS):
        c0 = hd * MLA_QK_PAD
        s0 = hd * LANES
        q_ref[:, c0:c0 + LANES] = q_nope[:, s0:s0 + LANES].astype(BF16)
        q_ref[:, c0 + LANES:c0 + 2 * LANES] = _rope_slab(
            q_pe[:, s0:s0 + LANES], cos, sin_neg, sin_pos).astype(BF16)
        k_ref[:, c0:c0 + LANES] = k_nope[:, s0:s0 + LANES].astype(BF16)
        k_ref[:, c0 + LANES:c0 + 2 * LANES] = k_pe


def _mla_prep(h_b, pos_f, invf, w_lat, qg, kg, w_qn, w_qp, w_kn, w_v, tm=512):
    t, d = h_b.shape
    full = lambda a: pl.BlockSpec(a.shape, lambda i: (0,) * a.ndim)
    wq = MLA_HEADS * MLA_QK_PAD
    return pl.pallas_call(
        _mla_prep_kernel,
        grid=(t // tm,),
        in_specs=[pl.BlockSpec((tm, d), lambda i: (i, 0)),
                  pl.BlockSpec((tm, 1), lambda i: (i, 0)),
                  full(invf), full(w_lat), full(qg), full(kg), full(w_qn), full(w_qp),
                  full(w_kn), full(w_v)],
        out_specs=[pl.BlockSpec((tm, wq), lambda i: (i, 0)),
                   pl.BlockSpec((tm, wq), lambda i: (i, 0)),
                   pl.BlockSpec((tm, MLA_WIDTH), lambda i: (i, 0))],
        out_shape=[jax.ShapeDtypeStruct((t, wq), BF16), jax.ShapeDtypeStruct((t, wq), BF16),
                   jax.ShapeDtypeStruct((t, MLA_WIDTH), BF16)],
        compiler_params=_cparams(1),
        name="mla_prep",
    )(h_b, pos_f, invf, w_lat, qg, kg, w_qn, w_qp, w_kn, w_v)


def _sb_attn_kernel(q_ref, k_ref, v_ref, o_ref, *, tq, tk, heads, scale):
    i = pl.program_id(2)
    t_idx = i * tq + lax.broadcasted_iota(jnp.int32, (tq, 1), 0)
    r_i = lax.broadcasted_iota(jnp.int32, (tk, tk), 0)
    c_i = lax.broadcasted_iota(jnp.int32, (tk, tk), 1)
    suffix = (r_i >= c_i).astype(BF16)
    suffix2 = jnp.concatenate([suffix, suffix], axis=0)
    nkb = (i + 1) * (tq // tk)

    def key_block(j, carry, diagonal):
        k0 = pl.multiple_of(j * tk, tk)
        if diagonal:
            past = (k0 + lax.broadcasted_iota(jnp.int32, (1, tk), 1)) < t_idx
        zs, splits = [], []
        for hd in range(heads):
            cols = slice(hd * HEAD_DIM, (hd + 1) * HEAD_DIM)
            z = lax.dot_general(q_ref[:, cols], k_ref[pl.ds(k0, tk), cols], (((1,), (1,)), ((), ())),
                                preferred_element_type=F32) * scale
            sp = jnp.maximum(z, 0.0) + jnp.log(1.0 + jnp.exp(-jnp.abs(z)))
            if diagonal:
                sp = jnp.where(past, sp, 0.0)
            hi = sp.astype(BF16)
            lo = (sp - hi.astype(F32)).astype(BF16)
            zs.append(z)
            splits.append(jnp.concatenate([hi, lo], axis=1))
        incl_all = jnp.dot(jnp.concatenate(splits, axis=0), suffix2, preferred_element_type=F32)
        out = []
        for hd in range(heads):
            c, acc = carry[hd]
            cols = slice(hd * HEAD_DIM, (hd + 1) * HEAD_DIM)
            incl = incl_all[hd * tq:(hd + 1) * tq]
            w = jnp.exp(zs[hd] - incl - c)
            if diagonal:
                w = jnp.where(past, w, 0.0)
            acc = acc + jnp.dot(w.astype(BF16), v_ref[pl.ds(k0, tk), cols],
                                preferred_element_type=F32)
            out.append((c + incl[:, 0:1], acc))
        return tuple(out)

    bpt = tq // tk

    def trip(p, carry, diagonal):
        for bk in range(bpt):
            carry = key_block(nkb - 1 - bpt * p - bk, carry, diagonal)
        return carry

    init = tuple((jnp.zeros((tq, 1), F32), jnp.zeros((tq, HEAD_DIM), F32)) for _ in range(heads))
    res = trip(0, init, True)
    res = lax.fori_loop(1, i + 1, functools.partial(trip, diagonal=False), res)
    for hd in range(heads):
        o_ref[:, hd * HEAD_DIM:(hd + 1) * HEAD_DIM] = res[hd][1].astype(o_ref.dtype)


def _sb_attention(u_sb3, tq=256, tk=128, heads=SB_HEADS):
    b, s, _ = u_sb3.shape
    assert tq % tk == 0 and SB_HEADS % heads == 0
    groups = SB_HEADS // heads
    w = heads * HEAD_DIM
    kern = functools.partial(_sb_attn_kernel, tq=tq, tk=tk, heads=heads, scale=HEAD_DIM ** -0.5)
    return pl.pallas_call(
        kern,
        grid=(b, groups, s // tq),
        in_specs=[pl.BlockSpec((None, tq, w), lambda bb, g, i: (bb, i, g)),
                  pl.BlockSpec((None, s, w), lambda bb, g, i: (bb, 0, groups + g)),
                  pl.BlockSpec((None, s, w), lambda bb, g, i: (bb, 0, 2 * groups + g))],
        out_specs=pl.BlockSpec((None, tq, w), lambda bb, g, i: (bb, i, g)),
        out_shape=jax.ShapeDtypeStruct((b, s, SB_WIDTH), BF16),
        compiler_params=_cparams(3),
        name="sb_attention",
    )(u_sb3, u_sb3, u_sb3)


def _mla_attn_kernel(q_ref, k_ref, v_ref, o_ref, *, tq, tk, heads, scale):
    i = pl.program_id(2)
    t_chunk = lax.shift_right_logical(
        i * tq + lax.broadcasted_iota(jnp.int32, (tq, 1), 0), CHUNK.bit_length() - 1)

    def body(j, carry):
        k0 = pl.multiple_of(j * tk, tk)
        s_chunk = lax.shift_right_logical(
            k0 + lax.broadcasted_iota(jnp.int32, (1, tk), 1), CHUNK.bit_length() - 1)
        visible = s_chunk <= t_chunk
        def qk_scores(hd):
            qk = slice(hd * MLA_QK_PAD, (hd + 1) * MLA_QK_PAD)
            return lax.dot_general(q_ref[:, qk], k_ref[pl.ds(k0, tk), qk],
                                   (((1,), (1,)), ((), ())), preferred_element_type=F32)

        out = []
        nxt = qk_scores(0)
        for hd in range(heads):
            m, l, acc = carry[hd]
            s = nxt
            if hd + 1 < heads:
                nxt = qk_scores(hd + 1)
            s = jnp.where(visible, s * scale, -1e30)
            m_new = jnp.maximum(m, jnp.max(s, -1, keepdims=True))
            alpha = jnp.exp(m - m_new)
            p = jnp.exp(s - m_new)
            l = alpha * l + jnp.sum(p, -1, keepdims=True)
            vs = v_ref[pl.ds(k0, tk), hd * MLA_V:(hd + 1) * MLA_V]
            acc = alpha * acc + jnp.dot(p.astype(BF16), vs, preferred_element_type=F32)
            out.append((m_new, l, acc))
        return tuple(out)

    init = tuple((jnp.full((tq, 1), -1e30, F32), jnp.zeros((tq, 1), F32),
                  jnp.zeros((tq, MLA_V), F32)) for _ in range(heads))
    res = lax.fori_loop(0, (i + 1) * (tq // tk), body, init)
    for hd in range(heads):
        _, l, acc = res[hd]
        o_ref[:, hd * MLA_V:(hd + 1) * MLA_V] = (acc / l).astype(o_ref.dtype)


def _mla_attention(q3, k3, v3, tq=256, tk=256, heads=MLA_HEADS):
    b, s, _ = q3.shape
    assert MLA_HEADS % heads == 0 and tq % tk == 0 and tk % CHUNK == 0
    kern = functools.partial(_mla_attn_kernel, tq=tq, tk=tk, heads=heads,
                             scale=(MLA_NOPE + MLA_ROPE) ** -0.5)
    return pl.pallas_call(
        kern,
        grid=(b, MLA_HEADS // heads, s // tq),
        in_specs=[pl.BlockSpec((None, tq, heads * MLA_QK_PAD), lambda bb, g, i: (bb, i, g)),
                  pl.BlockSpec((None, s, heads * MLA_QK_PAD), lambda bb, g, i: (bb, 0, g)),
                  pl.BlockSpec((None, s, heads * MLA_V), lambda bb, g, i: (bb, 0, g))],
        out_specs=pl.BlockSpec((None, tq, heads * MLA_V), lambda bb, g, i: (bb, i, g)),
        out_shape=jax.ShapeDtypeStruct((b, s, MLA_WIDTH), BF16),
        compiler_params=_cparams(3),
        name="mla_attention",
    )(q3, k3, v3)


def _mix_router_kernel(sb_ref, mla_ref, x_ref, lig_ref, lib_ref, sg_ref, mg_ref, wo_ref,
                       lmg_ref, lmb_ref, wr_ref, br_ref,
                       h2_ref, h2p_ref, sel_ref, gate_ref, cnt_ref, run_ref, *, tm):
    step = pl.program_id(0)

    @pl.when(step == 0)
    def _():
        run_ref[...] = jnp.zeros_like(run_ref)

    a = _rms_norm(sb_ref[...].astype(F32), sg_ref[...]).astype(BF16)
    bm = _rms_norm(mla_ref[...].astype(F32), mg_ref[...]).astype(BF16)
    mix = (jnp.dot(a, wo_ref[:SB_WIDTH, :], preferred_element_type=F32)
           + jnp.dot(bm, wo_ref[SB_WIDTH:, :], preferred_element_type=F32))
    h = _layer_norm(x_ref[...], lig_ref[...], lib_ref[...])
    h2 = _layer_norm(DEEPNORM_ALPHA * h + mix, lmg_ref[...], lmb_ref[...])
    h2_ref[...] = h2
    h2p_ref[...] = _pack_halves(h2[:, :HALF], h2[:, HALF:])

    h2_hi = h2.astype(BF16)
    h2_lo = (h2 - h2_hi.astype(F32)).astype(BF16)
    hw = jnp.dot(h2_hi, wr_ref[...], preferred_element_type=F32)
    logits = (hw[:, :LANES] + hw[:, LANES:]
              + jnp.dot(h2_lo, wr_ref[:, :LANES], preferred_element_type=F32) + br_ref[...])
    lane = lax.broadcasted_iota(jnp.int32, (tm, LANES), 1)
    lane_f = lane.astype(F32)
    logits = jnp.where(lane < N_EXPERTS, logits, -jnp.inf)
    vals, hots = [], []
    sel = jnp.zeros((tm, LANES), F32)
    for k in range(TOP_K):
        m = jnp.max(logits, -1, keepdims=True)
        idx = jnp.min(jnp.where(logits == m, lane_f, float(LANES)), -1, keepdims=True)
        hot = lane_f == idx
        logits = jnp.where(hot, -jnp.inf, logits)
        vals.append(m)
        hots.append(hot)
        sel = jnp.where(lane == k, idx, sel)
    exps = [jnp.exp(v - vals[0]) for v in vals]
    denom = exps[0] + exps[1] + exps[2] + exps[3]
    gates = jnp.zeros((tm, LANES), F32)
    for k in range(TOP_K):
        gates = jnp.where(lane == k, exps[k] / denom, gates)
    gate_ref[...] = gates

    onehot = (hots[0] | hots[1] | hots[2] | hots[3]).astype(F32)
    r_i = lax.broadcasted_iota(jnp.int32, (tm, tm), 0)
    c_i = lax.broadcasted_iota(jnp.int32, (tm, tm), 1)
    before = (c_i < r_i).astype(BF16)
    rank = jnp.dot(before, onehot.astype(BF16), preferred_element_type=F32) + run_ref[...]
    for k in range(TOP_K):
        pos = jnp.sum(jnp.where(hots[k], rank, 0.0), -1, keepdims=True)
        sel = jnp.where(lane == TOP_K + k, pos, sel)
    sel_ref[...] = sel.T[:2 * TOP_K, :]
    run_ref[...] = run_ref[...] + jnp.sum(onehot, 0, keepdims=True)
    cnt_ref[...] = run_ref[...]


def _mix_router(sb_o, mla_o, x2, lig, lib, sg, mg, w_o, lmg, lmb, w_r, b_r, tm=512):
    t, d = x2.shape
    full = lambda a: pl.BlockSpec(a.shape, lambda i: (0,) * a.ndim)
    row = lambda w: pl.BlockSpec((tm, w), lambda i: (i, 0))
    return pl.pallas_call(
        functools.partial(_mix_router_kernel, tm=tm),
        grid=(t // tm,),
        in_specs=[row(SB_WIDTH), row(MLA_WIDTH), row(d), full(lig), full(lib), full(sg), full(mg),
                  full(w_o), full(lmg), full(lmb), full(w_r), full(b_r)],
        out_specs=[row(d), row(HALF), pl.BlockSpec((2 * TOP_K, tm), lambda i: (0, i)), row(LANES),
                   pl.BlockSpec((1, LANES), lambda i: (0, 0))],
        out_shape=[jax.ShapeDtypeStruct((t, d), F32), jax.ShapeDtypeStruct((t, HALF), U32),
                   jax.ShapeDtypeStruct((2 * TOP_K, t), F32), jax.ShapeDtypeStruct((t, LANES), F32),
                   jax.ShapeDtypeStruct((1, LANES), F32)],
        scratch_shapes=[pltpu.VMEM((1, LANES), F32)],
        compiler_params=_cparams(1),
        name="mix_router",
    )(sb_o, mla_o, x2, lig, lib, sg, mg, w_o, lmg, lmb, w_r, b_r)


ROWS_PER_STEP = 4096


def _wait_rows(like_hbm, sem, groups):
    pltpu.make_async_copy(like_hbm.at[pl.ds(0, groups)], like_hbm.at[pl.ds(0, groups)], sem).wait()


def _row_of(ref3, r):
    if isinstance(r, tuple):
        return ref3.at[r[0], pl.ds(r[1], 1), :]
    return ref3.at[lax.shift_right_logical(r, 3), pl.ds(r & (SUBLANES - 1), 1), :]


def _gather_rows_kernel(idx_ref, src_hbm, out_ref, sem, *, rows):
    def issue(g, _):
        r0 = pl.multiple_of(g * SUBLANES, SUBLANES)
        for k in range(SUBLANES):
            pltpu.make_async_copy(_row_of(src_hbm, idx_ref[0, 0, r0 + k]),
                                  _row_of(out_ref, (g, k)), sem).start()
        return 0

    lax.fori_loop(0, rows // SUBLANES, issue, 0)
    _wait_rows(src_hbm, sem, rows // SUBLANES)


def _scatter_rows_kernel(idx_ref, src_ref, out_hbm, zero_ref, sem, *, rows, src_steps):
    step = pl.program_id(0)

    @pl.when(step == 0)
    def _():
        zero_ref[...] = jnp.zeros_like(zero_ref)

    @pl.when(step < src_steps)
    def _():
        def issue(g, _):
            r0 = pl.multiple_of(g * SUBLANES, SUBLANES)
            for k in range(SUBLANES):
                pltpu.make_async_copy(_row_of(src_ref, (g, k)),
                                      _row_of(out_hbm, idx_ref[0, 0, r0 + k]), sem).start()
            return 0
        lax.fori_loop(0, rows // SUBLANES, issue, 0)

    @pl.when(step >= src_steps)
    def _():
        def issue(a, _):
            pltpu.make_async_copy(_row_of(zero_ref, (0, 0)),
                                  _row_of(out_hbm, idx_ref[0, 0, a]), sem).start()
            return 0
        lax.fori_loop(0, rows, issue, 0, unroll=8)

    _wait_rows(out_hbm, sem, rows // SUBLANES)


def _gather_rows(src, idx):
    rows = ROWS_PER_STEP
    n, w = idx.shape[0], src.shape[1]
    assert n % rows == 0 and src.shape[0] >= rows and src.shape[0] % SUBLANES == 0
    out = pl.pallas_call(
        functools.partial(_gather_rows_kernel, rows=rows),
        grid=(n // rows,),
        in_specs=[pl.BlockSpec((1, 1, rows), lambda i: (i, 0, 0), memory_space=pltpu.SMEM),
                  pl.BlockSpec(memory_space=pl.ANY)],
        out_specs=pl.BlockSpec((rows // SUBLANES, SUBLANES, w), lambda i: (i, 0, 0)),
        out_shape=jax.ShapeDtypeStruct((n // SUBLANES, SUBLANES, w), src.dtype),
        scratch_shapes=[pltpu.SemaphoreType.DMA(())],
        compiler_params=_cparams(1),
        name="gather_rows",
    )(idx.reshape(n // rows, 1, rows), src.reshape(-1, SUBLANES, w))
    return out.reshape(n, w)


def _scatter_rows(src, passes, idx, out_rows):
    rows = ROWS_PER_STEP
    n, w = idx.shape[0], src.shape[1]
    src_blocks = src.shape[0] // rows
    assert src.shape[0] % rows == 0 and n % rows == 0 and n == out_rows
    src_steps = passes * src_blocks
    kern = functools.partial(_scatter_rows_kernel, rows=rows, src_steps=src_steps)

    def src_block(i):
        return jnp.minimum(i // passes, src_blocks - 1), 0

    def idx_block(i):
        return jnp.where(i < src_steps, lax.rem(i, passes) * src_blocks + i // passes, i), 0, 0

    out = pl.pallas_call(
        kern,
        grid=(n // rows,),
        in_specs=[pl.BlockSpec((1, 1, rows), idx_block, memory_space=pltpu.SMEM),
                  pl.BlockSpec((rows // SUBLANES, SUBLANES, w), lambda i: src_block(i) + (0,))],
        out_specs=pl.BlockSpec(memory_space=pl.ANY),
        out_shape=jax.ShapeDtypeStruct((out_rows // SUBLANES, SUBLANES, w), src.dtype),
        scratch_shapes=[pltpu.VMEM((1, SUBLANES, w), src.dtype), pltpu.SemaphoreType.DMA(())],
        compiler_params=_cparams(1),
        name="scatter_rows",
    )(idx.reshape(n // rows, 1, rows), src.reshape(-1, SUBLANES, w))
    return out.reshape(out_rows, w)


MOE_ROWS = 1536
MOE_SUB = 512
MOE_TF = 256
MOE_NF = D_FF // MOE_TF
MOE_ITEMS = N_EXPERTS + (N_EXPERTS * (MOE_BLOCK - 1) + 8192 * TOP_K) // MOE_ROWS + 1


MOE_GU_SLOTS = 2
MOE_WD_SLOTS = 3


def _moe_kernel(ie_ref, is_ref, nb_ref, tail_ref, xs_hbm, wgu_hbm, wd_hbm, bgu_ref, bd_ref,
                ys_hbm, pkin_ref, pkout_ref, x_ref, acc_ref, wg_buf, wu_buf, wd_buf,
                pend_ref, sem_in, sem_out, sem_w):
    i = pl.program_id(0)
    nb = nb_ref[i]
    start = is_ref[i]
    n_out_blocks = ys_hbm.shape[0] // MOE_BLOCK

    def weight_copies(item, f):
        e = ie_ref[item]
        g = item * MOE_NF + f
        gs = lax.rem(g, MOE_GU_SLOTS)
        ws = lax.rem(g, MOE_WD_SLOTS)
        c0 = pl.multiple_of(f * MOE_TF, MOE_TF)
        return (pltpu.make_async_copy(wgu_hbm.at[e, :, pl.ds(c0, MOE_TF)], wg_buf.at[gs],
                                      sem_w.at[gs]),
                pltpu.make_async_copy(wgu_hbm.at[e, :, pl.ds(D_FF + c0, MOE_TF)], wu_buf.at[gs],
                                      sem_w.at[gs]),
                pltpu.make_async_copy(wd_hbm.at[e, pl.ds(c0, MOE_TF), :], wd_buf.at[ws],
                                      sem_w.at[MOE_GU_SLOTS + ws]))

    def vm_block(ref, b):
        return ref.at[pl.ds(pl.multiple_of(b * MOE_BLOCK, MOE_BLOCK), MOE_BLOCK)]

    def hbm_block(ref, first_row, b):
        return ref.at[pl.ds(pl.multiple_of(first_row + b * MOE_BLOCK, MOE_BLOCK), MOE_BLOCK)]

    def in_copy(first_row, b):
        return pltpu.make_async_copy(hbm_block(xs_hbm, first_row, b), vm_block(pkin_ref, b), sem_in)

    def out_copy(first_row, b):
        return pltpu.make_async_copy(vm_block(pkout_ref, b), hbm_block(ys_hbm, first_row, b), sem_out)

    def for_blocks(n, fn):
        def body(b, _):
            fn(b)
            return 0
        lax.fori_loop(0, n, body, 0)

    def next_item_valid():
        return (i + 1 < MOE_ITEMS) & (nb_ref[jnp.minimum(i + 1, MOE_ITEMS - 1)] > 0)

    @pl.when(i == 0)
    def _():
        x_ref[...] = jnp.zeros_like(x_ref)
        acc_ref[...] = jnp.zeros_like(acc_ref)
        pend_ref[0] = 0
        pend_ref[1] = 0
        for_blocks(nb, lambda b: in_copy(start, b).start())
        for c in weight_copies(0, 0):
            c.start()

    @pl.when(nb > 0)
    def _():
        def unpack(b):
            r0 = pl.multiple_of(b * MOE_BLOCK, MOE_BLOCK)
            lo, hi = _unpack_halves(pkin_ref[pl.ds(r0, MOE_BLOCK), :])
            x_ref[pl.ds(r0, MOE_BLOCK), :HALF] = lo.astype(BF16)
            x_ref[pl.ds(r0, MOE_BLOCK), HALF:] = hi.astype(BF16)

        for_blocks(nb, lambda b: in_copy(start, b).wait())
        for_blocks(nb, unpack)

        @pl.when(next_item_valid())
        def _():
            nxt = jnp.minimum(i + 1, MOE_ITEMS - 1)
            for_blocks(nb_ref[nxt], lambda b: in_copy(is_ref[nxt], b).start())

        n_sub = (nb * MOE_BLOCK) // MOE_SUB
        n_tail = nb - n_sub * (MOE_SUB // MOE_BLOCK)

        def gate_up(f, r0, rows, gs):
            x = x_ref[pl.ds(pl.multiple_of(r0, MOE_BLOCK), rows), :]
            g = (jnp.dot(x, wg_buf[gs].astype(BF16), preferred_element_type=F32)
                 + bgu_ref[pl.ds(f, 1), :])
            u = (jnp.dot(x, wu_buf[gs].astype(BF16), preferred_element_type=F32)
                 + bgu_ref[pl.ds(MOE_NF + f, 1), :])
            g = jnp.minimum(g, SWIGLU_LIMIT)
            u = jnp.clip(u, -SWIGLU_LIMIT, SWIGLU_LIMIT)
            return ((u + 1.0) * (g * jax.nn.sigmoid(SWIGLU_ALPHA * g))).astype(BF16)

        def down(act, r0, ws, first):
            rows = act.shape[0]
            r0 = pl.multiple_of(r0, MOE_BLOCK)
            y = jnp.dot(act, wd_buf[ws].astype(BF16), preferred_element_type=F32)
            prev = jnp.where(first == 1, bd_ref[...], acc_ref[pl.ds(r0, rows), :])
            acc_ref[pl.ds(r0, rows), :] = prev + y

        def tile(f, carry):
            for c in weight_copies(i, f):
                c.wait()

            @pl.when(f + 1 < MOE_NF)
            def _():
                for c in weight_copies(i, f + 1):
                    c.start()

            @pl.when((f + 1 == MOE_NF) & next_item_valid())
            def _():
                for c in weight_copies(i + 1, 0):
                    c.start()

            g = i * MOE_NF + f
            gs = lax.rem(g, MOE_GU_SLOTS)
            ws = lax.rem(g, MOE_WD_SLOTS)
            first = jnp.asarray(f == 0, jnp.int32)

            def sub(s, c):
                act, r_prev, ws_prev, first_prev = c
                down(act, r_prev, ws_prev, first_prev)
                return gate_up(f, s * MOE_SUB, MOE_SUB, gs), s * MOE_SUB, ws, first

            carry = lax.fori_loop(0, n_sub, sub, carry)

            def tail(t, _):
                r0 = n_sub * MOE_SUB + t * MOE_BLOCK
                down(gate_up(f, r0, MOE_BLOCK, gs), r0, ws, first)
                return 0

            lax.fori_loop(0, n_tail, tail, 0)
            return carry

        assert MOE_ROWS % MOE_SUB == 0 and MOE_SUB % MOE_BLOCK == 0
        carry0 = (jnp.zeros((MOE_SUB, MOE_TF), BF16), jnp.int32(MOE_ROWS),
                  lax.rem(i * MOE_NF, MOE_WD_SLOTS), jnp.int32(0))
        down(*lax.fori_loop(0, MOE_NF, tile, carry0))

    def drain_pending_store():
        for_blocks(pend_ref[1], lambda b: out_copy(pend_ref[0], b).wait())
        pend_ref[1] = 0

    @pl.when(nb > 0)
    def _():
        def pack(b):
            r0 = pl.multiple_of(b * MOE_BLOCK, MOE_BLOCK)
            y = acc_ref[pl.ds(r0, MOE_BLOCK), :]
            pkout_ref[pl.ds(r0, MOE_BLOCK), :] = _pack_halves(y[:, :HALF], y[:, HALF:])

        drain_pending_store()
        for_blocks(nb, pack)
        for_blocks(nb, lambda b: out_copy(start, b).start())
        pend_ref[0] = start
        pend_ref[1] = nb

    @pl.when(i == MOE_ITEMS - 1)
    def _():
        drain_pending_store()
        pkout_ref[:MOE_BLOCK, :] = jnp.zeros((MOE_BLOCK, HALF), U32)

        def tail_copy(b):
            return pltpu.make_async_copy(
                pkout_ref.at[pl.ds(0, MOE_BLOCK)],
                ys_hbm.at[pl.ds(pl.multiple_of(b * MOE_BLOCK, MOE_BLOCK), MOE_BLOCK)], sem_out)

        def tail_blocks(fn):
            def body(b, _):
                fn(b)
                return 0
            lax.fori_loop(tail_ref[0], n_out_blocks, body, 0)

        tail_blocks(lambda b: tail_copy(b).start())
        tail_blocks(lambda b: tail_copy(b).wait())


def _moe_ffn(item_e, item_start, item_nb, tail_blk, xs, w_gu, b_gu, w_d, b_d, n_rows):
    by_expert = lambda i, ie, is_, nb, tail: (ie[i], 0, 0)
    grid_spec = pltpu.PrefetchScalarGridSpec(
        num_scalar_prefetch=4,
        grid=(MOE_ITEMS,),
        in_specs=[pl.BlockSpec(memory_space=pl.ANY),
                  pl.BlockSpec(memory_space=pl.ANY),
                  pl.BlockSpec(memory_space=pl.ANY),
                  pl.BlockSpec((None, 2 * MOE_NF, MOE_TF), by_expert),
                  pl.BlockSpec((None, 1, D_MODEL), by_expert)],
        out_specs=pl.BlockSpec(memory_space=pl.ANY),
        scratch_shapes=[pltpu.VMEM((MOE_ROWS, HALF), U32),
                        pltpu.VMEM((MOE_ROWS, HALF), U32),
                        pltpu.VMEM((MOE_ROWS, D_MODEL), BF16),
                        pltpu.VMEM((MOE_ROWS + MOE_SUB, D_MODEL), F32),
                        pltpu.VMEM((MOE_GU_SLOTS, D_MODEL, MOE_TF), F32),
                        pltpu.VMEM((MOE_GU_SLOTS, D_MODEL, MOE_TF), F32),
                        pltpu.VMEM((MOE_WD_SLOTS, MOE_TF, D_MODEL), F32),
                        pltpu.SMEM((2,), jnp.int32),
                        pltpu.SemaphoreType.DMA(()),
                        pltpu.SemaphoreType.DMA(()),
                        pltpu.SemaphoreType.DMA((MOE_GU_SLOTS + MOE_WD_SLOTS,))])
    return pl.pallas_call(
        _moe_kernel,
        grid_spec=grid_spec,
        out_shape=jax.ShapeDtypeStruct((n_rows, HALF), U32),
        compiler_params=_cparams(1),
        name="moe_ffn",
    )(item_e, item_start, item_nb, tail_blk, xs, w_gu, w_d, b_gu, b_d)


def _combine_kernel(y0_ref, y1_ref, y2_ref, y3_ref, gate_ref, h2_ref, g_ref, b_ref, o_ref):
    h2 = h2_ref[...]
    lo = DEEPNORM_ALPHA * h2[:, :HALF]
    hi = DEEPNORM_ALPHA * h2[:, HALF:]
    for k, y_ref in enumerate((y0_ref, y1_ref, y2_ref, y3_ref)):
        ylo, yhi = _unpack_halves(y_ref[...])
        gk = gate_ref[:, k:k + 1]
        lo = lo + gk * ylo
        hi = hi + gk * yhi
    mu = (jnp.sum(lo, -1, keepdims=True) + jnp.sum(hi, -1, keepdims=True)) * (1.0 / D_MODEL)
    lo = lo - mu
    hi = hi - mu
    var = (jnp.sum(lo * lo, -1, keepdims=True) + jnp.sum(hi * hi, -1, keepdims=True)) * (1.0 / D_MODEL)
    inv = lax.rsqrt(var + LN_EPS)
    o_ref[:, :HALF] = lo * inv * g_ref[:, :HALF] + b_ref[:, :HALF]
    o_ref[:, HALF:] = hi * inv * g_ref[:, HALF:] + b_ref[:, HALF:]


def _combine(yg, gates, h2, g, b, tm=256):
    t, d = h2.shape
    full = lambda a: pl.BlockSpec(a.shape, lambda i: (0,) * a.ndim)
    return pl.pallas_call(
        _combine_kernel,
        grid=(t // tm,),
        in_specs=[pl.BlockSpec((None, tm, HALF), functools.partial(lambda k, i: (k, i, 0), k))
                  for k in range(TOP_K)] + [
                  pl.BlockSpec((tm, LANES), lambda i: (i, 0)),
                  pl.BlockSpec((tm, d), lambda i: (i, 0)), full(g), full(b)],
        out_specs=pl.BlockSpec((tm, d), lambda i: (i, 0)),
        out_shape=jax.ShapeDtypeStruct((t, d), F32),
        compiler_params=_cparams(1),
        name="combine_ln",
    )(yg, yg, yg, yg, gates, h2, g, b)


def _routing_tables(sel, counts_f):
    t = sel.shape[1]
    idx = sel[:TOP_K].astype(jnp.int32)
    pos = sel[TOP_K:].astype(jnp.int32)
    counts = counts_f[0, :N_EXPERTS].astype(jnp.int32)
    padded = ((counts + MOE_BLOCK - 1) // MOE_BLOCK) * MOE_BLOCK
    pends = jnp.cumsum(padded)
    pstarts = pends - padded
    experts = jnp.arange(N_EXPERTS, dtype=jnp.int32)[:, None, None]
    first_row = jnp.sum(jnp.where(idx[None] == experts, pstarts[:, None, None], 0), axis=0)
    dest = (first_row + pos).reshape(-1)
    n_rows = t * TOP_K + N_EXPERTS * MOE_BLOCK
    j = jnp.arange(MOE_BLOCK, dtype=jnp.int32)[None, :]
    is_pad = j < (padded - counts)[:, None]
    n_after = jnp.cumsum((~is_pad).reshape(-1).astype(jnp.int32)).reshape(is_pad.shape) - 1
    zero_rows = jnp.where(is_pad, (pstarts + counts)[:, None] + j, pends[-1] + n_after)
    row_of = jnp.concatenate([dest, zero_rows.reshape(-1)])

    nchunk = (padded + MOE_ROWS - 1) // MOE_ROWS
    cend = jnp.cumsum(nchunk)
    cstart = cend - nchunk
    total = cend[-1]
    it = jnp.arange(MOE_ITEMS, dtype=jnp.int32)
    valid = it < total
    it_c = jnp.minimum(it, total - 1)
    ie = jnp.searchsorted(cend, it_c, side="right").astype(jnp.int32)
    c = it_c - cstart[ie]
    istart = (pstarts[ie] + c * MOE_ROWS).astype(jnp.int32)
    inb = jnp.where(valid, jnp.minimum(MOE_ROWS, padded[ie] - c * MOE_ROWS) // MOE_BLOCK, 0)
    tail_blk = (pends[-1] // MOE_BLOCK).astype(jnp.int32).reshape(1)
    return dest, row_of, ie, istart, inb.astype(jnp.int32), tail_blk, n_rows


def kernel(x, positions, ln_in_g, ln_in_b, w_in, q_a_norm, w_q_b, kv_a_norm, w_kv_b, sb_out_norm,
           mla_out_norm, w_o, ln_mix_g, ln_mix_b, w_router, b_router, w_gate_up, b_gate_up, w_down,
           b_down, ln_ffn_g, ln_ffn_b):
    bsz, seq, d = x.shape
    t = bsz * seq
    x2 = x.reshape(t, d)
    row = lambda v: v.reshape(1, -1)

    w_in0 = w_in[0]
    w_in_b = w_in0.astype(BF16)
    w_lat = jnp.pad(w_in_b[:, 3 * SB_WIDTH:], ((0, 0), (0, LANES - MLA_ROPE)))
    wq = w_q_b[0].reshape(MLA_Q_RANK, MLA_HEADS, MLA_NOPE + MLA_ROPE)
    w_qn = wq[:, :, :MLA_NOPE].reshape(MLA_Q_RANK, -1).astype(BF16)
    w_qp = jnp.pad(wq[:, :, MLA_NOPE:], ((0, 0), (0, 0), (0, LANES - MLA_ROPE))
                   ).reshape(MLA_Q_RANK, -1).astype(BF16)
    wkv = w_kv_b[0].reshape(MLA_KV_RANK, MLA_HEADS, MLA_NOPE + MLA_V)
    w_kn = wkv[:, :, :MLA_NOPE].reshape(MLA_KV_RANK, -1).astype(BF16)
    w_v = wkv[:, :, MLA_NOPE:].reshape(MLA_KV_RANK, -1).astype(BF16)
    w_r32 = jnp.pad(w_router[0], ((0, 0), (0, LANES - N_EXPERTS)))
    w_r_hi = w_r32.astype(BF16)
    w_r = jnp.concatenate([w_r_hi, (w_r32 - w_r_hi.astype(F32)).astype(BF16)], axis=1)
    b_r = jnp.pad(b_router[0], (0, LANES - N_EXPERTS)).reshape(1, LANES)
    half = MLA_ROPE // 2
    inv_freq = ROPE_THETA ** (-(jnp.arange(half, dtype=F32) * 2.0 / MLA_ROPE))
    invf = jnp.concatenate([inv_freq, inv_freq, jnp.zeros((LANES - MLA_ROPE,), F32)]).reshape(1, LANES)
    pos_f = positions.reshape(t, 1).astype(F32)

    for l in range(DEPTH):
        u_sb, h_b = _ln_inproj(x2, row(ln_in_g), row(ln_in_b), w_in_b, 3 * SB_WIDTH)
        q_m, k_m, v_m = _mla_prep(h_b, pos_f, invf, w_lat, row(q_a_norm[l]), row(kv_a_norm[l]),
                                  w_qn, w_qp, w_kn, w_v)
        sb_o = _sb_attention(u_sb.reshape(bsz, seq, -1)).reshape(t, -1)
        mla_o = _mla_attention(q_m.reshape(bsz, seq, -1), k_m.reshape(bsz, seq, -1),
                               v_m.reshape(bsz, seq, -1)).reshape(t, -1)
        h2, h2p, sel, gates, counts_f = _mix_router(
            sb_o, mla_o, x2, row(ln_in_g), row(ln_in_b), row(sb_out_norm[l]), row(mla_out_norm[l]),
            w_o[l].astype(BF16), row(ln_mix_g[l]), row(ln_mix_b[l]), w_r, b_r)
        dest, row_of, ie, istart, inb, tail_blk, n_rows = _routing_tables(sel, counts_f)
        xs = _scatter_rows(h2p, TOP_K, row_of, n_rows)
        ys = _moe_ffn(ie, istart, inb, tail_blk, xs, w_gate_up[l],
                      b_gate_up[l].reshape(N_EXPERTS, 2 * MOE_NF, MOE_TF),
                      w_down[l], b_down[l].reshape(N_EXPERTS, 1, -1), n_rows)
        yg = _gather_rows(ys, dest).reshape(TOP_K, t, HALF)
        out = _combine(yg, gates, h2, row(ln_ffn_g[l]), row(ln_ffn_b[l]))
    return out.reshape(bsz, seq, d)
```

```python
import functools

import jax
import jax.numpy as jnp
from jax import lax
from jax.experimental import pallas as pl
from jax.experimental.pallas import tpu as pltpu

D_MODEL = 2048
DEPTH = 1
CHUNK = 64
SB_HEADS = 8
HEAD_DIM = 128
SB_WIDTH = SB_HEADS * HEAD_DIM
MLA_HEADS = 8
MLA_Q_RANK = 512
MLA_KV_RANK = 256
MLA_NOPE = 128
MLA_ROPE = 64
MLA_V = 128
MLA_WIDTH = MLA_HEADS * MLA_V
MLA_QK_PAD = 256
ROPE_THETA = 10000.0
N_EXPERTS = 32
TOP_K = 4
D_FF = 2048
SWIGLU_LIMIT = 7.0
SWIGLU_ALPHA = 1.702
MOE_BLOCK = 128
LN_EPS = 1e-5
RMS_EPS = 1e-6
DEEPNORM_ALPHA = (2 * DEPTH) ** 0.25

LANES = 128
SUBLANES = 8
HALF = D_MODEL // 2
VMEM_LIMIT_BYTES = 56 * 1024 * 1024

BF16 = jnp.bfloat16
F32 = jnp.float32
U32 = jnp.uint32


def _cparams(n_axes, vmem=None, flags=None):
    return pltpu.CompilerParams(dimension_semantics=("arbitrary",) * n_axes,
                                vmem_limit_bytes=vmem or VMEM_LIMIT_BYTES, flags=flags)


def _layer_norm(x, g, b):
    mu = jnp.mean(x, -1, keepdims=True)
    xc = x - mu
    var = jnp.mean(xc * xc, -1, keepdims=True)
    return xc * lax.rsqrt(var + LN_EPS) * g + b


def _rms_norm(x, g):
    ms = jnp.mean(x * x, -1, keepdims=True)
    return x * lax.rsqrt(ms + RMS_EPS) * g


def _pack_halves(lo_f32, hi_f32):
    lo = lax.bitcast_convert_type(lo_f32.astype(BF16).astype(F32), U32)
    hi = lax.bitcast_convert_type(hi_f32.astype(BF16).astype(F32), U32)
    return lax.shift_right_logical(lo, U32(16)) | (hi & U32(0xFFFF0000))


def _unpack_halves(w):
    lo = lax.bitcast_convert_type(lax.shift_left(w, U32(16)), F32)
    hi = lax.bitcast_convert_type(w & U32(0xFFFF0000), F32)
    return lo, hi


def _ln_inproj_kernel(x_ref, g_ref, b_ref, w_ref, u_ref, h_ref):
    h = _layer_norm(x_ref[...], g_ref[...], b_ref[...]).astype(BF16)
    h_ref[...] = h
    u_ref[...] = jnp.dot(h, w_ref[...], preferred_element_type=F32).astype(BF16)


def _ln_inproj(x2, g, b, w_sb, n, tm=512):
    t, d = x2.shape
    return pl.pallas_call(
        _ln_inproj_kernel,
        grid=(t // tm,),
        in_specs=[pl.BlockSpec((tm, d), lambda i: (i, 0)),
                  pl.BlockSpec((1, d), lambda i: (0, 0)),
                  pl.BlockSpec((1, d), lambda i: (0, 0)),
                  pl.BlockSpec((d, n), lambda i: (0, 0))],
        out_specs=[pl.BlockSpec((tm, n), lambda i: (i, 0)),
                   pl.BlockSpec((tm, d), lambda i: (i, 0))],
        out_shape=[jax.ShapeDtypeStruct((t, n), BF16), jax.ShapeDtypeStruct((t, d), BF16)],
        compiler_params=_cparams(1),
        name="ln_inproj",
    )(x2, g, b, w_sb)


def _rope_slab(x, cos, sin_neg, sin_pos):
    return (x * cos + pltpu.roll(x, LANES - MLA_ROPE // 2, 1) * sin_neg
            + pltpu.roll(x, MLA_ROPE // 2, 1) * sin_pos)


def _mla_prep_kernel(h_ref, pos_ref, invf_ref, wl_ref, qg_ref, kg_ref, wqn_ref, wqp_ref,
                     wkn_ref, wv_ref, q_ref, k_ref, v_ref):
    lat = jnp.dot(h_ref[...], wl_ref[...], preferred_element_type=F32)
    qn = _rms_norm(lat[:, :MLA_Q_RANK], qg_ref[...]).astype(BF16)
    kvn = _rms_norm(lat[:, MLA_Q_RANK:MLA_Q_RANK + MLA_KV_RANK], kg_ref[...]).astype(BF16)
    k_rope = lat[:, MLA_Q_RANK + MLA_KV_RANK:]

    ang = pos_ref[...] * invf_ref[...]
    cos = jnp.cos(ang)
    sin = jnp.sin(ang)
    lane = lax.broadcasted_iota(jnp.int32, ang.shape, 1)
    sin_neg = jnp.where(lane < MLA_ROPE // 2, -sin, 0.0)
    sin_pos = jnp.where((lane >= MLA_ROPE // 2) & (lane < MLA_ROPE), sin, 0.0)

    q_nope = jnp.dot(qn, wqn_ref[...], preferred_element_type=F32)
    q_pe = jnp.dot(qn, wqp_ref[...], preferred_element_type=F32)
    k_nope = jnp.dot(kvn, wkn_ref[...], preferred_element_type=F32)
    v_ref[...] = jnp.dot(kvn, wv_ref[...], preferred_element_type=F32).astype(BF16)
    k_pe = _rope_slab(k_rope, cos, sin_neg, sin_pos).astype(BF16)
    for hd in range(MLA_HEADS):
        c0 = hd * MLA_QK_PAD
        s0 = hd * LANES
        q_ref[:, c0:c0 + LANES] = q_nope[:, s0:s0 + LANES].astype(BF16)
        q_ref[:, c0 + LANES:c0 + 2 * LANES] = _rope_slab(
            q_pe[:, s0:s0 + LANES], cos, sin_neg, sin_pos).astype(BF16)
        k_ref[:, c0:c0 + LANES] = k_nope[:, s0:s0 + LANES].astype(BF16)
        k_ref[:, c0 + LANES:c0 + 2 * LANES] = k_pe


def _mla_prep(h_b, pos_f, invf, w_lat, qg, kg, w_qn, w_qp, w_kn, w_v, tm=512):
    t, d = h_b.shape
    full = lambda a: pl.BlockSpec(a.shape, lambda i: (0,) * a.ndim)
    wq = MLA_HEADS * MLA_QK_PAD
    return pl.pallas_call(
        _mla_prep_kernel,
        grid=(t // tm,),
        in_specs=[pl.BlockSpec((tm, d), lambda i: (i, 0)),
                  pl.BlockSpec((tm, 1), lambda i: (i, 0)),
                  full(invf), full(w_lat), full(qg), full(kg), full(w_qn), full(w_qp),
                  full(w_kn), full(w_v)],
        out_specs=[pl.BlockSpec((tm, wq), lambda i: (i, 0)),
                   pl.BlockSpec((tm, wq), lambda i: (i, 0)),
                   pl.BlockSpec((tm, MLA_WIDTH), lambda i: (i, 0))],
        out_shape=[jax.ShapeDtypeStruct((t, wq), BF16), jax.ShapeDtypeStruct((t, wq), BF16),
                   jax.ShapeDtypeStruct((t, MLA_WIDTH), BF16)],
        compiler_params=_cparams(1),
        name="mla_prep",
    )(h_b, pos_f, invf, w_lat, qg, kg, w_qn, w_qp, w_kn, w_v)


def _sb_attn_kernel(q_ref, k_ref, v_ref, o_ref, *, tq, tk, heads, scale):
    i = pl.program_id(2)
    t_idx = i * tq + lax.broadcasted_iota(jnp.int32, (tq, 1), 0)
    r_i = lax.broadcasted_iota(jnp.int32, (tk, tk), 0)
    c_i = lax.broadcasted_iota(jnp.int32, (tk, tk), 1)
    suffix = (r_i >= c_i).astype(BF16)
    suffix2 = jnp.concatenate([suffix, suffix], axis=0)
    nkb = (i + 1) * (tq // tk)

    def key_block(j, carry, diagonal):
        k0 = pl.multiple_of(j * tk, tk)
        if diagonal:
            past = (k0 + lax.broadcasted_iota(jnp.int32, (1, tk), 1)) < t_idx
        zs, splits = [], []
        for hd in range(heads):
            cols = slice(hd * HEAD_DIM, (hd + 1) * HEAD_DIM)
            z = lax.dot_general(q_ref[:, cols], k_ref[pl.ds(k0, tk), cols], (((1,), (1,)), ((), ())),
                                preferred_element_type=F32) * scale
            sp = jnp.maximum(z, 0.0) + jnp.log(1.0 + jnp.exp(-jnp.abs(z)))
            if diagonal:
                sp = jnp.where(past, sp, 0.0)
            hi = sp.astype(BF16)
            lo = (sp - hi.astype(F32)).astype(BF16)
            zs.append(z)
            splits.append(jnp.concatenate([hi, lo], axis=1))
        incl_all = jnp.dot(jnp.concatenate(splits, axis=0), suffix2, preferred_element_type=F32)
        out = []
        for hd in range(heads):
            c, acc = carry[hd]
            cols = slice(hd * HEAD_DIM, (hd + 1) * HEAD_DIM)
            incl = incl_all[hd * tq:(hd + 1) * tq]
            w = jnp.exp(zs[hd] - incl - c)
            if diagonal:
                w = jnp.where(past, w, 0.0)
            acc = acc + jnp.dot(w.astype(BF16), v_ref[pl.ds(k0, tk), cols],
                                preferred_element_type=F32)
            out.append((c + incl[:, 0:1], acc))
        return tuple(out)

    bpt = tq // tk

    def trip(p, carry, diagonal):
        for bk in range(bpt):
            carry = key_block(nkb - 1 - bpt * p - bk, carry, diagonal)
        return carry

    init = tuple((jnp.zeros((tq, 1), F32), jnp.zeros((tq, HEAD_DIM), F32)) for _ in range(heads))
    res = trip(0, init, True)
    res = lax.fori_loop(1, i + 1, functools.partial(trip, diagonal=False), res)
    for hd in range(heads):
        o_ref[:, hd * HEAD_DIM:(hd + 1) * HEAD_DIM] = res[hd][1].astype(o_ref.dtype)


def _sb_attention(u_sb3, tq=256, tk=128, heads=SB_HEADS):
    b, s, _ = u_sb3.shape
    assert tq % tk == 0 and SB_HEADS % heads == 0
    groups = SB_HEADS // heads
    w = heads * HEAD_DIM
    kern = functools.partial(_sb_attn_kernel, tq=tq, tk=tk, heads=heads, scale=HEAD_DIM ** -0.5)
    return pl.pallas_call(
        kern,
        grid=(b, groups, s // tq),
        in_specs=[pl.BlockSpec((None, tq, w), lambda bb, g, i: (bb, i, g)),
                  pl.BlockSpec((None, s, w), lambda bb, g, i: (bb, 0, groups + g)),
                  pl.BlockSpec((None, s, w), lambda bb, g, i: (bb, 0, 2 * groups + g))],
        out_specs=pl.BlockSpec((None, tq, w), lambda bb, g, i: (bb, i, g)),
        out_shape=jax.ShapeDtypeStruct((b, s, SB_WIDTH), BF16),
        compiler_params=_cparams(3),
        name="sb_attention",
    )(u_sb3, u_sb3, u_sb3)


def _mla_attn_kernel(q_ref, k_ref, v_ref, o_ref, *, tq, tk, heads, scale):
    i = pl.program_id(2)
    t_chunk = lax.shift_right_logical(
        i * tq + lax.broadcasted_iota(jnp.int32, (tq, 1), 0), CHUNK.bit_length() - 1)

    def body(j, carry, diagonal):
        k0 = pl.multiple_of(j * tk, tk)
        if diagonal:
            s_chunk = lax.shift_right_logical(
                k0 + lax.broadcasted_iota(jnp.int32, (1, tk), 1), CHUNK.bit_length() - 1)
            visible = s_chunk <= t_chunk

        def qk_scores(hd):
            qk = slice(hd * MLA_QK_PAD, (hd + 1) * MLA_QK_PAD)
            return lax.dot_general(q_ref[:, qk], k_ref[pl.ds(k0, tk), qk],
                                   (((1,), (1,)), ((), ())), preferred_element_type=F32)

        out = []
        nxt = qk_scores(0)
        for hd in range(heads):
            m, l, acc = carry[hd]
            s = nxt
            if hd + 1 < heads:
                nxt = qk_scores(hd + 1)
            s = s * scale
            if diagonal:
                s = jnp.where(visible, s, -1e30)
            m_new = jnp.maximum(m, jnp.max(s, -1, keepdims=True))
            alpha = jnp.exp(m - m_new)
            p = jnp.exp(s - m_new)
            l = alpha * l + jnp.sum(p, -1, keepdims=True)
            vs = v_ref[pl.ds(k0, tk), hd * MLA_V:(hd + 1) * MLA_V]
            acc = alpha * acc + jnp.dot(p.astype(BF16), vs, preferred_element_type=F32)
            out.append((m_new, l, acc))
        return tuple(out)

    init = tuple((jnp.full((tq, 1), -1e30, F32), jnp.zeros((tq, 1), F32),
                  jnp.zeros((tq, MLA_V), F32)) for _ in range(heads))
    res = lax.fori_loop(0, i, functools.partial(body, diagonal=False), init)
    res = body(i, res, True)
    for hd in range(heads):
        _, l, acc = res[hd]
        o_ref[:, hd * MLA_V:(hd + 1) * MLA_V] = (acc / l).astype(o_ref.dtype)


def _mla_attention(q3, k3, v3, tq=256, tk=256, heads=MLA_HEADS):
    b, s, _ = q3.shape
    assert MLA_HEADS % heads == 0 and tq == tk and tk % CHUNK == 0
    kern = functools.partial(_mla_attn_kernel, tq=tq, tk=tk, heads=heads,
                             scale=(MLA_NOPE + MLA_ROPE) ** -0.5)
    return pl.pallas_call(
        kern,
        grid=(b, MLA_HEADS // heads, s // tq),
        in_specs=[pl.BlockSpec((None, tq, heads * MLA_QK_PAD), lambda bb, g, i: (bb, i, g)),
                  pl.BlockSpec((None, s, heads * MLA_QK_PAD), lambda bb, g, i: (bb, 0, g)),
                  pl.BlockSpec((None, s, heads * MLA_V), lambda bb, g, i: (bb, 0, g))],
        out_specs=pl.BlockSpec((None, tq, heads * MLA_V), lambda bb, g, i: (bb, i, g)),
        out_shape=jax.ShapeDtypeStruct((b, s, MLA_WIDTH), BF16),
        compiler_params=_cparams(3),
        name="mla_attention",
    )(q3, k3, v3)


def _mix_router_kernel(sb_ref, mla_ref, x_ref, lig_ref, lib_ref, sg_ref, mg_ref, wo_ref,
                       lmg_ref, lmb_ref, wr_ref, br_ref,
                       h2_ref, h2p_ref, sel_ref, gate_ref, cnt_ref, run_ref, *, tm):
    step = pl.program_id(0)

    @pl.when(step == 0)
    def _():
        run_ref[...] = jnp.zeros_like(run_ref)

    a = _rms_norm(sb_ref[...].astype(F32), sg_ref[...]).astype(BF16)
    bm = _rms_norm(mla_ref[...].astype(F32), mg_ref[...]).astype(BF16)
    mix = (jnp.dot(a, wo_ref[:SB_WIDTH, :], preferred_element_type=F32)
           + jnp.dot(bm, wo_ref[SB_WIDTH:, :], preferred_element_type=F32))
    h = _layer_norm(x_ref[...], lig_ref[...], lib_ref[...])
    h2 = _layer_norm(DEEPNORM_ALPHA * h + mix, lmg_ref[...], lmb_ref[...])
    h2_ref[...] = h2
    h2p_ref[...] = _pack_halves(h2[:, :HALF], h2[:, HALF:])

    h2_hi = h2.astype(BF16)
    h2_lo = (h2 - h2_hi.astype(F32)).astype(BF16)
    hw = jnp.dot(h2_hi, wr_ref[...], preferred_element_type=F32)
    logits = (hw[:, :LANES] + hw[:, LANES:]
              + jnp.dot(h2_lo, wr_ref[:, :LANES], preferred_element_type=F32) + br_ref[...])
    lane = lax.broadcasted_iota(jnp.int32, (tm, LANES), 1)
    lane_f = lane.astype(F32)
    logits = jnp.where(lane < N_EXPERTS, logits, -jnp.inf)
    vals, hots = [], []
    sel = jnp.zeros((tm, LANES), F32)
    for k in range(TOP_K):
        m = jnp.max(logits, -1, keepdims=True)
        idx = jnp.min(jnp.where(logits == m, lane_f, float(LANES)), -1, keepdims=True)
        hot = lane_f == idx
        logits = jnp.where(hot, -jnp.inf, logits)
        vals.append(m)
        hots.append(hot)
        sel = jnp.where(lane == k, idx, sel)
    exps = [jnp.exp(v - vals[0]) for v in vals]
    denom = exps[0] + exps[1] + exps[2] + exps[3]
    gates = jnp.zeros((tm, LANES), F32)
    for k in range(TOP_K):
        gates = jnp.where(lane == k, exps[k] / denom, gates)
    gate_ref[...] = gates

    onehot = (hots[0] | hots[1] | hots[2] | hots[3]).astype(F32)
    r_i = lax.broadcasted_iota(jnp.int32, (tm, tm), 0)
    c_i = lax.broadcasted_iota(jnp.int32, (tm, tm), 1)
    before = (c_i < r_i).astype(BF16)
    rank = jnp.dot(before, onehot.astype(BF16), preferred_element_type=F32) + run_ref[...]
    for k in range(TOP_K):
        pos = jnp.sum(jnp.where(hots[k], rank, 0.0), -1, keepdims=True)
        sel = jnp.where(lane == TOP_K + k, pos, sel)
    sel_ref[...] = sel.T[:2 * TOP_K, :]
    run_ref[...] = run_ref[...] + jnp.sum(onehot, 0, keepdims=True)
    cnt_ref[...] = run_ref[...]


def _mix_router(sb_o, mla_o, x2, lig, lib, sg, mg, w_o, lmg, lmb, w_r, b_r, tm=512):
    t, d = x2.shape
    full = lambda a: pl.BlockSpec(a.shape, lambda i: (0,) * a.ndim)
    row = lambda w: pl.BlockSpec((tm, w), lambda i: (i, 0))
    return pl.pallas_call(
        functools.partial(_mix_router_kernel, tm=tm),
        grid=(t // tm,),
        in_specs=[row(SB_WIDTH), row(MLA_WIDTH), row(d), full(lig), full(lib), full(sg), full(mg),
                  full(w_o), full(lmg), full(lmb), full(w_r), full(b_r)],
        out_specs=[row(d), row(HALF), pl.BlockSpec((2 * TOP_K, tm), lambda i: (0, i)), row(LANES),
                   pl.BlockSpec((1, LANES), lambda i: (0, 0))],
        out_shape=[jax.ShapeDtypeStruct((t, d), F32), jax.ShapeDtypeStruct((t, HALF), U32),
                   jax.ShapeDtypeStruct((2 * TOP_K, t), F32), jax.ShapeDtypeStruct((t, LANES), F32),
                   jax.ShapeDtypeStruct((1, LANES), F32)],
        scratch_shapes=[pltpu.VMEM((1, LANES), F32)],
        compiler_params=_cparams(1),
        name="mix_router",
    )(sb_o, mla_o, x2, lig, lib, sg, mg, w_o, lmg, lmb, w_r, b_r)


ROWS_PER_STEP = 4096


def _wait_rows(like_hbm, sem, groups):
    pltpu.make_async_copy(like_hbm.at[pl.ds(0, groups)], like_hbm.at[pl.ds(0, groups)], sem).wait()


def _row_of(ref3, r):
    if isinstance(r, tuple):
        return ref3.at[r[0], pl.ds(r[1], 1), :]
    return ref3.at[lax.shift_right_logical(r, 3), pl.ds(r & (SUBLANES - 1), 1), :]


def _gather_rows_kernel(idx_ref, src_hbm, out_ref, sem, *, rows):
    def issue(g, _):
        r0 = pl.multiple_of(g * SUBLANES, SUBLANES)
        for k in range(SUBLANES):
            pltpu.make_async_copy(_row_of(src_hbm, idx_ref[0, 0, r0 + k]),
                                  _row_of(out_ref, (g, k)), sem).start()
        return 0

    lax.fori_loop(0, rows // SUBLANES, issue, 0)
    _wait_rows(src_hbm, sem, rows // SUBLANES)


def _scatter_rows_kernel(idx_ref, src_ref, out_hbm, zero_ref, sem, *, rows, src_steps):
    step = pl.program_id(0)

    @pl.when(step == 0)
    def _():
        zero_ref[...] = jnp.zeros_like(zero_ref)

    @pl.when(step < src_steps)
    def _():
        def issue(g, _):
            r0 = pl.multiple_of(g * SUBLANES, SUBLANES)
            for k in range(SUBLANES):
                pltpu.make_async_copy(_row_of(src_ref, (g, k)),
                                      _row_of(out_hbm, idx_ref[0, 0, r0 + k]), sem).start()
            return 0
        lax.fori_loop(0, rows // SUBLANES, issue, 0)

    @pl.when(step >= src_steps)
    def _():
        def issue(a, _):
            pltpu.make_async_copy(_row_of(zero_ref, (0, 0)),
                                  _row_of(out_hbm, idx_ref[0, 0, a]), sem).start()
            return 0
        lax.fori_loop(0, rows, issue, 0, unroll=8)

    _wait_rows(out_hbm, sem, rows // SUBLANES)


def _gather_rows(src, idx):
    rows = ROWS_PER_STEP
    n, w = idx.shape[0], src.shape[1]
    assert n % rows == 0 and src.shape[0] >= rows and src.shape[0] % SUBLANES == 0
    out = pl.pallas_call(
        functools.partial(_gather_rows_kernel, rows=rows),
        grid=(n // rows,),
        in_specs=[pl.BlockSpec((1, 1, rows), lambda i: (i, 0, 0), memory_space=pltpu.SMEM),
                  pl.BlockSpec(memory_space=pl.ANY)],
        out_specs=pl.BlockSpec((rows // SUBLANES, SUBLANES, w), lambda i: (i, 0, 0)),
        out_shape=jax.ShapeDtypeStruct((n // SUBLANES, SUBLANES, w), src.dtype),
        scratch_shapes=[pltpu.SemaphoreType.DMA(())],
        compiler_params=_cparams(1),
        name="gather_rows",
    )(idx.reshape(n // rows, 1, rows), src.reshape(-1, SUBLANES, w))
    return out.reshape(n, w)


def _scatter_rows(src, passes, idx, out_rows):
    rows = ROWS_PER_STEP
    n, w = idx.shape[0], src.shape[1]
    src_blocks = src.shape[0] // rows
    assert src.shape[0] % rows == 0 and n % rows == 0 and n == out_rows
    src_steps = passes * src_blocks
    kern = functools.partial(_scatter_rows_kernel, rows=rows, src_steps=src_steps)

    def src_block(i):
        return jnp.minimum(i // passes, src_blocks - 1), 0

    def idx_block(i):
        return jnp.where(i < src_steps, lax.rem(i, passes) * src_blocks + i // passes, i), 0, 0

    out = pl.pallas_call(
        kern,
        grid=(n // rows,),
        in_specs=[pl.BlockSpec((1, 1, rows), idx_block, memory_space=pltpu.SMEM),
                  pl.BlockSpec((rows // SUBLANES, SUBLANES, w), lambda i: src_block(i) + (0,))],
        out_specs=pl.BlockSpec(memory_space=pl.ANY),
        out_shape=jax.ShapeDtypeStruct((out_rows // SUBLANES, SUBLANES, w), src.dtype),
        scratch_shapes=[pltpu.VMEM((1, SUBLANES, w), src.dtype), pltpu.SemaphoreType.DMA(())],
        compiler_params=_cparams(1),
        name="scatter_rows",
    )(idx.reshape(n // rows, 1, rows), src.reshape(-1, SUBLANES, w))
    return out.reshape(out_rows, w)


MOE_ROWS = 1536
MOE_SUB = 512
MOE_TF = 256
MOE_NF = D_FF // MOE_TF
MOE_ITEMS = N_EXPERTS + (N_EXPERTS * (MOE_BLOCK - 1) + 8192 * TOP_K) // MOE_ROWS + 1


MOE_GU_SLOTS = 2
MOE_WD_SLOTS = 3


def _moe_kernel(ie_ref, is_ref, nb_ref, tail_ref, xs_hbm, wgu_hbm, wd_hbm, bgu_ref, bd_ref,
                ys_hbm, pkin_ref, pkout_ref, x_ref, acc_ref, wg_buf, wu_buf, wd_buf,
                pend_ref, sem_in, sem_out, sem_w):
    i = pl.program_id(0)
    nb = nb_ref[i]
    start = is_ref[i]
    n_out_blocks = ys_hbm.shape[0] // MOE_BLOCK

    def weight_copies(item, f):
        e = ie_ref[item]
        g = item * MOE_NF + f
        gs = lax.rem(g, MOE_GU_SLOTS)
        ws = lax.rem(g, MOE_WD_SLOTS)
        c0 = pl.multiple_of(f * MOE_TF, MOE_TF)
        return (pltpu.make_async_copy(wgu_hbm.at[e, :, pl.ds(c0, MOE_TF)], wg_buf.at[gs],
                                      sem_w.at[gs]),
                pltpu.make_async_copy(wgu_hbm.at[e, :, pl.ds(D_FF + c0, MOE_TF)], wu_buf.at[gs],
                                      sem_w.at[gs]),
                pltpu.make_async_copy(wd_hbm.at[e, pl.ds(c0, MOE_TF), :], wd_buf.at[ws],
                                      sem_w.at[MOE_GU_SLOTS + ws]))

    def vm_block(ref, b):
        return ref.at[pl.ds(pl.multiple_of(b * MOE_BLOCK, MOE_BLOCK), MOE_BLOCK)]

    def hbm_block(ref, first_row, b):
        return ref.at[pl.ds(pl.multiple_of(first_row + b * MOE_BLOCK, MOE_BLOCK), MOE_BLOCK)]

    def in_copy(first_row, b):
        return pltpu.make_async_copy(hbm_block(xs_hbm, first_row, b), vm_block(pkin_ref, b), sem_in)

    def out_copy(first_row, b):
        return pltpu.make_async_copy(vm_block(pkout_ref, b), hbm_block(ys_hbm, first_row, b), sem_out)

    def for_blocks(n, fn):
        def body(b, _):
            fn(b)
            return 0
        lax.fori_loop(0, n, body, 0)

    def next_item_valid():
        return (i + 1 < MOE_ITEMS) & (nb_ref[jnp.minimum(i + 1, MOE_ITEMS - 1)] > 0)

    @pl.when(i == 0)
    def _():
        x_ref[...] = jnp.zeros_like(x_ref)
        acc_ref[...] = jnp.zeros_like(acc_ref)
        pend_ref[0] = 0
        pend_ref[1] = 0
        for_blocks(nb, lambda b: in_copy(start, b).start())
        for c in weight_copies(0, 0):
            c.start()

    @pl.when(nb > 0)
    def _():
        def unpack(b):
            r0 = pl.multiple_of(b * MOE_BLOCK, MOE_BLOCK)
            lo, hi = _unpack_halves(pkin_ref[pl.ds(r0, MOE_BLOCK), :])
            x_ref[pl.ds(r0, MOE_BLOCK), :HALF] = lo.astype(BF16)
            x_ref[pl.ds(r0, MOE_BLOCK), HALF:] = hi.astype(BF16)

        for_blocks(nb, lambda b: in_copy(start, b).wait())
        for_blocks(nb, unpack)

        @pl.when(next_item_valid())
        def _():
            nxt = jnp.minimum(i + 1, MOE_ITEMS - 1)
            for_blocks(nb_ref[nxt], lambda b: in_copy(is_ref[nxt], b).start())

        n_sub = (nb * MOE_BLOCK) // MOE_SUB
        n_tail = nb - n_sub * (MOE_SUB // MOE_BLOCK)

        def gate_up(f, r0, rows, gs):
            x = x_ref[pl.ds(pl.multiple_of(r0, MOE_BLOCK), rows), :]
            g = (jnp.dot(x, wg_buf[gs].astype(BF16), preferred_element_type=F32)
                 + bgu_ref[pl.ds(f, 1), :])
            u = (jnp.dot(x, wu_buf[gs].astype(BF16), preferred_element_type=F32)
                 + bgu_ref[pl.ds(MOE_NF + f, 1), :])
            g = jnp.minimum(g, SWIGLU_LIMIT)
            u = jnp.clip(u, -SWIGLU_LIMIT, SWIGLU_LIMIT)
            return ((u + 1.0) * (g * jax.nn.sigmoid(SWIGLU_ALPHA * g))).astype(BF16)

        def down(act, r0, ws, first):
            rows = act.shape[0]
            r0 = pl.multiple_of(r0, MOE_BLOCK)
            y = jnp.dot(act, wd_buf[ws].astype(BF16), preferred_element_type=F32)
            prev = jnp.where(first == 1, bd_ref[...], acc_ref[pl.ds(r0, rows), :])
            acc_ref[pl.ds(r0, rows), :] = prev + y

        def tile(f, carry):
            for c in weight_copies(i, f):
                c.wait()

            @pl.when(f + 1 < MOE_NF)
            def _():
                for c in weight_copies(i, f + 1):
                    c.start()

            @pl.when((f + 1 == MOE_NF) & next_item_valid())
            def _():
                for c in weight_copies(i + 1, 0):
                    c.start()

            g = i * MOE_NF + f
            gs = lax.rem(g, MOE_GU_SLOTS)
            ws = lax.rem(g, MOE_WD_SLOTS)
            first = jnp.asarray(f == 0, jnp.int32)

            def sub(s, c):
                act, r_prev, ws_prev, first_prev = c
                down(act, r_prev, ws_prev, first_prev)
                return gate_up(f, s * MOE_SUB, MOE_SUB, gs), s * MOE_SUB, ws, first

            carry = lax.fori_loop(0, n_sub, sub, carry)

            def tail(t, _):
                r0 = n_sub * MOE_SUB + t * MOE_BLOCK
                down(gate_up(f, r0, MOE_BLOCK, gs), r0, ws, first)
                return 0

            lax.fori_loop(0, n_tail, tail, 0)
            return carry

        assert MOE_ROWS % MOE_SUB == 0 and MOE_SUB % MOE_BLOCK == 0
        carry0 = (jnp.zeros((MOE_SUB, MOE_TF), BF16), jnp.int32(MOE_ROWS),
                  lax.rem(i * MOE_NF, MOE_WD_SLOTS), jnp.int32(0))
        down(*lax.fori_loop(0, MOE_NF, tile, carry0))

    def drain_pending_store():
        for_blocks(pend_ref[1], lambda b: out_copy(pend_ref[0], b).wait())
        pend_ref[1] = 0

    @pl.when(nb > 0)
    def _():
        def pack(b):
            r0 = pl.multiple_of(b * MOE_BLOCK, MOE_BLOCK)
            y = acc_ref[pl.ds(r0, MOE_BLOCK), :]
            pkout_ref[pl.ds(r0, MOE_BLOCK), :] = _pack_halves(y[:, :HALF], y[:, HALF:])

        drain_pending_store()
        for_blocks(nb, pack)
        for_blocks(nb, lambda b: out_copy(start, b).start())
        pend_ref[0] = start
        pend_ref[1] = nb

    @pl.when(i == MOE_ITEMS - 1)
    def _():
        drain_pending_store()
        pkout_ref[:MOE_BLOCK, :] = jnp.zeros((MOE_BLOCK, HALF), U32)

        def tail_copy(b):
            return pltpu.make_async_copy(
                pkout_ref.at[pl.ds(0, MOE_BLOCK)],
                ys_hbm.at[pl.ds(pl.multiple_of(b * MOE_BLOCK, MOE_BLOCK), MOE_BLOCK)], sem_out)

        def tail_blocks(fn):
            def body(b, _):
                fn(b)
                return 0
            lax.fori_loop(tail_ref[0], n_out_blocks, body, 0)

        tail_blocks(lambda b: tail_copy(b).start())
        tail_blocks(lambda b: tail_copy(b).wait())


def _moe_ffn(item_e, item_start, item_nb, tail_blk, xs, w_gu, b_gu, w_d, b_d, n_rows):
    by_expert = lambda i, ie, is_, nb, tail: (ie[i], 0, 0)
    grid_spec = pltpu.PrefetchScalarGridSpec(
        num_scalar_prefetch=4,
        grid=(MOE_ITEMS,),
        in_specs=[pl.BlockSpec(memory_space=pl.ANY),
                  pl.BlockSpec(memory_space=pl.ANY),
                  pl.BlockSpec(memory_space=pl.ANY),
                  pl.BlockSpec((None, 2 * MOE_NF, MOE_TF), by_expert),
                  pl.BlockSpec((None, 1, D_MODEL), by_expert)],
        out_specs=pl.BlockSpec(memory_space=pl.ANY),
        scratch_shapes=[pltpu.VMEM((MOE_ROWS, HALF), U32),
                        pltpu.VMEM((MOE_ROWS, HALF), U32),
                        pltpu.VMEM((MOE_ROWS, D_MODEL), BF16),
                        pltpu.VMEM((MOE_ROWS + MOE_SUB, D_MODEL), F32),
                        pltpu.VMEM((MOE_GU_SLOTS, D_MODEL, MOE_TF), F32),
                        pltpu.VMEM((MOE_GU_SLOTS, D_MODEL, MOE_TF), F32),
                        pltpu.VMEM((MOE_WD_SLOTS, MOE_TF, D_MODEL), F32),
                        pltpu.SMEM((2,), jnp.int32),
                        pltpu.SemaphoreType.DMA(()),
                        pltpu.SemaphoreType.DMA(()),
                        pltpu.SemaphoreType.DMA((MOE_GU_SLOTS + MOE_WD_SLOTS,))])
    return pl.pallas_call(
        _moe_kernel,
        grid_spec=grid_spec,
        out_shape=jax.ShapeDtypeStruct((n_rows, HALF), U32),
        compiler_params=_cparams(1),
        name="moe_ffn",
    )(item_e, item_start, item_nb, tail_blk, xs, w_gu, w_d, b_gu, b_d)


def _combine_kernel(y0_ref, y1_ref, y2_ref, y3_ref, gate_ref, h2_ref, g_ref, b_ref, o_ref):
    h2 = h2_ref[...]
    lo = DEEPNORM_ALPHA * h2[:, :HALF]
    hi = DEEPNORM_ALPHA * h2[:, HALF:]
    for k, y_ref in enumerate((y0_ref, y1_ref, y2_ref, y3_ref)):
        ylo, yhi = _unpack_halves(y_ref[...])
        gk = gate_ref[:, k:k + 1]
        lo = lo + gk * ylo
        hi = hi + gk * yhi
    mu = (jnp.sum(lo, -1, keepdims=True) + jnp.sum(hi, -1, keepdims=True)) * (1.0 / D_MODEL)
    lo = lo - mu
    hi = hi - mu
    var = (jnp.sum(lo * lo, -1, keepdims=True) + jnp.sum(hi * hi, -1, keepdims=True)) * (1.0 / D_MODEL)
    inv = lax.rsqrt(var + LN_EPS)
    o_ref[:, :HALF] = lo * inv * g_ref[:, :HALF] + b_ref[:, :HALF]
    o_ref[:, HALF:] = hi * inv * g_ref[:, HALF:] + b_ref[:, HALF:]


def _combine(yg, gates, h2, g, b, tm=256):
    t, d = h2.shape
    full = lambda a: pl.BlockSpec(a.shape, lambda i: (0,) * a.ndim)
    return pl.pallas_call(
        _combine_kernel,
        grid=(t // tm,),
        in_specs=[pl.BlockSpec((None, tm, HALF), functools.partial(lambda k, i: (k, i, 0), k))
                  for k in range(TOP_K)] + [
                  pl.BlockSpec((tm, LANES), lambda i: (i, 0)),
                  pl.BlockSpec((tm, d), lambda i: (i, 0)), full(g), full(b)],
        out_specs=pl.BlockSpec((tm, d), lambda i: (i, 0)),
        out_shape=jax.ShapeDtypeStruct((t, d), F32),
        compiler_params=_cparams(1),
        name="combine_ln",
    )(yg, yg, yg, yg, gates, h2, g, b)


def _routing_tables(sel, counts_f):
    t = sel.shape[1]
    idx = sel[:TOP_K].astype(jnp.int32)
    pos = sel[TOP_K:].astype(jnp.int32)
    counts = counts_f[0, :N_EXPERTS].astype(jnp.int32)
    padded = ((counts + MOE_BLOCK - 1) // MOE_BLOCK) * MOE_BLOCK
    pends = jnp.cumsum(padded)
    pstarts = pends - padded
    experts = jnp.arange(N_EXPERTS, dtype=jnp.int32)[:, None, None]
    first_row = jnp.sum(jnp.where(idx[None] == experts, pstarts[:, None, None], 0), axis=0)
    dest = (first_row + pos).reshape(-1)
    n_rows = t * TOP_K + N_EXPERTS * MOE_BLOCK
    j = jnp.arange(MOE_BLOCK, dtype=jnp.int32)[None, :]
    is_pad = j < (padded - counts)[:, None]
    n_after = jnp.cumsum((~is_pad).reshape(-1).astype(jnp.int32)).reshape(is_pad.shape) - 1
    zero_rows = jnp.where(is_pad, (pstarts + counts)[:, None] + j, pends[-1] + n_after)
    row_of = jnp.concatenate([dest, zero_rows.reshape(-1)])

    nchunk = (padded + MOE_ROWS - 1) // MOE_ROWS
    cend = jnp.cumsum(nchunk)
    cstart = cend - nchunk
    total = cend[-1]
    it = jnp.arange(MOE_ITEMS, dtype=jnp.int32)
    valid = it < total
    it_c = jnp.minimum(it, total - 1)
    ie = jnp.searchsorted(cend, it_c, side="right").astype(jnp.int32)
    c = it_c - cstart[ie]
    istart = (pstarts[ie] + c * MOE_ROWS).astype(jnp.int32)
    inb = jnp.where(valid, jnp.minimum(MOE_ROWS, padded[ie] - c * MOE_ROWS) // MOE_BLOCK, 0)
    tail_blk = (pends[-1] // MOE_BLOCK).astype(jnp.int32).reshape(1)
    return dest, row_of, ie, istart, inb.astype(jnp.int32), tail_blk, n_rows


def kernel(x, positions, ln_in_g, ln_in_b, w_in, q_a_norm, w_q_b, kv_a_norm, w_kv_b, sb_out_norm,
           mla_out_norm, w_o, ln_mix_g, ln_mix_b, w_router, b_router, w_gate_up, b_gate_up, w_down,
           b_down, ln_ffn_g, ln_ffn_b):
    bsz, seq, d = x.shape
    t = bsz * seq
    x2 = x.reshape(t, d)
    row = lambda v: v.reshape(1, -1)

    w_in0 = w_in[0]
    w_in_b = w_in0.astype(BF16)
    w_lat = jnp.pad(w_in_b[:, 3 * SB_WIDTH:], ((0, 0), (0, LANES - MLA_ROPE)))
    wq = w_q_b[0].reshape(MLA_Q_RANK, MLA_HEADS, MLA_NOPE + MLA_ROPE)
    w_qn = wq[:, :, :MLA_NOPE].reshape(MLA_Q_RANK, -1).astype(BF16)
    w_qp = jnp.pad(wq[:, :, MLA_NOPE:], ((0, 0), (0, 0), (0, LANES - MLA_ROPE))
                   ).reshape(MLA_Q_RANK, -1).astype(BF16)
    wkv = w_kv_b[0].reshape(MLA_KV_RANK, MLA_HEADS, MLA_NOPE + MLA_V)
    w_kn = wkv[:, :, :MLA_NOPE].reshape(MLA_KV_RANK, -1).astype(BF16)
    w_v = wkv[:, :, MLA_NOPE:].reshape(MLA_KV_RANK, -1).astype(BF16)
    w_r32 = jnp.pad(w_router[0], ((0, 0), (0, LANES - N_EXPERTS)))
    w_r_hi = w_r32.astype(BF16)
    w_r = jnp.concatenate([w_r_hi, (w_r32 - w_r_hi.astype(F32)).astype(BF16)], axis=1)
    b_r = jnp.pad(b_router[0], (0, LANES - N_EXPERTS)).reshape(1, LANES)
    half = MLA_ROPE // 2
    inv_freq = ROPE_THETA ** (-(jnp.arange(half, dtype=F32) * 2.0 / MLA_ROPE))
    invf = jnp.concatenate([inv_freq, inv_freq, jnp.zeros((LANES - MLA_ROPE,), F32)]).reshape(1, LANES)
    pos_f = positions.reshape(t, 1).astype(F32)

    for l in range(DEPTH):
        u_sb, h_b = _ln_inproj(x2, row(ln_in_g), row(ln_in_b), w_in_b, 3 * SB_WIDTH)
        q_m, k_m, v_m = _mla_prep(h_b, pos_f, invf, w_lat, row(q_a_norm[l]), row(kv_a_norm[l]),
                                  w_qn, w_qp, w_kn, w_v)
        sb_o = _sb_attention(u_sb.reshape(bsz, seq, -1)).reshape(t, -1)
        mla_o = _mla_attention(q_m.reshape(bsz, seq, -1), k_m.reshape(bsz, seq, -1),
                               v_m.reshape(bsz, seq, -1)).reshape(t, -1)
        h2, h2p, sel, gates, counts_f = _mix_router(
            sb_o, mla_o, x2, row(ln_in_g), row(ln_in_b), row(sb_out_norm[l]), row(mla_out_norm[l]),
            w_o[l].astype(BF16), row(ln_mix_g[l]), row(ln_mix_b[l]), w_r, b_r)
        dest, row_of, ie, istart, inb, tail_blk, n_rows = _routing_tables(sel, counts_f)
        xs = _scatter_rows(h2p, TOP_K, row_of, n_rows)
        ys = _moe_ffn(ie, istart, inb, tail_blk, xs, w_gate_up[l],
                      b_gate_up[l].reshape(N_EXPERTS, 2 * MOE_NF, MOE_TF),
                      w_down[l], b_down[l].reshape(N_EXPERTS, 1, -1), n_rows)
        yg = _gather_rows(ys, dest).reshape(TOP_K, t, HALF)
        out = _combine(yg, gates, h2, row(ln_ffn_g[l]), row(ln_ffn_b[l]))
    return out.reshape(bsz, seq, d)
```

```python
import functools

import jax
import jax.numpy as jnp
from jax import lax
from jax.experimental import pallas as pl
from jax.experimental.pallas import tpu as pltpu

D_MODEL = 2048
DEPTH = 1
CHUNK = 64
SB_HEADS = 8
HEAD_DIM = 128
SB_WIDTH = SB_HEADS * HEAD_DIM
MLA_HEADS = 8
MLA_Q_RANK = 512
MLA_KV_RANK = 256
MLA_NOPE = 128
MLA_ROPE = 64
MLA_V = 128
MLA_WIDTH = MLA_HEADS * MLA_V
MLA_QK_PAD = 256
ROPE_THETA = 10000.0
N_EXPERTS = 32
TOP_K = 4
D_FF = 2048
SWIGLU_LIMIT = 7.0
SWIGLU_ALPHA = 1.702
MOE_BLOCK = 128
LN_EPS = 1e-5
RMS_EPS = 1e-6
DEEPNORM_ALPHA = (2 * DEPTH) ** 0.25

LANES = 128
SUBLANES = 8
HALF = D_MODEL // 2
VMEM_LIMIT_BYTES = 56 * 1024 * 1024

BF16 = jnp.bfloat16
F32 = jnp.float32
U32 = jnp.uint32


def _cparams(n_axes, vmem=None, flags=None):
    return pltpu.CompilerParams(dimension_semantics=("arbitrary",) * n_axes,
                                vmem_limit_bytes=vmem or VMEM_LIMIT_BYTES, flags=flags)


def _layer_norm(x, g, b):
    mu = jnp.mean(x, -1, keepdims=True)
    xc = x - mu
    var = jnp.mean(xc * xc, -1, keepdims=True)
    return xc * lax.rsqrt(var + LN_EPS) * g + b


def _rms_norm(x, g):
    ms = jnp.mean(x * x, -1, keepdims=True)
    return x * lax.rsqrt(ms + RMS_EPS) * g


def _pack_halves(lo_f32, hi_f32):
    lo = lax.bitcast_convert_type(lo_f32.astype(BF16).astype(F32), U32)
    hi = lax.bitcast_convert_type(hi_f32.astype(BF16).astype(F32), U32)
    return lax.shift_right_logical(lo, U32(16)) | (hi & U32(0xFFFF0000))


def _unpack_halves(w):
    lo = lax.bitcast_convert_type(lax.shift_left(w, U32(16)), F32)
    hi = lax.bitcast_convert_type(w & U32(0xFFFF0000), F32)
    return lo, hi


def _ln_inproj_kernel(x_ref, g_ref, b_ref, w_ref, u_ref, h_ref):
    h = _layer_norm(x_ref[...], g_ref[...], b_ref[...]).astype(BF16)
    h_ref[...] = h
    u_ref[...] = jnp.dot(h, w_ref[...], preferred_element_type=F32).astype(BF16)


def _ln_inproj(x2, g, b, w_sb, n, tm=512):
    t, d = x2.shape
    return pl.pallas_call(
        _ln_inproj_kernel,
        grid=(t // tm,),
        in_specs=[pl.BlockSpec((tm, d), lambda i: (i, 0)),
                  pl.BlockSpec((1, d), lambda i: (0, 0)),
                  pl.BlockSpec((1, d), lambda i: (0, 0)),
                  pl.BlockSpec((d, n), lambda i: (0, 0))],
        out_specs=[pl.BlockSpec((tm, n), lambda i: (i, 0)),
                   pl.BlockSpec((tm, d), lambda i: (i, 0))],
        out_shape=[jax.ShapeDtypeStruct((t, n), BF16), jax.ShapeDtypeStruct((t, d), BF16)],
        compiler_params=_cparams(1),
        name="ln_inproj",
    )(x2, g, b, w_sb)


def _rope_slab(x, cos, sin_neg, sin_pos):
    return (x * cos + pltpu.roll(x, LANES - MLA_ROPE // 2, 1) * sin_neg
            + pltpu.roll(x, MLA_ROPE // 2, 1) * sin_pos)


def _mla_prep_kernel(h_ref, pos_ref, invf_ref, wl_ref, qg_ref, kg_ref, wqn_ref, wqp_ref,
                     wkn_ref, wv_ref, q_ref, k_ref, v_ref):
    lat = jnp.dot(h_ref[...], wl_ref[...], preferred_element_type=F32)
    qn = _rms_norm(lat[:, :MLA_Q_RANK], qg_ref[...]).astype(BF16)
    kvn = _rms_norm(lat[:, MLA_Q_RANK:MLA_Q_RANK + MLA_KV_RANK], kg_ref[...]).astype(BF16)
    k_rope = lat[:, MLA_Q_RANK + MLA_KV_RANK:]

    ang = pos_ref[...] * invf_ref[...]
    cos = jnp.cos(ang)
    sin = jnp.sin(ang)
    lane = lax.broadcasted_iota(jnp.int32, ang.shape, 1)
    sin_neg = jnp.where(lane < MLA_ROPE // 2, -sin, 0.0)
    sin_pos = jnp.where((lane >= MLA_ROPE // 2) & (lane < MLA_ROPE), sin, 0.0)

    q_nope = jnp.dot(qn, wqn_ref[...], preferred_element_type=F32)
    q_pe = jnp.dot(qn, wqp_ref[...], preferred_element_type=F32)
    k_nope = jnp.dot(kvn, wkn_ref[...], preferred_element_type=F32)
    v_ref[...] = jnp.dot(kvn, wv_ref[...], preferred_element_type=F32).astype(BF16)
    k_pe = _rope_slab(k_rope, cos, sin_neg, sin_pos).astype(BF16)
    for hd in range(MLA_HEADS):
        c0 = hd * MLA_QK_PAD
        s0 = hd * LANES
        q_ref[:, c0:c0 + LANES] = q_nope[:, s0:s0 + LANES].astype(BF16)
        q_ref[:, c0 + LANES:c0 + 2 * LANES] = _rope_slab(
            q_pe[:, s0:s0 + LANES], cos, sin_neg, sin_pos).astype(BF16)
        k_ref[:, c0:c0 + LANES] = k_nope[:, s0:s0 + LANES].astype(BF16)
        k_ref[:, c0 + LANES:c0 + 2 * LANES] = k_pe


def _mla_prep(h_b, pos_f, invf, w_lat, qg, kg, w_qn, w_qp, w_kn, w_v, tm=512):
    t, d = h_b.shape
    full = lambda a: pl.BlockSpec(a.shape, lambda i: (0,) * a.ndim)
    wq = MLA_HEADS * MLA_QK_PAD
    return pl.pallas_call(
        _mla_prep_kernel,
        grid=(t // tm,),
        in_specs=[pl.BlockSpec((tm, d), lambda i: (i, 0)),
                  pl.BlockSpec((tm, 1), lambda i: (i, 0)),
                  full(invf), full(w_lat), full(qg), full(kg), full(w_qn), full(w_qp),
                  full(w_kn), full(w_v)],
        out_specs=[pl.BlockSpec((tm, wq), lambda i: (i, 0)),
                   pl.BlockSpec((tm, wq), lambda i: (i, 0)),
                   pl.BlockSpec((tm, MLA_WIDTH), lambda i: (i, 0))],
        out_shape=[jax.ShapeDtypeStruct((t, wq), BF16), jax.ShapeDtypeStruct((t, wq), BF16),
                   jax.ShapeDtypeStruct((t, MLA_WIDTH), BF16)],
        compiler_params=_cparams(1),
        name="mla_prep",
    )(h_b, pos_f, invf, w_lat, qg, kg, w_qn, w_qp, w_kn, w_v)


def _sb_attn_kernel(q_ref, k_ref, v_ref, o_ref, *, tq, tk, heads, scale):
    i = pl.program_id(2)
    t_idx = i * tq + lax.broadcasted_iota(jnp.int32, (tq, 1), 0)
    r_i = lax.broadcasted_iota(jnp.int32, (tk, tk), 0)
    c_i = lax.broadcasted_iota(jnp.int32, (tk, tk), 1)
    suffix = (r_i >= c_i).astype(BF16)
    suffix2 = jnp.concatenate([suffix, suffix], axis=0)
    nkb = (i + 1) * (tq // tk)

    def key_block(j, carry, diagonal, row0=0):
        m = tq - row0
        k0 = pl.multiple_of(j * tk, tk)
        if diagonal:
            past = (k0 + lax.broadcasted_iota(jnp.int32, (1, tk), 1)) < t_idx[row0:]
        zs, splits = [], []
        for hd in range(heads):
            cols = slice(hd * HEAD_DIM, (hd + 1) * HEAD_DIM)
            z = lax.dot_general(q_ref[row0:, cols], k_ref[pl.ds(k0, tk), cols],
                                (((1,), (1,)), ((), ())), preferred_element_type=F32) * scale
            sp = jnp.maximum(z, 0.0) + jnp.log(1.0 + jnp.exp(-jnp.abs(z)))
            if diagonal:
                sp = jnp.where(past, sp, 0.0)
            hi = sp.astype(BF16)
            lo = (sp - hi.astype(F32)).astype(BF16)
            zs.append(z)
            splits.append(jnp.concatenate([hi, lo], axis=1))
        incl_all = jnp.dot(jnp.concatenate(splits, axis=0), suffix2, preferred_element_type=F32)
        out = []
        for hd in range(heads):
            c, acc = carry[hd]
            cols = slice(hd * HEAD_DIM, (hd + 1) * HEAD_DIM)
            incl = incl_all[hd * m:(hd + 1) * m]
            w = jnp.exp(zs[hd] - incl - c[row0:])
            if diagonal:
                w = jnp.where(past, w, 0.0)
            pv = jnp.dot(w.astype(BF16), v_ref[pl.ds(k0, tk), cols], preferred_element_type=F32)
            c_new, acc_new = c[row0:] + incl[:, 0:1], acc[row0:] + pv
            if row0:
                c_new = jnp.concatenate([c[:row0], c_new], axis=0)
                acc_new = jnp.concatenate([acc[:row0], acc_new], axis=0)
            out.append((c_new, acc_new))
        return tuple(out)

    bpt = tq // tk

    def trip(p, carry, diagonal):
        for bk in range(bpt):
            row0 = (bpt - 1 - bk) * tk if diagonal else 0
            carry = key_block(nkb - 1 - bpt * p - bk, carry, diagonal, row0)
        return carry

    init = tuple((jnp.zeros((tq, 1), F32), jnp.zeros((tq, HEAD_DIM), F32)) for _ in range(heads))
    res = trip(0, init, True)
    res = lax.fori_loop(1, i + 1, functools.partial(trip, diagonal=False), res)
    for hd in range(heads):
        o_ref[:, hd * HEAD_DIM:(hd + 1) * HEAD_DIM] = res[hd][1].astype(o_ref.dtype)


def _sb_attention(u_sb3, tq=256, tk=128, heads=SB_HEADS):
    b, s, _ = u_sb3.shape
    assert tq % tk == 0 and SB_HEADS % heads == 0
    groups = SB_HEADS // heads
    w = heads * HEAD_DIM
    kern = functools.partial(_sb_attn_kernel, tq=tq, tk=tk, heads=heads, scale=HEAD_DIM ** -0.5)
    return pl.pallas_call(
        kern,
        grid=(b, groups, s // tq),
        in_specs=[pl.BlockSpec((None, tq, w), lambda bb, g, i: (bb, i, g)),
                  pl.BlockSpec((None, s, w), lambda bb, g, i: (bb, 0, groups + g)),
                  pl.BlockSpec((None, s, w), lambda bb, g, i: (bb, 0, 2 * groups + g))],
        out_specs=pl.BlockSpec((None, tq, w), lambda bb, g, i: (bb, i, g)),
        out_shape=jax.ShapeDtypeStruct((b, s, SB_WIDTH), BF16),
        compiler_params=_cparams(3),
        name="sb_attention",
    )(u_sb3, u_sb3, u_sb3)


def _mla_attn_kernel(q_ref, k_ref, v_ref, o_ref, *, tq, tk, heads, scale):
    i = pl.program_id(2)
    t_chunk = lax.shift_right_logical(
        i * tq + lax.broadcasted_iota(jnp.int32, (tq, 1), 0), CHUNK.bit_length() - 1)

    def body(j, carry, diagonal):
        k0 = pl.multiple_of(j * tk, tk)
        if diagonal:
            s_chunk = lax.shift_right_logical(
                k0 + lax.broadcasted_iota(jnp.int32, (1, tk), 1), CHUNK.bit_length() - 1)
            visible = s_chunk <= t_chunk

        def qk_scores(hd):
            qk = slice(hd * MLA_QK_PAD, (hd + 1) * MLA_QK_PAD)
            return lax.dot_general(q_ref[:, qk], k_ref[pl.ds(k0, tk), qk],
                                   (((1,), (1,)), ((), ())), preferred_element_type=F32)

        out = []
        nxt = qk_scores(0)
        for hd in range(heads):
            m, l, acc = carry[hd]
            s = nxt
            if hd + 1 < heads:
                nxt = qk_scores(hd + 1)
            s = s * scale
            if diagonal:
                s = jnp.where(visible, s, -1e30)
            m_new = jnp.maximum(m, jnp.max(s, -1, keepdims=True))
            alpha = jnp.exp(m - m_new)
            p = jnp.exp(s - m_new)
            l = alpha * l + jnp.sum(p, -1, keepdims=True)
            vs = v_ref[pl.ds(k0, tk), hd * MLA_V:(hd + 1) * MLA_V]
            acc = alpha * acc + jnp.dot(p.astype(BF16), vs, preferred_element_type=F32)
            out.append((m_new, l, acc))
        return tuple(out)

    init = tuple((jnp.full((tq, 1), -1e30, F32), jnp.zeros((tq, 1), F32),
                  jnp.zeros((tq, MLA_V), F32)) for _ in range(heads))
    res = lax.fori_loop(0, i, functools.partial(body, diagonal=False), init)
    res = body(i, res, True)
    for hd in range(heads):
        _, l, acc = res[hd]
        o_ref[:, hd * MLA_V:(hd + 1) * MLA_V] = (acc / l).astype(o_ref.dtype)


def _mla_attention(q3, k3, v3, tq=256, tk=256, heads=MLA_HEADS):
    b, s, _ = q3.shape
    assert MLA_HEADS % heads == 0 and tq == tk and tk % CHUNK == 0
    kern = functools.partial(_mla_attn_kernel, tq=tq, tk=tk, heads=heads,
                             scale=(MLA_NOPE + MLA_ROPE) ** -0.5)
    return pl.pallas_call(
        kern,
        grid=(b, MLA_HEADS // heads, s // tq),
        in_specs=[pl.BlockSpec((None, tq, heads * MLA_QK_PAD), lambda bb, g, i: (bb, i, g)),
                  pl.BlockSpec((None, s, heads * MLA_QK_PAD), lambda bb, g, i: (bb, 0, g)),
                  pl.BlockSpec((None, s, heads * MLA_V), lambda bb, g, i: (bb, 0, g))],
        out_specs=pl.BlockSpec((None, tq, heads * MLA_V), lambda bb, g, i: (bb, i, g)),
        out_shape=jax.ShapeDtypeStruct((b, s, MLA_WIDTH), BF16),
        compiler_params=_cparams(3),
        name="mla_attention",
    )(q3, k3, v3)


def _mix_router_kernel(sb_ref, mla_ref, x_ref, lig_ref, lib_ref, sg_ref, mg_ref, wo_ref,
                       lmg_ref, lmb_ref, wr_ref, br_ref,
                       h2_ref, h2p_ref, sel_ref, gate_ref, cnt_ref, run_ref, *, tm):
    step = pl.program_id(0)

    @pl.when(step == 0)
    def _():
        run_ref[...] = jnp.zeros_like(run_ref)

    a = _rms_norm(sb_ref[...].astype(F32), sg_ref[...]).astype(BF16)
    bm = _rms_norm(mla_ref[...].astype(F32), mg_ref[...]).astype(BF16)
    mix = (jnp.dot(a, wo_ref[:SB_WIDTH, :], preferred_element_type=F32)
           + jnp.dot(bm, wo_ref[SB_WIDTH:, :], preferred_element_type=F32))
    h = _layer_norm(x_ref[...], lig_ref[...], lib_ref[...])
    h2 = _layer_norm(DEEPNORM_ALPHA * h + mix, lmg_ref[...], lmb_ref[...])
    h2_ref[...] = h2
    h2p_ref[...] = _pack_halves(h2[:, :HALF], h2[:, HALF:])

    h2_hi = h2.astype(BF16)
    h2_lo = (h2 - h2_hi.astype(F32)).astype(BF16)
    hw = jnp.dot(h2_hi, wr_ref[...], preferred_element_type=F32)
    logits = (hw[:, :LANES] + hw[:, LANES:]
              + jnp.dot(h2_lo, wr_ref[:, :LANES], preferred_element_type=F32) + br_ref[...])
    lane = lax.broadcasted_iota(jnp.int32, (tm, LANES), 1)
    lane_f = lane.astype(F32)
    logits = jnp.where(lane < N_EXPERTS, logits, -jnp.inf)
    vals, hots = [], []
    sel = jnp.zeros((tm, LANES), F32)
    for k in range(TOP_K):
        m = jnp.max(logits, -1, keepdims=True)
        idx = jnp.min(jnp.where(logits == m, lane_f, float(LANES)), -1, keepdims=True)
        hot = lane_f == idx
        logits = jnp.where(hot, -jnp.inf, logits)
        vals.append(m)
        hots.append(hot)
        sel = jnp.where(lane == k, idx, sel)
    exps = [jnp.exp(v - vals[0]) for v in vals]
    denom = exps[0] + exps[1] + exps[2] + exps[3]
    gates = jnp.zeros((tm, LANES), F32)
    for k in range(TOP_K):
        gates = jnp.where(lane == k, exps[k] / denom, gates)
    gate_ref[...] = gates

    onehot = (hots[0] | hots[1] | hots[2] | hots[3]).astype(F32)
    r_i = lax.broadcasted_iota(jnp.int32, (tm, tm), 0)
    c_i = lax.broadcasted_iota(jnp.int32, (tm, tm), 1)
    before = (c_i < r_i).astype(BF16)
    rank = jnp.dot(before, onehot.astype(BF16), preferred_element_type=F32) + run_ref[...]
    for k in range(TOP_K):
        pos = jnp.sum(jnp.where(hots[k], rank, 0.0), -1, keepdims=True)
        sel = jnp.where(lane == TOP_K + k, pos, sel)
    sel_ref[...] = sel.T[:2 * TOP_K, :]
    run_ref[...] = run_ref[...] + jnp.sum(onehot, 0, keepdims=True)
    cnt_ref[...] = run_ref[...]


def _mix_router(sb_o, mla_o, x2, lig, lib, sg, mg, w_o, lmg, lmb, w_r, b_r, tm=512):
    t, d = x2.shape
    full = lambda a: pl.BlockSpec(a.shape, lambda i: (0,) * a.ndim)
    row = lambda w: pl.BlockSpec((tm, w), lambda i: (i, 0))
    return pl.pallas_call(
        functools.partial(_mix_router_kernel, tm=tm),
        grid=(t // tm,),
        in_specs=[row(SB_WIDTH), row(MLA_WIDTH), row(d), full(lig), full(lib), full(sg), full(mg),
                  full(w_o), full(lmg), full(lmb), full(w_r), full(b_r)],
        out_specs=[row(d), row(HALF), pl.BlockSpec((2 * TOP_K, tm), lambda i: (0, i)), row(LANES),
                   pl.BlockSpec((1, LANES), lambda i: (0, 0))],
        out_shape=[jax.ShapeDtypeStruct((t, d), F32), jax.ShapeDtypeStruct((t, HALF), U32),
                   jax.ShapeDtypeStruct((2 * TOP_K, t), F32), jax.ShapeDtypeStruct((t, LANES), F32),
                   jax.ShapeDtypeStruct((1, LANES), F32)],
        scratch_shapes=[pltpu.VMEM((1, LANES), F32)],
        compiler_params=_cparams(1),
        name="mix_router",
    )(sb_o, mla_o, x2, lig, lib, sg, mg, w_o, lmg, lmb, w_r, b_r)


ROWS_PER_STEP = 4096


def _wait_rows(like_hbm, sem, groups):
    pltpu.make_async_copy(like_hbm.at[pl.ds(0, groups)], like_hbm.at[pl.ds(0, groups)], sem).wait()


def _row_of(ref3, r):
    if isinstance(r, tuple):
        return ref3.at[r[0], pl.ds(r[1], 1), :]
    return ref3.at[lax.shift_right_logical(r, 3), pl.ds(r & (SUBLANES - 1), 1), :]


def _gather_rows_kernel(idx_ref, src_hbm, out_ref, sem, *, rows):
    def issue(g, _):
        r0 = pl.multiple_of(g * SUBLANES, SUBLANES)
        for k in range(SUBLANES):
            pltpu.make_async_copy(_row_of(src_hbm, idx_ref[0, 0, r0 + k]),
                                  _row_of(out_ref, (g, k)), sem).start()
        return 0

    lax.fori_loop(0, rows // SUBLANES, issue, 0)
    _wait_rows(src_hbm, sem, rows // SUBLANES)


def _scatter_rows_kernel(idx_ref, src_ref, out_hbm, zero_ref, sem, *, rows, src_steps):
    step = pl.program_id(0)

    @pl.when(step == 0)
    def _():
        zero_ref[...] = jnp.zeros_like(zero_ref)

    @pl.when(step < src_steps)
    def _():
        def issue(g, _):
            r0 = pl.multiple_of(g * SUBLANES, SUBLANES)
            for k in range(SUBLANES):
                pltpu.make_async_copy(_row_of(src_ref, (g, k)),
                                      _row_of(out_hbm, idx_ref[0, 0, r0 + k]), sem).start()
            return 0
        lax.fori_loop(0, rows // SUBLANES, issue, 0)

    @pl.when(step >= src_steps)
    def _():
        def issue(a, _):
            pltpu.make_async_copy(_row_of(zero_ref, (0, 0)),
                                  _row_of(out_hbm, idx_ref[0, 0, a]), sem).start()
            return 0
        lax.fori_loop(0, rows, issue, 0, unroll=8)

    _wait_rows(out_hbm, sem, rows // SUBLANES)


def _gather_rows(src, idx):
    rows = ROWS_PER_STEP
    n, w = idx.shape[0], src.shape[1]
    assert n % rows == 0 and src.shape[0] >= rows and src.shape[0] % SUBLANES == 0
    out = pl.pallas_call(
        functools.partial(_gather_rows_kernel, rows=rows),
        grid=(n // rows,),
        in_specs=[pl.BlockSpec((1, 1, rows), lambda i: (i, 0, 0), memory_space=pltpu.SMEM),
                  pl.BlockSpec(memory_space=pl.ANY)],
        out_specs=pl.BlockSpec((rows // SUBLANES, SUBLANES, w), lambda i: (i, 0, 0)),
        out_shape=jax.ShapeDtypeStruct((n // SUBLANES, SUBLANES, w), src.dtype),
        scratch_shapes=[pltpu.SemaphoreType.DMA(())],
        compiler_params=_cparams(1),
        name="gather_rows",
    )(idx.reshape(n // rows, 1, rows), src.reshape(-1, SUBLANES, w))
    return out.reshape(n, w)


def _scatter_rows(src, passes, idx, out_rows):
    rows = ROWS_PER_STEP
    n, w = idx.shape[0], src.shape[1]
    src_blocks = src.shape[0] // rows
    assert src.shape[0] % rows == 0 and n % rows == 0 and n == out_rows
    src_steps = passes * src_blocks
    kern = functools.partial(_scatter_rows_kernel, rows=rows, src_steps=src_steps)

    def src_block(i):
        return jnp.minimum(i // passes, src_blocks - 1), 0

    def idx_block(i):
        return jnp.where(i < src_steps, lax.rem(i, passes) * src_blocks + i // passes, i), 0, 0

    out = pl.pallas_call(
        kern,
        grid=(n // rows,),
        in_specs=[pl.BlockSpec((1, 1, rows), idx_block, memory_space=pltpu.SMEM),
                  pl.BlockSpec((rows // SUBLANES, SUBLANES, w), lambda i: src_block(i) + (0,))],
        out_specs=pl.BlockSpec(memory_space=pl.ANY),
        out_shape=jax.ShapeDtypeStruct((out_rows // SUBLANES, SUBLANES, w), src.dtype),
        scratch_shapes=[pltpu.VMEM((1, SUBLANES, w), src.dtype), pltpu.SemaphoreType.DMA(())],
        compiler_params=_cparams(1),
        name="scatter_rows",
    )(idx.reshape(n // rows, 1, rows), src.reshape(-1, SUBLANES, w))
    return out.reshape(out_rows, w)


MOE_ROWS = 1536
MOE_SUB = 512
MOE_TF = 256
MOE_NF = D_FF // MOE_TF
MOE_ITEMS = N_EXPERTS + (N_EXPERTS * (MOE_BLOCK - 1) + 8192 * TOP_K) // MOE_ROWS + 1


MOE_GU_SLOTS = 2
MOE_WD_SLOTS = 3


def _moe_kernel(ie_ref, is_ref, nb_ref, tail_ref, xs_hbm, wgu_hbm, wd_hbm, bgu_ref, bd_ref,
                ys_hbm, pkin_ref, pkout_ref, x_ref, acc_ref, wg_buf, wu_buf, wd_buf,
                pend_ref, sem_in, sem_out, sem_w):
    i = pl.program_id(0)
    nb = nb_ref[i]
    start = is_ref[i]
    n_out_blocks = ys_hbm.shape[0] // MOE_BLOCK

    def weight_copies(item, f):
        e = ie_ref[item]
        g = item * MOE_NF + f
        gs = lax.rem(g, MOE_GU_SLOTS)
        ws = lax.rem(g, MOE_WD_SLOTS)
        c0 = pl.multiple_of(f * MOE_TF, MOE_TF)
        return (pltpu.make_async_copy(wgu_hbm.at[e, :, pl.ds(c0, MOE_TF)], wg_buf.at[gs],
                                      sem_w.at[gs]),
                pltpu.make_async_copy(wgu_hbm.at[e, :, pl.ds(D_FF + c0, MOE_TF)], wu_buf.at[gs],
                                      sem_w.at[gs]),
                pltpu.make_async_copy(wd_hbm.at[e, pl.ds(c0, MOE_TF), :], wd_buf.at[ws],
                                      sem_w.at[MOE_GU_SLOTS + ws]))

    def vm_block(ref, b):
        return ref.at[pl.ds(pl.multiple_of(b * MOE_BLOCK, MOE_BLOCK), MOE_BLOCK)]

    def hbm_block(ref, first_row, b):
        return ref.at[pl.ds(pl.multiple_of(first_row + b * MOE_BLOCK, MOE_BLOCK), MOE_BLOCK)]

    def in_copy(first_row, b):
        return pltpu.make_async_copy(hbm_block(xs_hbm, first_row, b), vm_block(pkin_ref, b), sem_in)

    def out_copy(first_row, b):
        return pltpu.make_async_copy(vm_block(pkout_ref, b), hbm_block(ys_hbm, first_row, b), sem_out)

    def for_blocks(n, fn):
        def body(b, _):
            fn(b)
            return 0
        lax.fori_loop(0, n, body, 0)

    def next_item_valid():
        return (i + 1 < MOE_ITEMS) & (nb_ref[jnp.minimum(i + 1, MOE_ITEMS - 1)] > 0)

    @pl.when(i == 0)
    def _():
        x_ref[...] = jnp.zeros_like(x_ref)
        acc_ref[...] = jnp.zeros_like(acc_ref)
        pend_ref[0] = 0
        pend_ref[1] = 0
        for_blocks(nb, lambda b: in_copy(start, b).start())
        for c in weight_copies(0, 0):
            c.start()

    @pl.when(nb > 0)
    def _():
        def unpack(b):
            r0 = pl.multiple_of(b * MOE_BLOCK, MOE_BLOCK)
            lo, hi = _unpack_halves(pkin_ref[pl.ds(r0, MOE_BLOCK), :])
            x_ref[pl.ds(r0, MOE_BLOCK), :HALF] = lo.astype(BF16)
            x_ref[pl.ds(r0, MOE_BLOCK), HALF:] = hi.astype(BF16)

        for_blocks(nb, lambda b: in_copy(start, b).wait())
        for_blocks(nb, unpack)

        @pl.when(next_item_valid())
        def _():
            nxt = jnp.minimum(i + 1, MOE_ITEMS - 1)
            for_blocks(nb_ref[nxt], lambda b: in_copy(is_ref[nxt], b).start())

        n_sub = (nb * MOE_BLOCK) // MOE_SUB
        n_tail = nb - n_sub * (MOE_SUB // MOE_BLOCK)

        def gate_up(f, r0, rows, gs):
            x = x_ref[pl.ds(pl.multiple_of(r0, MOE_BLOCK), rows), :]
            g = (jnp.dot(x, wg_buf[gs].astype(BF16), preferred_element_type=F32)
                 + bgu_ref[pl.ds(f, 1), :])
            u = (jnp.dot(x, wu_buf[gs].astype(BF16), preferred_element_type=F32)
                 + bgu_ref[pl.ds(MOE_NF + f, 1), :])
            g = jnp.minimum(g, SWIGLU_LIMIT)
            u = jnp.clip(u, -SWIGLU_LIMIT, SWIGLU_LIMIT)
            return ((u + 1.0) * (g * jax.nn.sigmoid(SWIGLU_ALPHA * g))).astype(BF16)

        def down(act, r0, ws, first):
            rows = act.shape[0]
            r0 = pl.multiple_of(r0, MOE_BLOCK)
            y = jnp.dot(act, wd_buf[ws].astype(BF16), preferred_element_type=F32)
            prev = jnp.where(first == 1, bd_ref[...], acc_ref[pl.ds(r0, rows), :])
            acc_ref[pl.ds(r0, rows), :] = prev + y

        def tile(f, carry):
            for c in weight_copies(i, f):
                c.wait()

            @pl.when(f + 1 < MOE_NF)
            def _():
                for c in weight_copies(i, f + 1):
                    c.start()

            @pl.when((f + 1 == MOE_NF) & next_item_valid())
            def _():
                for c in weight_copies(i + 1, 0):
                    c.start()

            g = i * MOE_NF + f
            gs = lax.rem(g, MOE_GU_SLOTS)
            ws = lax.rem(g, MOE_WD_SLOTS)
            first = jnp.asarray(f == 0, jnp.int32)

            def sub(s, c):
                act, r_prev, ws_prev, first_prev = c
                down(act, r_prev, ws_prev, first_prev)
                return gate_up(f, s * MOE_SUB, MOE_SUB, gs), s * MOE_SUB, ws, first

            carry = lax.fori_loop(0, n_sub, sub, carry)

            def tail(t, _):
                r0 = n_sub * MOE_SUB + t * MOE_BLOCK
                down(gate_up(f, r0, MOE_BLOCK, gs), r0, ws, first)
                return 0

            lax.fori_loop(0, n_tail, tail, 0)
            return carry

        assert MOE_ROWS % MOE_SUB == 0 and MOE_SUB % MOE_BLOCK == 0
        carry0 = (jnp.zeros((MOE_SUB, MOE_TF), BF16), jnp.int32(MOE_ROWS),
                  lax.rem(i * MOE_NF, MOE_WD_SLOTS), jnp.int32(0))
        down(*lax.fori_loop(0, MOE_NF, tile, carry0))

    def drain_pending_store():
        for_blocks(pend_ref[1], lambda b: out_copy(pend_ref[0], b).wait())
        pend_ref[1] = 0

    @pl.when(nb > 0)
    def _():
        def pack(b):
            r0 = pl.multiple_of(b * MOE_BLOCK, MOE_BLOCK)
            y = acc_ref[pl.ds(r0, MOE_BLOCK), :]
            pkout_ref[pl.ds(r0, MOE_BLOCK), :] = _pack_halves(y[:, :HALF], y[:, HALF:])

        drain_pending_store()
        for_blocks(nb, pack)
        for_blocks(nb, lambda b: out_copy(start, b).start())
        pend_ref[0] = start
        pend_ref[1] = nb

    @pl.when(i == MOE_ITEMS - 1)
    def _():
        drain_pending_store()
        pkout_ref[:MOE_BLOCK, :] = jnp.zeros((MOE_BLOCK, HALF), U32)

        def tail_copy(b):
            return pltpu.make_async_copy(
                pkout_ref.at[pl.ds(0, MOE_BLOCK)],
                ys_hbm.at[pl.ds(pl.multiple_of(b * MOE_BLOCK, MOE_BLOCK), MOE_BLOCK)], sem_out)

        def tail_blocks(fn):
            def body(b, _):
                fn(b)
                return 0
            lax.fori_loop(tail_ref[0], n_out_blocks, body, 0)

        tail_blocks(lambda b: tail_copy(b).start())
        tail_blocks(lambda b: tail_copy(b).wait())


def _moe_ffn(item_e, item_start, item_nb, tail_blk, xs, w_gu, b_gu, w_d, b_d, n_rows):
    by_expert = lambda i, ie, is_, nb, tail: (ie[i], 0, 0)
    grid_spec = pltpu.PrefetchScalarGridSpec(
        num_scalar_prefetch=4,
        grid=(MOE_ITEMS,),
        in_specs=[pl.BlockSpec(memory_space=pl.ANY),
                  pl.BlockSpec(memory_space=pl.ANY),
                  pl.BlockSpec(memory_space=pl.ANY),
                  pl.BlockSpec((None, 2 * MOE_NF, MOE_TF), by_expert),
                  pl.BlockSpec((None, 1, D_MODEL), by_expert)],
        out_specs=pl.BlockSpec(memory_space=pl.ANY),
        scratch_shapes=[pltpu.VMEM((MOE_ROWS, HALF), U32),
                        pltpu.VMEM((MOE_ROWS, HALF), U32),
                        pltpu.VMEM((MOE_ROWS, D_MODEL), BF16),
                        pltpu.VMEM((MOE_ROWS + MOE_SUB, D_MODEL), F32),
                        pltpu.VMEM((MOE_GU_SLOTS, D_MODEL, MOE_TF), F32),
                        pltpu.VMEM((MOE_GU_SLOTS, D_MODEL, MOE_TF), F32),
                        pltpu.VMEM((MOE_WD_SLOTS, MOE_TF, D_MODEL), F32),
                        pltpu.SMEM((2,), jnp.int32),
                        pltpu.SemaphoreType.DMA(()),
                        pltpu.SemaphoreType.DMA(()),
                        pltpu.SemaphoreType.DMA((MOE_GU_SLOTS + MOE_WD_SLOTS,))])
    return pl.pallas_call(
        _moe_kernel,
        grid_spec=grid_spec,
        out_shape=jax.ShapeDtypeStruct((n_rows, HALF), U32),
        compiler_params=_cparams(1),
        name="moe_ffn",
    )(item_e, item_start, item_nb, tail_blk, xs, w_gu, w_d, b_gu, b_d)


def _combine_kernel(y0_ref, y1_ref, y2_ref, y3_ref, gate_ref, h2_ref, g_ref, b_ref, o_ref):
    h2 = h2_ref[...]
    lo = DEEPNORM_ALPHA * h2[:, :HALF]
    hi = DEEPNORM_ALPHA * h2[:, HALF:]
    for k, y_ref in enumerate((y0_ref, y1_ref, y2_ref, y3_ref)):
        ylo, yhi = _unpack_halves(y_ref[...])
        gk = gate_ref[:, k:k + 1]
        lo = lo + gk * ylo
        hi = hi + gk * yhi
    mu = (jnp.sum(lo, -1, keepdims=True) + jnp.sum(hi, -1, keepdims=True)) * (1.0 / D_MODEL)
    lo = lo - mu
    hi = hi - mu
    var = (jnp.sum(lo * lo, -1, keepdims=True) + jnp.sum(hi * hi, -1, keepdims=True)) * (1.0 / D_MODEL)
    inv = lax.rsqrt(var + LN_EPS)
    o_ref[:, :HALF] = lo * inv * g_ref[:, :HALF] + b_ref[:, :HALF]
    o_ref[:, HALF:] = hi * inv * g_ref[:, HALF:] + b_ref[:, HALF:]


def _combine(yg, gates, h2, g, b, tm=256):
    t, d = h2.shape
    full = lambda a: pl.BlockSpec(a.shape, lambda i: (0,) * a.ndim)
    return pl.pallas_call(
        _combine_kernel,
        grid=(t // tm,),
        in_specs=[pl.BlockSpec((None, tm, HALF), functools.partial(lambda k, i: (k, i, 0), k))
                  for k in range(TOP_K)] + [
                  pl.BlockSpec((tm, LANES), lambda i: (i, 0)),
                  pl.BlockSpec((tm, d), lambda i: (i, 0)), full(g), full(b)],
        out_specs=pl.BlockSpec((tm, d), lambda i: (i, 0)),
        out_shape=jax.ShapeDtypeStruct((t, d), F32),
        compiler_params=_cparams(1),
        name="combine_ln",
    )(yg, yg, yg, yg, gates, h2, g, b)


def _routing_tables(sel, counts_f):
    t = sel.shape[1]
    idx = sel[:TOP_K].astype(jnp.int32)
    pos = sel[TOP_K:].astype(jnp.int32)
    counts = counts_f[0, :N_EXPERTS].astype(jnp.int32)
    padded = ((counts + MOE_BLOCK - 1) // MOE_BLOCK) * MOE_BLOCK
    pends = jnp.cumsum(padded)
    pstarts = pends - padded
    experts = jnp.arange(N_EXPERTS, dtype=jnp.int32)[:, None, None]
    first_row = jnp.sum(jnp.where(idx[None] == experts, pstarts[:, None, None], 0), axis=0)
    dest = (first_row + pos).reshape(-1)
    n_rows = t * TOP_K + N_EXPERTS * MOE_BLOCK
    j = jnp.arange(MOE_BLOCK, dtype=jnp.int32)[None, :]
    is_pad = j < (padded - counts)[:, None]
    n_after = jnp.cumsum((~is_pad).reshape(-1).astype(jnp.int32)).reshape(is_pad.shape) - 1
    zero_rows = jnp.where(is_pad, (pstarts + counts)[:, None] + j, pends[-1] + n_after)
    row_of = jnp.concatenate([dest, zero_rows.reshape(-1)])

    nchunk = (padded + MOE_ROWS - 1) // MOE_ROWS
    cend = jnp.cumsum(nchunk)
    cstart = cend - nchunk
    total = cend[-1]
    it = jnp.arange(MOE_ITEMS, dtype=jnp.int32)
    valid = it < total
    it_c = jnp.minimum(it, total - 1)
    ie = jnp.searchsorted(cend, it_c, side="right").astype(jnp.int32)
    c = it_c - cstart[ie]
    istart = (pstarts[ie] + c * MOE_ROWS).astype(jnp.int32)
    inb = jnp.where(valid, jnp.minimum(MOE_ROWS, padded[ie] - c * MOE_ROWS) // MOE_BLOCK, 0)
    tail_blk = (pends[-1] // MOE_BLOCK).astype(jnp.int32).reshape(1)
    return dest, row_of, ie, istart, inb.astype(jnp.int32), tail_blk, n_rows


def kernel(x, positions, ln_in_g, ln_in_b, w_in, q_a_norm, w_q_b, kv_a_norm, w_kv_b, sb_out_norm,
           mla_out_norm, w_o, ln_mix_g, ln_mix_b, w_router, b_router, w_gate_up, b_gate_up, w_down,
           b_down, ln_ffn_g, ln_ffn_b):
    bsz, seq, d = x.shape
    t = bsz * seq
    x2 = x.reshape(t, d)
    row = lambda v: v.reshape(1, -1)

    w_in0 = w_in[0]
    w_in_b = w_in0.astype(BF16)
    w_lat = jnp.pad(w_in_b[:, 3 * SB_WIDTH:], ((0, 0), (0, LANES - MLA_ROPE)))
    wq = w_q_b[0].reshape(MLA_Q_RANK, MLA_HEADS, MLA_NOPE + MLA_ROPE)
    w_qn = wq[:, :, :MLA_NOPE].reshape(MLA_Q_RANK, -1).astype(BF16)
    w_qp = jnp.pad(wq[:, :, MLA_NOPE:], ((0, 0), (0, 0), (0, LANES - MLA_ROPE))
                   ).reshape(MLA_Q_RANK, -1).astype(BF16)
    wkv = w_kv_b[0].reshape(MLA_KV_RANK, MLA_HEADS, MLA_NOPE + MLA_V)
    w_kn = wkv[:, :, :MLA_NOPE].reshape(MLA_KV_RANK, -1).astype(BF16)
    w_v = wkv[:, :, MLA_NOPE:].reshape(MLA_KV_RANK, -1).astype(BF16)
    w_r32 = jnp.pad(w_router[0], ((0, 0), (0, LANES - N_EXPERTS)))
    w_r_hi = w_r32.astype(BF16)
    w_r = jnp.concatenate([w_r_hi, (w_r32 - w_r_hi.astype(F32)).astype(BF16)], axis=1)
    b_r = jnp.pad(b_router[0], (0, LANES - N_EXPERTS)).reshape(1, LANES)
    half = MLA_ROPE // 2
    inv_freq = ROPE_THETA ** (-(jnp.arange(half, dtype=F32) * 2.0 / MLA_ROPE))
    invf = jnp.concatenate([inv_freq, inv_freq, jnp.zeros((LANES - MLA_ROPE,), F32)]).reshape(1, LANES)
    pos_f = positions.reshape(t, 1).astype(F32)

    for l in range(DEPTH):
        u_sb, h_b = _ln_inproj(x2, row(ln_in_g), row(ln_in_b), w_in_b, 3 * SB_WIDTH)
        q_m, k_m, v_m = _mla_prep(h_b, pos_f, invf, w_lat, row(q_a_norm[l]), row(kv_a_norm[l]),
                                  w_qn, w_qp, w_kn, w_v)
        sb_o = _sb_attention(u_sb.reshape(bsz, seq, -1)).reshape(t, -1)
        mla_o = _mla_attention(q_m.reshape(bsz, seq, -1), k_m.reshape(bsz, seq, -1),
                               v_m.reshape(bsz, seq, -1)).reshape(t, -1)
        h2, h2p, sel, gates, counts_f = _mix_router(
            sb_o, mla_o, x2, row(ln_in_g), row(ln_in_b), row(sb_out_norm[l]), row(mla_out_norm[l]),
            w_o[l].astype(BF16), row(ln_mix_g[l]), row(ln_mix_b[l]), w_r, b_r)
        dest, row_of, ie, istart, inb, tail_blk, n_rows = _routing_tables(sel, counts_f)
        xs = _scatter_rows(h2p, TOP_K, row_of, n_rows)
        ys = _moe_ffn(ie, istart, inb, tail_blk, xs, w_gate_up[l],
                      b_gate_up[l].reshape(N_EXPERTS, 2 * MOE_NF, MOE_TF),
                      w_down[l], b_down[l].reshape(N_EXPERTS, 1, -1), n_rows)
        yg = _gather_rows(ys, dest).reshape(TOP_K, t, HALF)
        out = _combine(yg, gates, h2, row(ln_ffn_g[l]), row(ln_ffn_b[l]))
    return out.reshape(bsz, seq, d)
```

```python
import functools

import jax
import jax.numpy as jnp
from jax import lax
from jax.experimental import pallas as pl
from jax.experimental.pallas import tpu as pltpu

D_MODEL = 2048
DEPTH = 1
CHUNK = 64
SB_HEADS = 8
HEAD_DIM = 128
SB_WIDTH = SB_HEADS * HEAD_DIM
MLA_HEADS = 8
MLA_Q_RANK = 512
MLA_KV_RANK = 256
MLA_NOPE = 128
MLA_ROPE = 64
MLA_V = 128
MLA_WIDTH = MLA_HEADS * MLA_V
MLA_QK_PAD = 256
ROPE_THETA = 10000.0
N_EXPERTS = 32
TOP_K = 4
D_FF = 2048
SWIGLU_LIMIT = 7.0
SWIGLU_ALPHA = 1.702
MOE_BLOCK = 128
LN_EPS = 1e-5
RMS_EPS = 1e-6
DEEPNORM_ALPHA = (2 * DEPTH) ** 0.25

LANES = 128
SUBLANES = 8
HALF = D_MODEL // 2
VMEM_LIMIT_BYTES = 56 * 1024 * 1024

BF16 = jnp.bfloat16
F32 = jnp.float32
U32 = jnp.uint32


def _cparams(n_axes, vmem=None, flags=None):
    return pltpu.CompilerParams(dimension_semantics=("arbitrary",) * n_axes,
                                vmem_limit_bytes=vmem or VMEM_LIMIT_BYTES, flags=flags)


def _layer_norm(x, g, b):
    mu = jnp.mean(x, -1, keepdims=True)
    xc = x - mu
    var = jnp.mean(xc * xc, -1, keepdims=True)
    return xc * lax.rsqrt(var + LN_EPS) * g + b


def _rms_norm(x, g):
    ms = jnp.mean(x * x, -1, keepdims=True)
    return x * lax.rsqrt(ms + RMS_EPS) * g


def _pack_halves(lo_f32, hi_f32):
    lo = lax.bitcast_convert_type(lo_f32.astype(BF16).astype(F32), U32)
    hi = lax.bitcast_convert_type(hi_f32.astype(BF16).astype(F32), U32)
    return lax.shift_right_logical(lo, U32(16)) | (hi & U32(0xFFFF0000))


def _unpack_halves(w):
    lo = lax.bitcast_convert_type(lax.shift_left(w, U32(16)), F32)
    hi = lax.bitcast_convert_type(w & U32(0xFFFF0000), F32)
    return lo, hi


def _ln_inproj_kernel(x_ref, g_ref, b_ref, w_ref, u_ref, h_ref):
    h = _layer_norm(x_ref[...], g_ref[...], b_ref[...]).astype(BF16)
    h_ref[...] = h
    u_ref[...] = jnp.dot(h, w_ref[...], preferred_element_type=F32).astype(BF16)


def _ln_inproj(x2, g, b, w_sb, n, tm=512):
    t, d = x2.shape
    return pl.pallas_call(
        _ln_inproj_kernel,
        grid=(t // tm,),
        in_specs=[pl.BlockSpec((tm, d), lambda i: (i, 0)),
                  pl.BlockSpec((1, d), lambda i: (0, 0)),
                  pl.BlockSpec((1, d), lambda i: (0, 0)),
                  pl.BlockSpec((d, n), lambda i: (0, 0))],
        out_specs=[pl.BlockSpec((tm, n), lambda i: (i, 0)),
                   pl.BlockSpec((tm, d), lambda i: (i, 0))],
        out_shape=[jax.ShapeDtypeStruct((t, n), BF16), jax.ShapeDtypeStruct((t, d), BF16)],
        compiler_params=_cparams(1),
        name="ln_inproj",
    )(x2, g, b, w_sb)


def _rope_slab(x, cos, sin_neg, sin_pos):
    return (x * cos + pltpu.roll(x, LANES - MLA_ROPE // 2, 1) * sin_neg
            + pltpu.roll(x, MLA_ROPE // 2, 1) * sin_pos)


def _mla_prep_kernel(h_ref, pos_ref, invf_ref, wl_ref, qg_ref, kg_ref, wqn_ref, wqp_ref,
                     wkn_ref, wv_ref, q_ref, k_ref, v_ref):
    lat = jnp.dot(h_ref[...], wl_ref[...], preferred_element_type=F32)
    qn = _rms_norm(lat[:, :MLA_Q_RANK], qg_ref[...]).astype(BF16)
    kvn = _rms_norm(lat[:, MLA_Q_RANK:MLA_Q_RANK + MLA_KV_RANK], kg_ref[...]).astype(BF16)
    k_rope = lat[:, MLA_Q_RANK + MLA_KV_RANK:]

    ang = pos_ref[...] * invf_ref[...]
    cos = jnp.cos(ang)
    sin = jnp.sin(ang)
    lane = lax.broadcasted_iota(jnp.int32, ang.shape, 1)
    sin_neg = jnp.where(lane < MLA_ROPE // 2, -sin, 0.0)
    sin_pos = jnp.where((lane >= MLA_ROPE // 2) & (lane < MLA_ROPE), sin, 0.0)

    q_nope = jnp.dot(qn, wqn_ref[...], preferred_element_type=F32)
    q_pe = jnp.dot(qn, wqp_ref[...], preferred_element_type=F32)
    k_nope = jnp.dot(kvn, wkn_ref[...], preferred_element_type=F32)
    v_ref[...] = jnp.dot(kvn, wv_ref[...], preferred_element_type=F32).astype(BF16)
    k_pe = _rope_slab(k_rope, cos, sin_neg, sin_pos).astype(BF16)
    for hd in range(MLA_HEADS):
        c0 = hd * MLA_QK_PAD
        s0 = hd * LANES
        q_ref[:, c0:c0 + LANES] = q_nope[:, s0:s0 + LANES].astype(BF16)
        q_ref[:, c0 + LANES:c0 + 2 * LANES] = _rope_slab(
            q_pe[:, s0:s0 + LANES], cos, sin_neg, sin_pos).astype(BF16)
        k_ref[:, c0:c0 + LANES] = k_nope[:, s0:s0 + LANES].astype(BF16)
        k_ref[:, c0 + LANES:c0 + 2 * LANES] = k_pe


def _mla_prep(h_b, pos_f, invf, w_lat, qg, kg, w_qn, w_qp, w_kn, w_v, tm=512):
    t, d = h_b.shape
    full = lambda a: pl.BlockSpec(a.shape, lambda i: (0,) * a.ndim)
    wq = MLA_HEADS * MLA_QK_PAD
    return pl.pallas_call(
        _mla_prep_kernel,
        grid=(t // tm,),
        in_specs=[pl.BlockSpec((tm, d), lambda i: (i, 0)),
                  pl.BlockSpec((tm, 1), lambda i: (i, 0)),
                  full(invf), full(w_lat), full(qg), full(kg), full(w_qn), full(w_qp),
                  full(w_kn), full(w_v)],
        out_specs=[pl.BlockSpec((tm, wq), lambda i: (i, 0)),
                   pl.BlockSpec((tm, wq), lambda i: (i, 0)),
                   pl.BlockSpec((tm, MLA_WIDTH), lambda i: (i, 0))],
        out_shape=[jax.ShapeDtypeStruct((t, wq), BF16), jax.ShapeDtypeStruct((t, wq), BF16),
                   jax.ShapeDtypeStruct((t, MLA_WIDTH), BF16)],
        compiler_params=_cparams(1),
        name="mla_prep",
    )(h_b, pos_f, invf, w_lat, qg, kg, w_qn, w_qp, w_kn, w_v)


def _sb_attn_kernel(q_ref, k_ref, v_ref, o_ref, *, tq, tk, heads, scale):
    i = pl.program_id(2)
    t_idx = i * tq + lax.broadcasted_iota(jnp.int32, (tq, 1), 0)
    r_i = lax.broadcasted_iota(jnp.int32, (tk, tk), 0)
    c_i = lax.broadcasted_iota(jnp.int32, (tk, tk), 1)
    suffix = (r_i >= c_i).astype(BF16)
    suffix2 = jnp.concatenate([suffix, suffix], axis=0)
    nkb = (i + 1) * (tq // tk)

    def key_block(j, carry, diagonal, row0=0):
        m = tq - row0
        k0 = pl.multiple_of(j * tk, tk)
        if diagonal:
            past = (k0 + lax.broadcasted_iota(jnp.int32, (1, tk), 1)) < t_idx[row0:]
        zs, splits = [], []
        for hd in range(heads):
            cols = slice(hd * HEAD_DIM, (hd + 1) * HEAD_DIM)
            z = lax.dot_general(q_ref[row0:, cols], k_ref[pl.ds(k0, tk), cols],
                                (((1,), (1,)), ((), ())), preferred_element_type=F32) * scale
            sp = jnp.maximum(z, 0.0) + jnp.log(1.0 + jnp.exp(-jnp.abs(z)))
            if diagonal:
                sp = jnp.where(past, sp, 0.0)
            hi = sp.astype(BF16)
            lo = (sp - hi.astype(F32)).astype(BF16)
            zs.append(z)
            splits.append(jnp.concatenate([hi, lo], axis=1))
        incl_all = jnp.dot(jnp.concatenate(splits, axis=0), suffix2, preferred_element_type=F32)
        out = []
        for hd in range(heads):
            c, acc = carry[hd]
            cols = slice(hd * HEAD_DIM, (hd + 1) * HEAD_DIM)
            incl = incl_all[hd * m:(hd + 1) * m]
            w = jnp.exp(zs[hd] - incl - c[row0:])
            if diagonal:
                w = jnp.where(past, w, 0.0)
            pv = jnp.dot(w.astype(BF16), v_ref[pl.ds(k0, tk), cols], preferred_element_type=F32)
            c_new, acc_new = c[row0:] + incl[:, 0:1], acc[row0:] + pv
            if row0:
                c_new = jnp.concatenate([c[:row0], c_new], axis=0)
                acc_new = jnp.concatenate([acc[:row0], acc_new], axis=0)
            out.append((c_new, acc_new))
        return tuple(out)

    bpt = tq // tk

    def trip(p, carry, diagonal):
        for bk in range(bpt):
            row0 = (bpt - 1 - bk) * tk if diagonal else 0
            carry = key_block(nkb - 1 - bpt * p - bk, carry, diagonal, row0)
        return carry

    init = tuple((jnp.zeros((tq, 1), F32), jnp.zeros((tq, HEAD_DIM), F32)) for _ in range(heads))
    res = trip(0, init, True)
    res = lax.fori_loop(1, i + 1, functools.partial(trip, diagonal=False), res)
    for hd in range(heads):
        o_ref[:, hd * HEAD_DIM:(hd + 1) * HEAD_DIM] = res[hd][1].astype(o_ref.dtype)


def _sb_attention(u_sb3, tq=512, tk=128, heads=SB_HEADS):
    b, s, _ = u_sb3.shape
    assert tq % tk == 0 and SB_HEADS % heads == 0
    groups = SB_HEADS // heads
    w = heads * HEAD_DIM
    kern = functools.partial(_sb_attn_kernel, tq=tq, tk=tk, heads=heads, scale=HEAD_DIM ** -0.5)
    return pl.pallas_call(
        kern,
        grid=(b, groups, s // tq),
        in_specs=[pl.BlockSpec((None, tq, w), lambda bb, g, i: (bb, i, g)),
                  pl.BlockSpec((None, s, w), lambda bb, g, i: (bb, 0, groups + g)),
                  pl.BlockSpec((None, s, w), lambda bb, g, i: (bb, 0, 2 * groups + g))],
        out_specs=pl.BlockSpec((None, tq, w), lambda bb, g, i: (bb, i, g)),
        out_shape=jax.ShapeDtypeStruct((b, s, SB_WIDTH), BF16),
        compiler_params=_cparams(3),
        name="sb_attention",
    )(u_sb3, u_sb3, u_sb3)


def _mla_attn_kernel(q_ref, k_ref, v_ref, o_ref, *, tq, tk, heads, scale):
    i = pl.program_id(2)
    t_chunk = lax.shift_right_logical(
        i * tq + lax.broadcasted_iota(jnp.int32, (tq, 1), 0), CHUNK.bit_length() - 1)

    def body(j, carry, diagonal):
        k0 = pl.multiple_of(j * tk, tk)
        if diagonal:
            s_chunk = lax.shift_right_logical(
                k0 + lax.broadcasted_iota(jnp.int32, (1, tk), 1), CHUNK.bit_length() - 1)
            visible = s_chunk <= t_chunk

        def qk_scores(hd):
            qk = slice(hd * MLA_QK_PAD, (hd + 1) * MLA_QK_PAD)
            return lax.dot_general(q_ref[:, qk], k_ref[pl.ds(k0, tk), qk],
                                   (((1,), (1,)), ((), ())), preferred_element_type=F32)

        out = []
        nxt = qk_scores(0)
        for hd in range(heads):
            m, l, acc = carry[hd]
            s = nxt
            if hd + 1 < heads:
                nxt = qk_scores(hd + 1)
            s = s * scale
            if diagonal:
                s = jnp.where(visible, s, -1e30)
            m_new = jnp.maximum(m, jnp.max(s, -1, keepdims=True))
            alpha = jnp.exp(m - m_new)
            p = jnp.exp(s - m_new)
            l = alpha * l + jnp.sum(p, -1, keepdims=True)
            vs = v_ref[pl.ds(k0, tk), hd * MLA_V:(hd + 1) * MLA_V]
            acc = alpha * acc + jnp.dot(p.astype(BF16), vs, preferred_element_type=F32)
            out.append((m_new, l, acc))
        return tuple(out)

    init = tuple((jnp.full((tq, 1), -1e30, F32), jnp.zeros((tq, 1), F32),
                  jnp.zeros((tq, MLA_V), F32)) for _ in range(heads))
    res = lax.fori_loop(0, i, functools.partial(body, diagonal=False), init)
    res = body(i, res, True)
    for hd in range(heads):
        _, l, acc = res[hd]
        o_ref[:, hd * MLA_V:(hd + 1) * MLA_V] = (acc / l).astype(o_ref.dtype)


def _mla_attention(q3, k3, v3, tq=256, tk=256, heads=MLA_HEADS):
    b, s, _ = q3.shape
    assert MLA_HEADS % heads == 0 and tq == tk and tk % CHUNK == 0
    kern = functools.partial(_mla_attn_kernel, tq=tq, tk=tk, heads=heads,
                             scale=(MLA_NOPE + MLA_ROPE) ** -0.5)
    return pl.pallas_call(
        kern,
        grid=(b, MLA_HEADS // heads, s // tq),
        in_specs=[pl.BlockSpec((None, tq, heads * MLA_QK_PAD), lambda bb, g, i: (bb, i, g)),
                  pl.BlockSpec((None, s, heads * MLA_QK_PAD), lambda bb, g, i: (bb, 0, g)),
                  pl.BlockSpec((None, s, heads * MLA_V), lambda bb, g, i: (bb, 0, g))],
        out_specs=pl.BlockSpec((None, tq, heads * MLA_V), lambda bb, g, i: (bb, i, g)),
        out_shape=jax.ShapeDtypeStruct((b, s, MLA_WIDTH), BF16),
        compiler_params=_cparams(3),
        name="mla_attention",
    )(q3, k3, v3)


def _mix_router_kernel(sb_ref, mla_ref, x_ref, lig_ref, lib_ref, sg_ref, mg_ref, wo_ref,
                       lmg_ref, lmb_ref, wr_ref, br_ref,
                       h2_ref, h2p_ref, sel_ref, gate_ref, cnt_ref, run_ref, *, tm):
    step = pl.program_id(0)

    @pl.when(step == 0)
    def _():
        run_ref[...] = jnp.zeros_like(run_ref)

    a = _rms_norm(sb_ref[...].astype(F32), sg_ref[...]).astype(BF16)
    bm = _rms_norm(mla_ref[...].astype(F32), mg_ref[...]).astype(BF16)
    mix = (jnp.dot(a, wo_ref[:SB_WIDTH, :], preferred_element_type=F32)
           + jnp.dot(bm, wo_ref[SB_WIDTH:, :], preferred_element_type=F32))
    h = _layer_norm(x_ref[...], lig_ref[...], lib_ref[...])
    h2 = _layer_norm(DEEPNORM_ALPHA * h + mix, lmg_ref[...], lmb_ref[...])
    h2_ref[...] = h2
    h2p_ref[...] = _pack_halves(h2[:, :HALF], h2[:, HALF:])

    h2_hi = h2.astype(BF16)
    h2_lo = (h2 - h2_hi.astype(F32)).astype(BF16)
    hw = jnp.dot(h2_hi, wr_ref[...], preferred_element_type=F32)
    logits = (hw[:, :LANES] + hw[:, LANES:]
              + jnp.dot(h2_lo, wr_ref[:, :LANES], preferred_element_type=F32) + br_ref[...])
    lane = lax.broadcasted_iota(jnp.int32, (tm, LANES), 1)
    lane_f = lane.astype(F32)
    logits = jnp.where(lane < N_EXPERTS, logits, -jnp.inf)
    vals, hots = [], []
    sel = jnp.zeros((tm, LANES), F32)
    for k in range(TOP_K):
        m = jnp.max(logits, -1, keepdims=True)
        idx = jnp.min(jnp.where(logits == m, lane_f, float(LANES)), -1, keepdims=True)
        hot = lane_f == idx
        logits = jnp.where(hot, -jnp.inf, logits)
        vals.append(m)
        hots.append(hot)
        sel = jnp.where(lane == k, idx, sel)
    exps = [jnp.exp(v - vals[0]) for v in vals]
    denom = exps[0] + exps[1] + exps[2] + exps[3]
    gates = jnp.zeros((tm, LANES), F32)
    for k in range(TOP_K):
        gates = jnp.where(lane == k, exps[k] / denom, gates)
    gate_ref[...] = gates

    onehot = (hots[0] | hots[1] | hots[2] | hots[3]).astype(F32)
    r_i = lax.broadcasted_iota(jnp.int32, (tm, tm), 0)
    c_i = lax.broadcasted_iota(jnp.int32, (tm, tm), 1)
    before = (c_i < r_i).astype(BF16)
    rank = jnp.dot(before, onehot.astype(BF16), preferred_element_type=F32) + run_ref[...]
    for k in range(TOP_K):
        pos = jnp.sum(jnp.where(hots[k], rank, 0.0), -1, keepdims=True)
        sel = jnp.where(lane == TOP_K + k, pos, sel)
    sel_ref[...] = sel.T[:2 * TOP_K, :]
    run_ref[...] = run_ref[...] + jnp.sum(onehot, 0, keepdims=True)
    cnt_ref[...] = run_ref[...]


def _mix_router(sb_o, mla_o, x2, lig, lib, sg, mg, w_o, lmg, lmb, w_r, b_r, tm=512):
    t, d = x2.shape
    full = lambda a: pl.BlockSpec(a.shape, lambda i: (0,) * a.ndim)
    row = lambda w: pl.BlockSpec((tm, w), lambda i: (i, 0))
    return pl.pallas_call(
        functools.partial(_mix_router_kernel, tm=tm),
        grid=(t // tm,),
        in_specs=[row(SB_WIDTH), row(MLA_WIDTH), row(d), full(lig), full(lib), full(sg), full(mg),
                  full(w_o), full(lmg), full(lmb), full(w_r), full(b_r)],
        out_specs=[row(d), row(HALF), pl.BlockSpec((2 * TOP_K, tm), lambda i: (0, i)), row(LANES),
                   pl.BlockSpec((1, LANES), lambda i: (0, 0))],
        out_shape=[jax.ShapeDtypeStruct((t, d), F32), jax.ShapeDtypeStruct((t, HALF), U32),
                   jax.ShapeDtypeStruct((2 * TOP_K, t), F32), jax.ShapeDtypeStruct((t, LANES), F32),
                   jax.ShapeDtypeStruct((1, LANES), F32)],
        scratch_shapes=[pltpu.VMEM((1, LANES), F32)],
        compiler_params=_cparams(1),
        name="mix_router",
    )(sb_o, mla_o, x2, lig, lib, sg, mg, w_o, lmg, lmb, w_r, b_r)


ROWS_PER_STEP = 4096


def _wait_rows(like_hbm, sem, groups):
    pltpu.make_async_copy(like_hbm.at[pl.ds(0, groups)], like_hbm.at[pl.ds(0, groups)], sem).wait()


def _row_of(ref3, r):
    if isinstance(r, tuple):
        return ref3.at[r[0], pl.ds(r[1], 1), :]
    return ref3.at[lax.shift_right_logical(r, 3), pl.ds(r & (SUBLANES - 1), 1), :]


def _gather_rows_kernel(idx_ref, src_hbm, out_ref, sem, *, rows):
    def issue(g, _):
        r0 = pl.multiple_of(g * SUBLANES, SUBLANES)
        for k in range(SUBLANES):
            pltpu.make_async_copy(_row_of(src_hbm, idx_ref[0, 0, r0 + k]),
                                  _row_of(out_ref, (g, k)), sem).start()
        return 0

    lax.fori_loop(0, rows // SUBLANES, issue, 0)
    _wait_rows(src_hbm, sem, rows // SUBLANES)


def _scatter_rows_kernel(idx_ref, src_ref, out_hbm, zero_ref, sem, *, rows, src_steps):
    step = pl.program_id(0)

    @pl.when(step == 0)
    def _():
        zero_ref[...] = jnp.zeros_like(zero_ref)

    @pl.when(step < src_steps)
    def _():
        def issue(g, _):
            r0 = pl.multiple_of(g * SUBLANES, SUBLANES)
            for k in range(SUBLANES):
                pltpu.make_async_copy(_row_of(src_ref, (g, k)),
                                      _row_of(out_hbm, idx_ref[0, 0, r0 + k]), sem).start()
            return 0
        lax.fori_loop(0, rows // SUBLANES, issue, 0)

    @pl.when(step >= src_steps)
    def _():
        def issue(a, _):
            pltpu.make_async_copy(_row_of(zero_ref, (0, 0)),
                                  _row_of(out_hbm, idx_ref[0, 0, a]), sem).start()
            return 0
        lax.fori_loop(0, rows, issue, 0, unroll=8)

    _wait_rows(out_hbm, sem, rows // SUBLANES)


def _gather_rows(src, idx):
    rows = ROWS_PER_STEP
    n, w = idx.shape[0], src.shape[1]
    assert n % rows == 0 and src.shape[0] >= rows and src.shape[0] % SUBLANES == 0
    out = pl.pallas_call(
        functools.partial(_gather_rows_kernel, rows=rows),
        grid=(n // rows,),
        in_specs=[pl.BlockSpec((1, 1, rows), lambda i: (i, 0, 0), memory_space=pltpu.SMEM),
                  pl.BlockSpec(memory_space=pl.ANY)],
        out_specs=pl.BlockSpec((rows // SUBLANES, SUBLANES, w), lambda i: (i, 0, 0)),
        out_shape=jax.ShapeDtypeStruct((n // SUBLANES, SUBLANES, w), src.dtype),
        scratch_shapes=[pltpu.SemaphoreType.DMA(())],
        compiler_params=_cparams(1),
        name="gather_rows",
    )(idx.reshape(n // rows, 1, rows), src.reshape(-1, SUBLANES, w))
    return out.reshape(n, w)


def _scatter_rows(src, passes, idx, out_rows):
    rows = ROWS_PER_STEP
    n, w = idx.shape[0], src.shape[1]
    src_blocks = src.shape[0] // rows
    assert src.shape[0] % rows == 0 and n % rows == 0 and n == out_rows
    src_steps = passes * src_blocks
    kern = functools.partial(_scatter_rows_kernel, rows=rows, src_steps=src_steps)

    def src_block(i):
        return jnp.minimum(i // passes, src_blocks - 1), 0

    def idx_block(i):
        return jnp.where(i < src_steps, lax.rem(i, passes) * src_blocks + i // passes, i), 0, 0

    out = pl.pallas_call(
        kern,
        grid=(n // rows,),
        in_specs=[pl.BlockSpec((1, 1, rows), idx_block, memory_space=pltpu.SMEM),
                  pl.BlockSpec((rows // SUBLANES, SUBLANES, w), lambda i: src_block(i) + (0,))],
        out_specs=pl.BlockSpec(memory_space=pl.ANY),
        out_shape=jax.ShapeDtypeStruct((out_rows // SUBLANES, SUBLANES, w), src.dtype),
        scratch_shapes=[pltpu.VMEM((1, SUBLANES, w), src.dtype), pltpu.SemaphoreType.DMA(())],
        compiler_params=_cparams(1),
        name="scatter_rows",
    )(idx.reshape(n // rows, 1, rows), src.reshape(-1, SUBLANES, w))
    return out.reshape(out_rows, w)


MOE_ROWS = 1536
MOE_SUB = 512
MOE_TF = 256
MOE_NF = D_FF // MOE_TF
MOE_ITEMS = N_EXPERTS + (N_EXPERTS * (MOE_BLOCK - 1) + 8192 * TOP_K) // MOE_ROWS + 1


MOE_GU_SLOTS = 2
MOE_WD_SLOTS = 3


def _moe_kernel(ie_ref, is_ref, nb_ref, tail_ref, xs_hbm, wgu_hbm, wd_hbm, bgu_ref, bd_ref,
                ys_hbm, pkin_ref, pkout_ref, x_ref, acc_ref, wg_buf, wu_buf, wd_buf,
                pend_ref, sem_in, sem_out, sem_w):
    i = pl.program_id(0)
    nb = nb_ref[i]
    start = is_ref[i]
    n_out_blocks = ys_hbm.shape[0] // MOE_BLOCK

    def weight_copies(item, f):
        e = ie_ref[item]
        g = item * MOE_NF + f
        gs = lax.rem(g, MOE_GU_SLOTS)
        ws = lax.rem(g, MOE_WD_SLOTS)
        c0 = pl.multiple_of(f * MOE_TF, MOE_TF)
        return (pltpu.make_async_copy(wgu_hbm.at[e, :, pl.ds(c0, MOE_TF)], wg_buf.at[gs],
                                      sem_w.at[gs]),
                pltpu.make_async_copy(wgu_hbm.at[e, :, pl.ds(D_FF + c0, MOE_TF)], wu_buf.at[gs],
                                      sem_w.at[gs]),
                pltpu.make_async_copy(wd_hbm.at[e, pl.ds(c0, MOE_TF), :], wd_buf.at[ws],
                                      sem_w.at[MOE_GU_SLOTS + ws]))

    def vm_block(ref, b):
        return ref.at[pl.ds(pl.multiple_of(b * MOE_BLOCK, MOE_BLOCK), MOE_BLOCK)]

    def hbm_block(ref, first_row, b):
        return ref.at[pl.ds(pl.multiple_of(first_row + b * MOE_BLOCK, MOE_BLOCK), MOE_BLOCK)]

    def in_copy(first_row, b):
        return pltpu.make_async_copy(hbm_block(xs_hbm, first_row, b), vm_block(pkin_ref, b), sem_in)

    def out_copy(first_row, b):
        return pltpu.make_async_copy(vm_block(pkout_ref, b), hbm_block(ys_hbm, first_row, b), sem_out)

    def for_blocks(n, fn):
        def body(b, _):
            fn(b)
            return 0
        lax.fori_loop(0, n, body, 0)

    def next_item_valid():
        return (i + 1 < MOE_ITEMS) & (nb_ref[jnp.minimum(i + 1, MOE_ITEMS - 1)] > 0)

    @pl.when(i == 0)
    def _():
        x_ref[...] = jnp.zeros_like(x_ref)
        acc_ref[...] = jnp.zeros_like(acc_ref)
        pend_ref[0] = 0
        pend_ref[1] = 0
        for_blocks(nb, lambda b: in_copy(start, b).start())
        for c in weight_copies(0, 0):
            c.start()

    @pl.when(nb > 0)
    def _():
        def unpack(b):
            r0 = pl.multiple_of(b * MOE_BLOCK, MOE_BLOCK)
            lo, hi = _unpack_halves(pkin_ref[pl.ds(r0, MOE_BLOCK), :])
            x_ref[pl.ds(r0, MOE_BLOCK), :HALF] = lo.astype(BF16)
            x_ref[pl.ds(r0, MOE_BLOCK), HALF:] = hi.astype(BF16)

        for_blocks(nb, lambda b: in_copy(start, b).wait())
        for_blocks(nb, unpack)

        @pl.when(next_item_valid())
        def _():
            nxt = jnp.minimum(i + 1, MOE_ITEMS - 1)
            for_blocks(nb_ref[nxt], lambda b: in_copy(is_ref[nxt], b).start())

        n_sub = (nb * MOE_BLOCK) // MOE_SUB
        n_tail = nb - n_sub * (MOE_SUB // MOE_BLOCK)

        def gate_up(f, r0, rows, gs):
            x = x_ref[pl.ds(pl.multiple_of(r0, MOE_BLOCK), rows), :]
            g = (jnp.dot(x, wg_buf[gs].astype(BF16), preferred_element_type=F32)
                 + bgu_ref[pl.ds(f, 1), :])
            u = (jnp.dot(x, wu_buf[gs].astype(BF16), preferred_element_type=F32)
                 + bgu_ref[pl.ds(MOE_NF + f, 1), :])
            g = jnp.minimum(g, SWIGLU_LIMIT)
            u = jnp.clip(u, -SWIGLU_LIMIT, SWIGLU_LIMIT)
            return ((u + 1.0) * (g * jax.nn.sigmoid(SWIGLU_ALPHA * g))).astype(BF16)

        def down(act, r0, ws, first):
            rows = act.shape[0]
            r0 = pl.multiple_of(r0, MOE_BLOCK)
            y = jnp.dot(act, wd_buf[ws].astype(BF16), preferred_element_type=F32)
            prev = jnp.where(first == 1, bd_ref[...], acc_ref[pl.ds(r0, rows), :])
            acc_ref[pl.ds(r0, rows), :] = prev + y

        def tile(f, carry):
            for c in weight_copies(i, f):
                c.wait()

            @pl.when(f + 1 < MOE_NF)
            def _():
                for c in weight_copies(i, f + 1):
                    c.start()

            @pl.when((f + 1 == MOE_NF) & next_item_valid())
            def _():
                for c in weight_copies(i + 1, 0):
                    c.start()

            g = i * MOE_NF + f
            gs = lax.rem(g, MOE_GU_SLOTS)
            ws = lax.rem(g, MOE_WD_SLOTS)
            first = jnp.asarray(f == 0, jnp.int32)

            def sub(s, c):
                act, r_prev, ws_prev, first_prev = c
                down(act, r_prev, ws_prev, first_prev)
                return gate_up(f, s * MOE_SUB, MOE_SUB, gs), s * MOE_SUB, ws, first

            carry = lax.fori_loop(0, n_sub, sub, carry)

            def tail(t, _):
                r0 = n_sub * MOE_SUB + t * MOE_BLOCK
                down(gate_up(f, r0, MOE_BLOCK, gs), r0, ws, first)
                return 0

            lax.fori_loop(0, n_tail, tail, 0)
            return carry

        assert MOE_ROWS % MOE_SUB == 0 and MOE_SUB % MOE_BLOCK == 0
        carry0 = (jnp.zeros((MOE_SUB, MOE_TF), BF16), jnp.int32(MOE_ROWS),
                  lax.rem(i * MOE_NF, MOE_WD_SLOTS), jnp.int32(0))
        down(*lax.fori_loop(0, MOE_NF, tile, carry0))

    def drain_pending_store():
        for_blocks(pend_ref[1], lambda b: out_copy(pend_ref[0], b).wait())
        pend_ref[1] = 0

    @pl.when(nb > 0)
    def _():
        def pack(b):
            r0 = pl.multiple_of(b * MOE_BLOCK, MOE_BLOCK)
            y = acc_ref[pl.ds(r0, MOE_BLOCK), :]
            pkout_ref[pl.ds(r0, MOE_BLOCK), :] = _pack_halves(y[:, :HALF], y[:, HALF:])

        drain_pending_store()
        for_blocks(nb, pack)
        for_blocks(nb, lambda b: out_copy(start, b).start())
        pend_ref[0] = start
        pend_ref[1] = nb

    @pl.when(i == MOE_ITEMS - 1)
    def _():
        drain_pending_store()
        pkout_ref[:MOE_BLOCK, :] = jnp.zeros((MOE_BLOCK, HALF), U32)

        def tail_copy(b):
            return pltpu.make_async_copy(
                pkout_ref.at[pl.ds(0, MOE_BLOCK)],
                ys_hbm.at[pl.ds(pl.multiple_of(b * MOE_BLOCK, MOE_BLOCK), MOE_BLOCK)], sem_out)

        def tail_blocks(fn):
            def body(b, _):
                fn(b)
                return 0
            lax.fori_loop(tail_ref[0], n_out_blocks, body, 0)

        tail_blocks(lambda b: tail_copy(b).start())
        tail_blocks(lambda b: tail_copy(b).wait())


def _moe_ffn(item_e, item_start, item_nb, tail_blk, xs, w_gu, b_gu, w_d, b_d, n_rows):
    by_expert = lambda i, ie, is_, nb, tail: (ie[i], 0, 0)
    grid_spec = pltpu.PrefetchScalarGridSpec(
        num_scalar_prefetch=4,
        grid=(MOE_ITEMS,),
        in_specs=[pl.BlockSpec(memory_space=pl.ANY),
                  pl.BlockSpec(memory_space=pl.ANY),
                  pl.BlockSpec(memory_space=pl.ANY),
                  pl.BlockSpec((None, 2 * MOE_NF, MOE_TF), by_expert),
                  pl.BlockSpec((None, 1, D_MODEL), by_expert)],
        out_specs=pl.BlockSpec(memory_space=pl.ANY),
        scratch_shapes=[pltpu.VMEM((MOE_ROWS, HALF), U32),
                        pltpu.VMEM((MOE_ROWS, HALF), U32),
                        pltpu.VMEM((MOE_ROWS, D_MODEL), BF16),
                        pltpu.VMEM((MOE_ROWS + MOE_SUB, D_MODEL), F32),
                        pltpu.VMEM((MOE_GU_SLOTS, D_MODEL, MOE_TF), F32),
                        pltpu.VMEM((MOE_GU_SLOTS, D_MODEL, MOE_TF), F32),
                        pltpu.VMEM((MOE_WD_SLOTS, MOE_TF, D_MODEL), F32),
                        pltpu.SMEM((2,), jnp.int32),
                        pltpu.SemaphoreType.DMA(()),
                        pltpu.SemaphoreType.DMA(()),
                        pltpu.SemaphoreType.DMA((MOE_GU_SLOTS + MOE_WD_SLOTS,))])
    return pl.pallas_call(
        _moe_kernel,
        grid_spec=grid_spec,
        out_shape=jax.ShapeDtypeStruct((n_rows, HALF), U32),
        compiler_params=_cparams(1),
        name="moe_ffn",
    )(item_e, item_start, item_nb, tail_blk, xs, w_gu, w_d, b_gu, b_d)


def _combine_kernel(y0_ref, y1_ref, y2_ref, y3_ref, gate_ref, h2_ref, g_ref, b_ref, o_ref):
    h2 = h2_ref[...]
    lo = DEEPNORM_ALPHA * h2[:, :HALF]
    hi = DEEPNORM_ALPHA * h2[:, HALF:]
    for k, y_ref in enumerate((y0_ref, y1_ref, y2_ref, y3_ref)):
        ylo, yhi = _unpack_halves(y_ref[...])
        gk = gate_ref[:, k:k + 1]
        lo = lo + gk * ylo
        hi = hi + gk * yhi
    mu = (jnp.sum(lo, -1, keepdims=True) + jnp.sum(hi, -1, keepdims=True)) * (1.0 / D_MODEL)
    lo = lo - mu
    hi = hi - mu
    var = (jnp.sum(lo * lo, -1, keepdims=True) + jnp.sum(hi * hi, -1, keepdims=True)) * (1.0 / D_MODEL)
    inv = lax.rsqrt(var + LN_EPS)
    o_ref[:, :HALF] = lo * inv * g_ref[:, :HALF] + b_ref[:, :HALF]
    o_ref[:, HALF:] = hi * inv * g_ref[:, HALF:] + b_ref[:, HALF:]


def _combine(yg, gates, h2, g, b, tm=256):
    t, d = h2.shape
    full = lambda a: pl.BlockSpec(a.shape, lambda i: (0,) * a.ndim)
    return pl.pallas_call(
        _combine_kernel,
        grid=(t // tm,),
        in_specs=[pl.BlockSpec((None, tm, HALF), functools.partial(lambda k, i: (k, i, 0), k))
                  for k in range(TOP_K)] + [
                  pl.BlockSpec((tm, LANES), lambda i: (i, 0)),
                  pl.BlockSpec((tm, d), lambda i: (i, 0)), full(g), full(b)],
        out_specs=pl.BlockSpec((tm, d), lambda i: (i, 0)),
        out_shape=jax.ShapeDtypeStruct((t, d), F32),
        compiler_params=_cparams(1),
        name="combine_ln",
    )(yg, yg, yg, yg, gates, h2, g, b)


def _routing_tables(sel, counts_f):
    t = sel.shape[1]
    idx = sel[:TOP_K].astype(jnp.int32)
    pos = sel[TOP_K:].astype(jnp.int32)
    counts = counts_f[0, :N_EXPERTS].astype(jnp.int32)
    padded = ((counts + MOE_BLOCK - 1) // MOE_BLOCK) * MOE_BLOCK
    pends = jnp.cumsum(padded)
    pstarts = pends - padded
    experts = jnp.arange(N_EXPERTS, dtype=jnp.int32)[:, None, None]
    first_row = jnp.sum(jnp.where(idx[None] == experts, pstarts[:, None, None], 0), axis=0)
    dest = (first_row + pos).reshape(-1)
    n_rows = t * TOP_K + N_EXPERTS * MOE_BLOCK
    j = jnp.arange(MOE_BLOCK, dtype=jnp.int32)[None, :]
    is_pad = j < (padded - counts)[:, None]
    n_after = jnp.cumsum((~is_pad).reshape(-1).astype(jnp.int32)).reshape(is_pad.shape) - 1
    zero_rows = jnp.where(is_pad, (pstarts + counts)[:, None] + j, pends[-1] + n_after)
    row_of = jnp.concatenate([dest, zero_rows.reshape(-1)])

    nchunk = (padded + MOE_ROWS - 1) // MOE_ROWS
    cend = jnp.cumsum(nchunk)
    cstart = cend - nchunk
    total = cend[-1]
    it = jnp.arange(MOE_ITEMS, dtype=jnp.int32)
    valid = it < total
    it_c = jnp.minimum(it, total - 1)
    ie = jnp.searchsorted(cend, it_c, side="right").astype(jnp.int32)
    c = it_c - cstart[ie]
    istart = (pstarts[ie] + c * MOE_ROWS).astype(jnp.int32)
    inb = jnp.where(valid, jnp.minimum(MOE_ROWS, padded[ie] - c * MOE_ROWS) // MOE_BLOCK, 0)
    tail_blk = (pends[-1] // MOE_BLOCK).astype(jnp.int32).reshape(1)
    return dest, row_of, ie, istart, inb.astype(jnp.int32), tail_blk, n_rows


def kernel(x, positions, ln_in_g, ln_in_b, w_in, q_a_norm, w_q_b, kv_a_norm, w_kv_b, sb_out_norm,
           mla_out_norm, w_o, ln_mix_g, ln_mix_b, w_router, b_router, w_gate_up, b_gate_up, w_down,
           b_down, ln_ffn_g, ln_ffn_b):
    bsz, seq, d = x.shape
    t = bsz * seq
    x2 = x.reshape(t, d)
    row = lambda v: v.reshape(1, -1)

    w_in0 = w_in[0]
    w_in_b = w_in0.astype(BF16)
    w_lat = jnp.pad(w_in_b[:, 3 * SB_WIDTH:], ((0, 0), (0, LANES - MLA_ROPE)))
    wq = w_q_b[0].reshape(MLA_Q_RANK, MLA_HEADS, MLA_NOPE + MLA_ROPE)
    w_qn = wq[:, :, :MLA_NOPE].reshape(MLA_Q_RANK, -1).astype(BF16)
    w_qp = jnp.pad(wq[:, :, MLA_NOPE:], ((0, 0), (0, 0), (0, LANES - MLA_ROPE))
                   ).reshape(MLA_Q_RANK, -1).astype(BF16)
    wkv = w_kv_b[0].reshape(MLA_KV_RANK, MLA_HEADS, MLA_NOPE + MLA_V)
    w_kn = wkv[:, :, :MLA_NOPE].reshape(MLA_KV_RANK, -1).astype(BF16)
    w_v = wkv[:, :, MLA_NOPE:].reshape(MLA_KV_RANK, -1).astype(BF16)
    w_r32 = jnp.pad(w_router[0], ((0, 0), (0, LANES - N_EXPERTS)))
    w_r_hi = w_r32.astype(BF16)
    w_r = jnp.concatenate([w_r_hi, (w_r32 - w_r_hi.astype(F32)).astype(BF16)], axis=1)
    b_r = jnp.pad(b_router[0], (0, LANES - N_EXPERTS)).reshape(1, LANES)
    half = MLA_ROPE // 2
    inv_freq = ROPE_THETA ** (-(jnp.arange(half, dtype=F32) * 2.0 / MLA_ROPE))
    invf = jnp.concatenate([inv_freq, inv_freq, jnp.zeros((LANES - MLA_ROPE,), F32)]).reshape(1, LANES)
    pos_f = positions.reshape(t, 1).astype(F32)

    for l in range(DEPTH):
        u_sb, h_b = _ln_inproj(x2, row(ln_in_g), row(ln_in_b), w_in_b, 3 * SB_WIDTH)
        q_m, k_m, v_m = _mla_prep(h_b, pos_f, invf, w_lat, row(q_a_norm[l]), row(kv_a_norm[l]),
                                  w_qn, w_qp, w_kn, w_v)
        sb_o = _sb_attention(u_sb.reshape(bsz, seq, -1)).reshape(t, -1)
        mla_o = _mla_attention(q_m.reshape(bsz, seq, -1), k_m.reshape(bsz, seq, -1),
                               v_m.reshape(bsz, seq, -1)).reshape(t, -1)
        h2, h2p, sel, gates, counts_f = _mix_router(
            sb_o, mla_o, x2, row(ln_in_g), row(ln_in_b), row(sb_out_norm[l]), row(mla_out_norm[l]),
            w_o[l].astype(BF16), row(ln_mix_g[l]), row(ln_mix_b[l]), w_r, b_r)
        dest, row_of, ie, istart, inb, tail_blk, n_rows = _routing_tables(sel, counts_f)
        xs = _scatter_rows(h2p, TOP_K, row_of, n_rows)
        ys = _moe_ffn(ie, istart, inb, tail_blk, xs, w_gate_up[l],
                      b_gate_up[l].reshape(N_EXPERTS, 2 * MOE_NF, MOE_TF),
                      w_down[l], b_down[l].reshape(N_EXPERTS, 1, -1), n_rows)
        yg = _gather_rows(ys, dest).reshape(TOP_K, t, HALF)
        out = _combine(yg, gates, h2, row(ln_ffn_g[l]), row(ln_ffn_b[l]))
    return out.reshape(bsz, seq, d)
```
